```python
import math
import jax, jax.numpy as jnp
from jax import lax
import numpy as np

D_MODEL = 2048
BATCH = 4
SEQ = 2048
DEPTH = 4
DEC_BATCH = 8
DEC_SEQ = 8
PAST_LEN = 16384
PAGE_SIZE = 128

D_MIX = D_MODEL
D_A = D_MIX // 2
H_A = D_A // 128
DV_A = D_A // H_A
DK_A = DV_A // 2
D_B = D_MIX // 4
H_B = 4
DK_B = D_B // H_B
DV_B = D_B // H_B
D_C = D_MIX - D_A - D_B
H_C = 4
DV_C = D_C // H_C
DK_C = DV_C // 2
GLA_RANK = 16
GLA_TAU = 16.0
N_BUCKETS = 32
MAX_DISTANCE = 128
D_FF = ((8 * D_MODEL // 3 + 255) // 256) * 256
CONV_W = 3
Q_BLOCK = 128
CHUNK = 64
EPS = 1e-6
NEG = -1e30

COL_SIZES = (H_A * 2 * DK_A, H_A * 2 * DK_A, H_A * DV_A,
             H_B * DK_B, H_B * DK_B, H_B * DV_B, H_B * DV_B, H_B, H_B,
             H_C * DK_C, H_C * DK_C, H_C * DV_C, H_C * DV_C, GLA_RANK)
D_IN = sum(COL_SIZES)

kernel_name = 'hybrid_diffattn_mlstm_gla_convffn_step'


def split_cols(z):
    cuts = []
    pos = 0
    for s in COL_SIZES[:-1]:
        pos += s
        cuts.append(pos)
    return jnp.split(z, cuts, axis=-1)


def rmsnorm(x, g):
    xf = x.astype(jnp.float32)
    y = xf * lax.rsqrt(jnp.mean(xf * xf, axis=-1, keepdims=True) + EPS)
    return (y * g.astype(jnp.float32)).astype(x.dtype)


def rel_bucket(n):
    max_exact = N_BUCKETS // 2
    nf = jnp.maximum(n, max_exact).astype(jnp.float32)
    large = max_exact + (jnp.log(nf / max_exact) / math.log(MAX_DISTANCE / max_exact)
                         * (N_BUCKETS - max_exact)).astype(jnp.int32)
    large = jnp.minimum(large, N_BUCKETS - 1)
    return jnp.where(n < max_exact, jnp.maximum(n, 0), large)


def _diff_attn_block(q, k, v, q_pos, k_pos, lam, rel_bias):
    s = jnp.einsum('bqhmd,bkhmd->bhmqk', q, k, preferred_element_type=jnp.float32) * (DK_A ** -0.5)
    n = q_pos[:, None] - k_pos[None, :]
    bias = jnp.transpose(rel_bias[rel_bucket(n)].astype(jnp.float32), (2, 0, 1))
    s = jnp.where((n >= 0)[None, None, None], s + bias[None, :, None], NEG)
    p = jax.nn.softmax(s, axis=-1)
    a = p[:, :, 0] - lam * p[:, :, 1]
    return jnp.einsum('bhqk,bkhe->bqhe', a.astype(v.dtype), v)


def diff_attention(q, k, v, q_pos, k_pos, lam, rel_bias):
    B, Sq = q.shape[0], q.shape[1]
    if Sq > Q_BLOCK and Sq % Q_BLOCK == 0:
        nb = Sq // Q_BLOCK
        qb = jnp.moveaxis(q.reshape(B, nb, Q_BLOCK, *q.shape[2:]), 1, 0)
        pb = q_pos.reshape(nb, Q_BLOCK)
        ob = lax.map(lambda a: _diff_attn_block(a[0], k, v, a[1], k_pos, lam, rel_bias), (qb, pb))
        return jnp.moveaxis(ob, 0, 1).reshape(B, Sq, *ob.shape[3:])
    return _diff_attn_block(q, k, v, q_pos, k_pos, lam, rel_bias)


def _chunks(x, L):
    B, S = x.shape[0], x.shape[1]
    return jnp.moveaxis(x.reshape(B, S // L, L, *x.shape[2:]), 1, 0)


def mlstm_scan(q, k, v, li, lf, C0, n0, m0):
    B, S, H, _ = q.shape
    L = CHUNK if S % CHUNK == 0 else S
    causal = jnp.tril(jnp.ones((L, L), dtype=bool))
    f32 = jnp.float32

    def step(carry, inp):
        C, n, m = carry
        qc, kc, vc, lic, lfc = inp
        qc, kc, vc = qc.astype(f32), kc.astype(f32), vc.astype(f32)
        b = jnp.cumsum(lfc, axis=1)
        inter = b + m[:, None, :]
        D = b[:, :, None, :] - b[:, None, :, :] + lic[:, None, :, :]
        D = jnp.where(causal[None, :, :, None], D, NEG)
        mt = jnp.maximum(inter, jnp.max(D, axis=2))
        wi = jnp.exp(inter - mt)
        P = jnp.einsum('bthd,bshd->btsh', qc, kc) * jnp.exp(D - mt[:, :, None, :])
        num = wi[..., None] * jnp.einsum('bthd,bhde->bthe', qc, C) + jnp.einsum('btsh,bshe->bthe', P, vc)
        den = wi * jnp.einsum('bthd,bhd->bth', qc, n) + jnp.sum(P, axis=2)
        h = num / jnp.maximum(jnp.abs(den), jnp.exp(-mt))[..., None]
        m_new = mt[:, -1]
        a = jnp.exp(b[:, -1] + m - m_new)
        ws = jnp.exp(b[:, -1:] - b + lic - m_new[:, None])
        C_new = a[..., None, None] * C + jnp.einsum('bsh,bshd,bshe->bhde', ws, kc, vc)
        n_new = a[..., None] * n + jnp.einsum('bsh,bshd->bhd', ws, kc)
        return (C_new, n_new, m_new), h

    init = (C0.astype(f32), n0.astype(f32), m0.astype(f32))
    (C, n, m), h = lax.scan(step, init, (_chunks(q, L), _chunks(k, L), _chunks(v, L),
                                         _chunks(li, L), _chunks(lf, L)))
    h = jnp.moveaxis(h, 0, 1).reshape(B, S, H, v.shape[-1])
    return h, C, n, m


def gla_scan(q, k, v, la, S0):
    B, S, H, _ = q.shape
    L = CHUNK if S % CHUNK == 0 else S
    causal = jnp.tril(jnp.ones((L, L), dtype=bool))
    f32 = jnp.float32

    def step(St, inp):
        qc, kc, vc, lac = inp
        qc, kc, vc = qc.astype(f32), kc.astype(f32), vc.astype(f32)
        Bc = jnp.cumsum(lac, axis=1)
        inter = jnp.einsum('bthd,bhde->bthe', qc * jnp.exp(Bc), St)
        diff = Bc[:, :, None] - Bc[:, None, :]
        dec = jnp.exp(jnp.where(causal[None, :, :, None, None], diff, NEG))
        A = jnp.einsum('bthd,bshd,btshd->btsh', qc, kc, dec)
        o = inter + jnp.einsum('btsh,bshe->bthe', A, vc)
        tot = Bc[:, -1]
        S_new = jnp.exp(tot)[..., None] * St + jnp.einsum('bshd,bshe->bhde', kc * jnp.exp(tot[:, None] - Bc), vc)
        return S_new, o

    S_fin, o = lax.scan(step, S0.astype(f32), (_chunks(q, L), _chunks(k, L), _chunks(v, L), _chunks(la, L)))
    o = jnp.moveaxis(o, 0, 1).reshape(B, S, H, v.shape[-1])
    return o, S_fin


def layer(x, l, pos0, k_past, v_past, mC, mn, mm, gS, conv_buf, weights):
    (norm_mix_g, w_in, q_norm_g, k_norm_g, diff_lambda, diff_subln_g, rel_bias,
     mlstm_gate_b, mlstm_norm_g, gla_w_alpha, gla_b_alpha, gla_norm_g, w_out,
     norm_ffn_g, ffn_w_up, ffn_conv_w, ffn_conv_b, ffn_w_down) = weights
    B, S, _ = x.shape
    f32 = jnp.float32
    h = rmsnorm(x, norm_mix_g[l])
    (aq, ak, av, bq, bk, bv, bo, bi, bf, cq, ck, cv, cg, ca) = split_cols(h @ w_in[l])

    qa = rmsnorm(aq.reshape(B, S, H_A, 2, DK_A), q_norm_g[l])
    ka = rmsnorm(ak.reshape(B, S, H_A, 2, DK_A), k_norm_g[l])
    va = av.reshape(B, S, H_A, DV_A)
    lam_init = 0.8 - 0.6 * math.exp(-0.3 * l)
    dl = diff_lambda[l].astype(f32)
    lam = jnp.exp(jnp.sum(dl[0] * dl[1])) - jnp.exp(jnp.sum(dl[2] * dl[3])) + lam_init
    q_pos = pos0 + jnp.arange(S, dtype=jnp.int32)
    if k_past is None:
        k_all, v_all, k_pos = ka, va, q_pos
    else:
        k_all = jnp.concatenate([k_past.reshape(B, -1, H_A, 2, DK_A).astype(ka.dtype), ka], axis=1)
        v_all = jnp.concatenate([v_past.astype(va.dtype), va], axis=1)
        k_pos = jnp.arange(k_all.shape[1], dtype=jnp.int32)
    oa = diff_attention(qa, k_all, v_all, q_pos, k_pos, lam, rel_bias)
    oa = (rmsnorm(oa, diff_subln_g[l]) * (1.0 - lam_init)).astype(x.dtype)

    gb = mlstm_gate_b[l].astype(f32)
    li = bi.astype(f32) + gb[0]
    lf = jax.nn.log_sigmoid(bf.astype(f32) + gb[1])
    hb, mC, mn, mm = mlstm_scan(bq.reshape(B, S, H_B, DK_B), bk.reshape(B, S, H_B, DK_B) * (DK_B ** -0.5),
                                bv.reshape(B, S, H_B, DV_B), li, lf, mC, mn, mm)
    ob = rmsnorm(jax.nn.sigmoid(bo.astype(f32)).reshape(B, S, H_B, DV_B) * hb, mlstm_norm_g[l]).astype(x.dtype)

    la = jax.nn.log_sigmoid((ca @ gla_w_alpha[l] + gla_b_alpha[l]).astype(f32)) / GLA_TAU
    oc, gS = gla_scan(cq.reshape(B, S, H_C, DK_C) * (DK_C ** -0.5), ck.reshape(B, S, H_C, DK_C),
                      cv.reshape(B, S, H_C, DV_C), la.reshape(B, S, H_C, DK_C), gS)
    oc = (rmsnorm(oc, gla_norm_g[l]) * jax.nn.silu(cg.astype(f32)).reshape(B, S, H_C, DV_C)).astype(x.dtype)

    mix = jnp.concatenate([oa.reshape(B, S, D_A), ob.reshape(B, S, D_B), oc.reshape(B, S, D_C)], axis=-1)
    x = x + mix @ w_out[l]

    h2 = rmsnorm(x, norm_ffn_g[l])
    u = h2 @ ffn_w_up[l]
    if conv_buf is None:
        conv_buf = jnp.zeros((B, CONV_W - 1, 2 * D_FF), u.dtype)
    ue = jnp.concatenate([conv_buf.astype(u.dtype), u], axis=1)
    cw = ffn_conv_w[l]
    c = ffn_conv_b[l] + cw[0] * ue[:, 0:S]
    for j in range(1, CONV_W):
        c = c + cw[j] * ue[:, j:j + S]
    gate, val = jnp.split(c, 2, axis=-1)
    x = x + (jax.nn.silu(gate) * val) @ ffn_w_down[l]
    dt = x.dtype
    return (x, ka.reshape(B, S, H_A, 2 * DK_A), va, mC.astype(dt), mn.astype(dt), mm.astype(dt),
            gS.astype(dt), ue[:, S:])


def setup_inputs(seed: int = 0) -> dict:
    key = jax.random.key(seed)
    ks = jax.random.split(key, 32)
    f32 = jnp.float32

    def nrm(i, shape, scale):
        return scale * jax.random.normal(ks[i], shape, f32)

    n_pages = PAST_LEN // PAGE_SIZE
    n_used = DEC_BATCH * n_pages
    n_pool = n_used + max(1, n_used // 4)
    page_table = jax.random.permutation(ks[0], n_pool)[:n_used].reshape(DEC_BATCH, n_pages).astype(jnp.int32)
    f_bias = jnp.linspace(3.0, 6.0, H_B, dtype=f32)
    mlstm_gate_b = jnp.stack([nrm(20, (DEPTH, H_B), 0.1), f_bias[None, :] + nrm(21, (DEPTH, H_B), 0.1)], axis=1)
    return {
        'x_prompt': nrm(1, (BATCH, SEQ, D_MODEL), 1.0),
        'x_sample': nrm(2, (DEC_BATCH, DEC_SEQ, D_MODEL), 1.0),
        'cache_k': nrm(3, (DEPTH, n_pool, PAGE_SIZE, H_A, 2 * DK_A), 1.0),
        'cache_v': nrm(4, (DEPTH, n_pool, PAGE_SIZE, H_A, DV_A), 1.0),
        'page_table': page_table,
        'state_mlstm_C': nrm(5, (DEPTH, DEC_BATCH, H_B, DK_B, DV_B), 0.3),
        'state_mlstm_n': jnp.abs(nrm(6, (DEPTH, DEC_BATCH, H_B, DK_B), 0.3)),
        'state_mlstm_m': nrm(7, (DEPTH, DEC_BATCH, H_B), 1.0),
        'state_gla_S': nrm(8, (DEPTH, DEC_BATCH, H_C, DK_C, DV_C), 0.3),
        'state_ffn_conv': nrm(9, (DEPTH, DEC_BATCH, CONV_W - 1, 2 * D_FF), 1.0),
        'norm_mix_g': 1.0 + nrm(10, (DEPTH, D_MODEL), 0.02),
        'w_in': nrm(11, (DEPTH, D_MODEL, D_IN), D_MODEL ** -0.5),
        'q_norm_g': 1.0 + nrm(12, (DEPTH, DK_A), 0.02),
        'k_norm_g': 1.0 + nrm(13, (DEPTH, DK_A), 0.02),
        'diff_lambda': nrm(14, (DEPTH, 4, DK_A), 0.1),
        'diff_subln_g': 1.0 + nrm(15, (DEPTH, DV_A), 0.02),
        'rel_bias': nrm(16, (N_BUCKETS, H_A), 0.5),
        'mlstm_gate_b': mlstm_gate_b,
        'mlstm_norm_g': 1.0 + nrm(17, (DEPTH, DV_B), 0.02),
        'gla_w_alpha': nrm(18, (DEPTH, GLA_RANK, H_C * DK_C), GLA_RANK ** -0.5),
        'gla_b_alpha': nrm(19, (DEPTH, H_C * DK_C), 0.1),
        'gla_norm_g': 1.0 + nrm(22, (DEPTH, DV_C), 0.02),
        'w_out': nrm(23, (DEPTH, D_MIX, D_MODEL), D_MIX ** -0.5),
        'norm_ffn_g': 1.0 + nrm(24, (DEPTH, D_MODEL), 0.02),
        'ffn_w_up': nrm(25, (DEPTH, D_MODEL, 2 * D_FF), D_MODEL ** -0.5),
        'ffn_conv_w': nrm(26, (DEPTH, CONV_W, 2 * D_FF), CONV_W ** -0.5),
        'ffn_conv_b': nrm(27, (DEPTH, 2 * D_FF), 0.02),
        'ffn_w_down': nrm(28, (DEPTH, D_FF, D_MODEL), D_FF ** -0.5),
    }


def reference(x_prompt, x_sample, cache_k, cache_v, page_table, state_mlstm_C, state_mlstm_n,
              state_mlstm_m, state_gla_S, state_ffn_conv, norm_mix_g, w_in, q_norm_g, k_norm_g,
              diff_lambda, diff_subln_g, rel_bias, mlstm_gate_b, mlstm_norm_g, gla_w_alpha,
              gla_b_alpha, gla_norm_g, w_out, norm_ffn_g, ffn_w_up, ffn_conv_w, ffn_conv_b, ffn_w_down):
    weights = (norm_mix_g, w_in, q_norm_g, k_norm_g, diff_lambda, diff_subln_g, rel_bias,
               mlstm_gate_b, mlstm_norm_g, gla_w_alpha, gla_b_alpha, gla_norm_g, w_out,
               norm_ffn_g, ffn_w_up, ffn_conv_w, ffn_conv_b, ffn_w_down)
    f32 = jnp.float32
    Bp = x_prompt.shape[0]
    Bs = x_sample.shape[0]
    past_len = page_table.shape[1] * cache_k.shape[2]
    xp, xs = x_prompt, x_sample
    rows_p, rows_s = [], []
    for l in range(DEPTH):
        outp = layer(xp, l, 0, None, None,
                     jnp.zeros((Bp, H_B, DK_B, DV_B), f32), jnp.zeros((Bp, H_B, DK_B), f32),
                     jnp.zeros((Bp, H_B), f32), jnp.zeros((Bp, H_C, DK_C, DV_C), f32), None, weights)
        xp = outp[0]
        rows_p.append(outp[1:])
        k_past = cache_k[l, page_table].reshape(Bs, past_len, H_A, 2 * DK_A)
        v_past = cache_v[l, page_table].reshape(Bs, past_len, H_A, DV_A)
        outs = layer(xs, l, past_len, k_past, v_past, state_mlstm_C[l], state_mlstm_n[l],
                     state_mlstm_m[l], state_gla_S[l], state_ffn_conv[l], weights)
        xs = outs[0]
        rows_s.append(outs[1:])

    def field(rows, i):
        return jnp.stack([r[i] for r in rows], axis=0)

    return (xp, xs,
            field(rows_p, 0), field(rows_p, 1), field(rows_p, 2), field(rows_p, 3),
            field(rows_p, 4), field(rows_p, 5), field(rows_p, 6),
            field(rows_s, 0), field(rows_s, 1), field(rows_s, 2), field(rows_s, 3),
            field(rows_s, 4), field(rows_s, 5), field(rows_s, 6))
```

```python
import functools
import math

import numpy as np
import jax
import jax.numpy as jnp
from jax import lax
from jax.experimental import pallas as pl
from jax.experimental.pallas import tpu as pltpu

F32 = jnp.float32
BF16 = jnp.bfloat16

LANES = 128
VMEM_LIMIT = 52 * 1024 * 1024
EPS = 1e-6
NEG = -1e30
GLA_TAU = 16.0
MAX_DISTANCE = 128
ATTN_TILE = 256
SCAN_CHUNK = 256
GLA_CHUNK = 128
SAMPLE_CHUNK = 128


def _cparams(*sem):
    return pltpu.CompilerParams(dimension_semantics=sem, vmem_limit_bytes=VMEM_LIMIT)


def _split_bf16(x):
    hi = x.astype(BF16)
    lo = (x - hi.astype(F32)).astype(BF16)
    return hi, lo


def _dot(a, b):
    return jnp.dot(a, b, preferred_element_type=F32)


def _dot_nt(a, b):
    return lax.dot_general(a, b, (((1,), (1,)), ((), ())), preferred_element_type=F32)


def _dot_tn(a, b):
    return lax.dot_general(a, b, (((0,), (0,)), ((), ())), preferred_element_type=F32)


def _log_sigmoid(x):
    return jnp.minimum(x, 0.0) - jnp.log(1.0 + jnp.exp(-jnp.abs(x)))


def _sigmoid(x):
    return 1.0 / (1.0 + jnp.exp(-x))


def _div_pow2(x, n):
    assert n & (n - 1) == 0
    return lax.shift_right_logical(x, n.bit_length() - 1)


def _mod_pow2(x, n):
    assert n & (n - 1) == 0
    return x & (n - 1)


def _pad_rows(x, rows):
    if x.shape[0] == rows:
        return x
    return jnp.concatenate([x, jnp.zeros((rows - x.shape[0], x.shape[1]), x.dtype)], axis=0)


def _rmsnorm_kernel(x_ref, g_ref, o_ref):
    x = x_ref[...]
    ms = jnp.mean(x * x, axis=-1, keepdims=True)
    o_ref[...] = (x * lax.rsqrt(ms + EPS) * g_ref[...]).astype(o_ref.dtype)


def _rmsnorm(x, g):
    m, d = x.shape
    tm = min(m, 512)
    return pl.pallas_call(
        _rmsnorm_kernel,
        out_shape=jax.ShapeDtypeStruct((m, d), BF16),
        grid=(m // tm,),
        in_specs=[pl.BlockSpec((tm, d), lambda i: (i, 0)),
                  pl.BlockSpec((1, d), lambda i: (0, 0))],
        out_specs=pl.BlockSpec((tm, d), lambda i: (i, 0)),
        compiler_params=_cparams("parallel"),
        name="rmsnorm",
    )(x, g.reshape(1, d))


def _mm_kernel(a_ref, w_ref, o_ref):
    o_ref[...] = _dot(a_ref[...], w_ref[...]).astype(o_ref.dtype)


def _mm_res_kernel(a_ref, w_ref, r_ref, o_ref):
    o_ref[...] = (r_ref[...] + _dot(a_ref[...], w_ref[...])).astype(o_ref.dtype)


def _matmul(a, w, res=None, name="matmul"):
    m, k = a.shape
    n = w.shape[1]
    tm = min(m, 1024 if k <= 2048 else 512)
    tn = 512 if n % 512 == 0 else (256 if n % 256 == 0 else LANES)
    in_specs = [pl.BlockSpec((tm, k), lambda i, j: (i, 0)),
                pl.BlockSpec((k, tn), lambda i, j: (0, j))]
    args = [a, w]
    kern = _mm_kernel
    if res is not None:
        in_specs.append(pl.BlockSpec((tm, tn), lambda i, j: (i, j)))
        args.append(res)
        kern = _mm_res_kernel
    return pl.pallas_call(
        kern,
        out_shape=jax.ShapeDtypeStruct((m, n), F32),
        grid=(m // tm, n // tn),
        in_specs=in_specs,
        out_specs=pl.BlockSpec((tm, tn), lambda i, j: (i, j)),
        compiler_params=_cparams("parallel", "parallel"),
        name=name,
    )(*args)


def _qknorm_kernel(aq_ref, ak_ref, av_ref, qg_ref, kg_ref, bd_ref,
                   qn_ref, kn_ref, knb_ref, vb_ref, *, heads, inv_dk, scale):
    bd = bd_ref[...]
    qg = qg_ref[...]
    kg = kg_ref[...]

    def norm(x, g):
        hi, lo = _split_bf16(x * x)
        ss = _dot(hi, bd) + _dot(lo, bd)
        return x * lax.rsqrt(ss * inv_dk + EPS) * g

    for h in range(heads):
        sl = slice(LANES * h, LANES * (h + 1))
        qn_ref[:, sl] = (norm(aq_ref[:, sl], qg) * scale).astype(BF16)
        kn = norm(ak_ref[:, sl], kg)
        kn_ref[:, sl] = kn
        knb_ref[:, sl] = kn.astype(BF16)
    vb_ref[...] = av_ref[...].astype(BF16)


def _qknorm(z, qg, kg, heads, dk):
    m = z.shape[0]
    w = heads * LANES
    tm = min(m, 256)
    lane = np.arange(LANES)
    bd = jnp.asarray((lane[:, None] // dk) == (lane[None, :] // dk), BF16)
    reps = LANES // dk
    kern = functools.partial(_qknorm_kernel, heads=heads, inv_dk=1.0 / dk, scale=dk ** -0.5)
    return pl.pallas_call(
        kern,
        out_shape=(jax.ShapeDtypeStruct((m, w), BF16), jax.ShapeDtypeStruct((m, w), F32),
                   jax.ShapeDtypeStruct((m, w), BF16), jax.ShapeDtypeStruct((m, w), BF16)),
        grid=(m // tm,),
        in_specs=[pl.BlockSpec((tm, w), lambda i: (i, 0)),
                  pl.BlockSpec((tm, w), lambda i: (i, 1)),
                  pl.BlockSpec((tm, w), lambda i: (i, 2)),
                  pl.BlockSpec((1, LANES), lambda i: (0, 0)),
                  pl.BlockSpec((1, LANES), lambda i: (0, 0)),
                  pl.BlockSpec((LANES, LANES), lambda i: (0, 0))],
        out_specs=(pl.BlockSpec((tm, w), lambda i: (i, 0)),) * 4,
        compiler_params=_cparams("parallel"),
        name="qknorm",
    )(z, z, z, jnp.tile(qg, reps).reshape(1, LANES), jnp.tile(kg, reps).reshape(1, LANES), bd)


def _bucket(n, n_buckets):
    max_exact = n_buckets // 2
    nf = jnp.maximum(n, max_exact).astype(F32)
    large = max_exact + (jnp.log(nf / max_exact) / math.log(MAX_DISTANCE / max_exact)
                         * (n_buckets - max_exact)).astype(jnp.int32)
    large = jnp.minimum(large, n_buckets - 1)
    return jnp.where(n < max_exact, jnp.maximum(n, 0), large)


def _prompt_bias_kernel(rb_ref, o_ref, *, tile, n_buckets):
    h = pl.program_id(0)
    i = lax.broadcasted_iota(jnp.int32, (tile, tile), 0)
    j = lax.broadcasted_iota(jnp.int32, (tile, tile), 1)
    far = jnp.full((tile, tile), MAX_DISTANCE, jnp.int32)
    for t, n in enumerate((i - j, tile + i - j, far)):
        b = _bucket(n, n_buckets)
        val = jnp.full((tile, tile), rb_ref[0, h], F32)
        for k in range(1, n_buckets):
            val = jnp.where(b == k, rb_ref[k, h], val)
        o_ref[0, t] = jnp.where(n >= 0, val, NEG)


def _prompt_bias(rel_bias, tile):
    n_buckets, heads = rel_bias.shape
    kern = functools.partial(_prompt_bias_kernel, tile=tile, n_buckets=n_buckets)
    return pl.pallas_call(
        kern,
        out_shape=jax.ShapeDtypeStruct((heads, 3, tile, tile), F32),
        grid=(heads,),
        in_specs=[pl.BlockSpec(memory_space=pltpu.SMEM)],
        out_specs=pl.BlockSpec((1, 3, tile, tile), lambda h: (h, 0, 0, 0)),
        compiler_params=_cparams("parallel"),
        name="prompt_bias",
    )(rel_bias)


def _decode_bias_kernel(rb_ref, pg_ref, new_ref, *, page, dec_seq, heads, n_buckets):
    grp = 2 * dec_seq

    def lookup(n, h, keep):
        b = _bucket(n, n_buckets)
        val = jnp.full(n.shape, rb_ref[0, h], F32)
        for k in range(1, n_buckets):
            val = jnp.where(b == k, rb_ref[k, h], val)
        return jnp.where(keep & (n >= 0), val, NEG)

    qi = _mod_pow2(lax.broadcasted_iota(jnp.int32, (grp, page * heads), 0), dec_seq)
    c = lax.broadcasted_iota(jnp.int32, (grp, page * heads), 1)
    tok = _div_pow2(c, heads)
    far = jnp.full((grp, page * heads), MAX_DISTANCE, jnp.int32)
    qn = _mod_pow2(lax.broadcasted_iota(jnp.int32, (grp, page), 0), dec_seq)
    jn = lax.broadcasted_iota(jnp.int32, (grp, page), 1)
    for h in range(heads):
        keep = _mod_pow2(c, heads) == h
        rs = slice(grp * h, grp * (h + 1))
        pg_ref[0, rs, :] = lookup(far, h, keep)
        pg_ref[1, rs, :] = lookup(page + qi - tok, h, keep)
        new_ref[rs, :] = lookup(qn - jn, h, jn < dec_seq)


def _decode_bias(rel_bias, page, dec_seq):
    n_buckets, heads = rel_bias.shape
    rows = 2 * dec_seq * heads
    kern = functools.partial(_decode_bias_kernel, page=page, dec_seq=dec_seq, heads=heads,
                             n_buckets=n_buckets)
    return pl.pallas_call(
        kern,
        out_shape=(jax.ShapeDtypeStruct((2, rows, page * heads), F32),
                   jax.ShapeDtypeStruct((rows, page), F32)),
        in_specs=[pl.BlockSpec(memory_space=pltpu.SMEM)],
        name="decode_bias",
    )(rel_bias)


def _diff_lambda(dl, lam_init):
    s1 = jnp.sum(dl[0:1] * dl[1:2], axis=-1, keepdims=True)
    s2 = jnp.sum(dl[2:3] * dl[3:4], axis=-1, keepdims=True)
    return jnp.exp(s1) - jnp.exp(s2) + lam_init


def _subln(o, g, lam_init):
    ms = jnp.mean(o * o, axis=-1, keepdims=True)
    return o * lax.rsqrt(ms + EPS) * g * (1.0 - lam_init)


def _flash_update(s, m_prev, l_prev):
    m_new = jnp.maximum(m_prev, jnp.max(s, axis=-1, keepdims=True))
    alpha = jnp.exp(m_prev - m_new)
    p = jnp.exp(s - m_new)
    l_new = alpha * l_prev + jnp.sum(p, axis=-1, keepdims=True)
    return p, alpha, m_new, l_new


def _attn_kernel(q_ref, k_ref, v_ref, bias_ref, dl_ref, g_ref, o_ref, m_sc, l_sc, acc_sc,
                 *, heads, tile, dk, lam_init):
    qi = pl.program_id(1)
    ki = pl.program_id(2)

    @pl.when(ki == 0)
    def _init():
        m_sc[...] = jnp.full(m_sc.shape, -jnp.inf, F32)
        l_sc[...] = jnp.zeros(l_sc.shape, F32)
        acc_sc[...] = jnp.zeros(acc_sc.shape, F32)

    @pl.when(ki <= qi)
    def _step():
        lane = lax.broadcasted_iota(jnp.int32, (tile, LANES), 1)
        for h in range(heads):
            sl = slice(LANES * h, LANES * (h + 1))
            qh = q_ref[0, :, sl]
            zero = jnp.zeros_like(qh)
            qs = jnp.concatenate([jnp.where(lane < dk, qh, zero), jnp.where(lane >= dk, qh, zero)], axis=0)
            b = bias_ref[h, 0]
            s = _dot_nt(qs, k_ref[0, :, sl]) + jnp.concatenate([b, b], axis=0)
            p, alpha, m_new, l_new = _flash_update(s, m_sc[h], l_sc[h])
            acc_sc[h] = alpha * acc_sc[h] + _dot(p.astype(BF16), v_ref[0, :, sl])
            m_sc[h] = m_new
            l_sc[h] = l_new

    @pl.when(ki == qi)
    def _finish():
        lam = _diff_lambda(dl_ref[...], lam_init)
        g = g_ref[...]
        for h in range(heads):
            acc = acc_sc[h]
            l = l_sc[h]
            o = acc[:tile] / l[:tile] - lam * (acc[tile:] / l[tile:])
            o_ref[0, :, LANES * h:LANES * (h + 1)] = _subln(o, g, lam_init)


def _prompt_attention(qn, knb, vb, bias, dl, g, lam_init, dk):
    bsz, seq, w = qn.shape
    heads = w // LANES
    tile = bias.shape[-1]
    nq = seq // tile
    kern = functools.partial(_attn_kernel, heads=heads, tile=tile, dk=dk, lam_init=lam_init)

    def bias_idx(b, qi, ki):
        return (0, jnp.where(ki >= qi, 0, jnp.where(ki == qi - 1, 1, 2)), 0, 0)

    return pl.pallas_call(
        kern,
        out_shape=jax.ShapeDtypeStruct((bsz, seq, w), F32),
        grid=(bsz, nq, nq),
        in_specs=[pl.BlockSpec((1, tile, w), lambda b, qi, ki: (b, qi, 0)),
                  pl.BlockSpec((1, tile, w), lambda b, qi, ki: (b, jnp.minimum(ki, qi), 0)),
                  pl.BlockSpec((1, tile, w), lambda b, qi, ki: (b, jnp.minimum(ki, qi), 0)),
                  pl.BlockSpec((heads, 1, tile, tile), bias_idx),
                  pl.BlockSpec(dl.shape, lambda b, qi, ki: (0, 0)),
                  pl.BlockSpec((1, LANES), lambda b, qi, ki: (0, 0))],
        out_specs=pl.BlockSpec((1, tile, w), lambda b, qi, ki: (b, qi, 0)),
        scratch_shapes=[pltpu.VMEM((heads, 2 * tile, 1), F32),
                        pltpu.VMEM((heads, 2 * tile, 1), F32),
                        pltpu.VMEM((heads, 2 * tile, LANES), F32)],
        compiler_params=_cparams("parallel", "parallel", "arbitrary"),
        name="prompt_attention",
    )(qn, knb, vb, bias, dl, g.reshape(1, LANES))


def _decode_attn_kernel(pt_ref, q_ref, kn_ref, vn_ref, kc_ref, vc_ref, bias_ref, bnew_ref, dl_ref, g_ref,
                        o_ref, wq_sc, wqb_sc, m_sc, l_sc, acc_sc, *, heads, dec_seq, page, dk, lam_init):
    del pt_ref
    p_idx = pl.program_id(1)
    rows = 2 * dec_seq * heads
    grp = 2 * dec_seq
    width = heads * LANES

    @pl.when(p_idx == 0)
    def _init():
        q = q_ref[0].astype(F32)
        lane = lax.broadcasted_iota(jnp.int32, (dec_seq, LANES), 1)
        pieces = []
        for h in range(heads):
            qh = q[:, LANES * h:LANES * (h + 1)]
            pieces += [jnp.where(lane < dk, qh, 0.0), jnp.where(lane >= dk, qh, 0.0)]
        wq = jnp.concatenate(pieces, axis=0)
        wq_sc[...] = wq.astype(BF16)
        c = lax.broadcasted_iota(jnp.int32, (rows, width), 1)
        r = lax.broadcasted_iota(jnp.int32, (rows, width), 0)
        own = _div_pow2(r, grp) == _div_pow2(c, LANES)
        wqb_sc[...] = jnp.where(own, jnp.concatenate([wq] * heads, axis=1), 0.0).astype(BF16)
        m_sc[...] = jnp.full(m_sc.shape, -jnp.inf, F32)
        l_sc[...] = jnp.zeros(l_sc.shape, F32)
        acc_sc[...] = jnp.zeros(acc_sc.shape, F32)

    s = _dot_nt(wq_sc[...], kc_ref[...].astype(BF16)) + bias_ref[0]
    p, alpha, m_new, l_new = _flash_update(s, m_sc[...], l_sc[...])
    m_sc[...] = m_new
    l_sc[...] = l_new
    acc_sc[...] = alpha * acc_sc[...] + _dot(p.astype(BF16), vc_ref[...].astype(BF16))

    @pl.when(p_idx == pl.num_programs(1) - 1)
    def _finish():
        kn = _pad_rows(kn_ref[0].astype(F32), page).astype(BF16)
        vn = _pad_rows(vn_ref[0].astype(F32), page)
        s = _dot_nt(wqb_sc[...], kn) + bnew_ref[...]
        p, alpha, _, l_fin = _flash_update(s, m_sc[...], l_sc[...])
        pb = p.astype(BF16)
        lam = _diff_lambda(dl_ref[...], lam_init)
        g = g_ref[...]
        for h in range(heads):
            rs = slice(grp * h, grp * (h + 1))
            vh = vn[:, LANES * h:LANES * (h + 1)].astype(BF16)
            acc = alpha[rs] * acc_sc[rs, :] + _dot(pb[rs], vh)
            l = l_fin[rs]
            o = acc[:dec_seq] / l[:dec_seq] - lam * (acc[dec_seq:] / l[dec_seq:])
            o_ref[0, :, LANES * h:LANES * (h + 1)] = _subln(o, g, lam_init)


def _decode_attention(layer, qn, knb, vb, cache_k, cache_v, page_table, bias_pg, bias_new, dl, g, lam_init, dk):
    bsz, dec_seq, w = qn.shape
    heads = w // LANES
    n_pages = page_table.shape[1]
    page = cache_k.shape[2] // heads
    rows = 2 * dec_seq * heads
    kern = functools.partial(_decode_attn_kernel, heads=heads, dec_seq=dec_seq, page=page, dk=dk,
                             lam_init=lam_init)
    new_spec = pl.BlockSpec((1, dec_seq, w), lambda b, p, pt: (b, 0, 0))
    cache_spec = pl.BlockSpec((None, None, page * heads, LANES), lambda b, p, pt: (layer, pt[b, p], 0, 0))
    grid_spec = pltpu.PrefetchScalarGridSpec(
        num_scalar_prefetch=1,
        grid=(bsz, n_pages),
        in_specs=[new_spec, new_spec, new_spec, cache_spec, cache_spec,
                  pl.BlockSpec((1, rows, page * heads),
                               lambda b, p, pt: (jnp.where(p == n_pages - 1, 1, 0), 0, 0)),
                  pl.BlockSpec((rows, page), lambda b, p, pt: (0, 0)),
                  pl.BlockSpec(dl.shape, lambda b, p, pt: (0, 0)),
                  pl.BlockSpec((1, LANES), lambda b, p, pt: (0, 0))],
        out_specs=pl.BlockSpec((1, dec_seq, w), lambda b, p, pt: (b, 0, 0)),
        scratch_shapes=[pltpu.VMEM((rows, LANES), BF16),
                        pltpu.VMEM((rows, w), BF16),
                        pltpu.VMEM((rows, 1), F32),
                        pltpu.VMEM((rows, 1), F32),
                        pltpu.VMEM((rows, LANES), F32)],
    )
    return pl.pallas_call(
        kern,
        out_shape=jax.ShapeDtypeStruct((bsz, dec_seq, w), F32),
        grid_spec=grid_spec,
        compiler_params=_cparams("parallel", "arbitrary"),
        name="decode_attention",
    )(page_table, qn, knb, vb, cache_k, cache_v, bias_pg, bias_new, dl, g.reshape(1, LANES))


GATE_I = 16
GATE_F = 20


def _mlstm_kernel(q_ref, k_ref, v_ref, og_ref, zs_ref, gt_ref, gbl_ref, gbc_ref, g_ref,
                  c0_ref, n0_ref, m0_ref,
                  o_ref, cf_ref, nf_ref, mf_ref, c_sc, n_sc, m_sc,
                  *, heads, chunk, valid, scale):
    c_idx = pl.program_id(1)

    @pl.when(c_idx == 0)
    def _init():
        c_sc[...] = c0_ref[0]
        n_sc[...] = n0_ref[0]
        m_sc[...] = m0_ref[0]

    rows_in = q_ref.shape[0]
    row = lax.broadcasted_iota(jnp.int32, (chunk, 1), 0)
    t_i = lax.broadcasted_iota(jnp.int32, (chunk, chunk), 0)
    s_i = lax.broadcasted_iota(jnp.int32, (chunk, chunk), 1)
    causal = t_i >= s_i
    tril = causal.astype(BF16)
    triu = (t_i <= s_i).astype(BF16)

    gcol = _pad_rows(zs_ref[...], chunk) + gbl_ref[...]
    lf_mat = _log_sigmoid(gcol)
    if valid < chunk:
        lf_mat = jnp.where(row < valid, lf_mat, 0.0)
    hi, lo = _split_bf16(lf_mat)
    b_mat = _dot(tril, hi) + _dot(tril, lo)

    grow = gt_ref[0] + gbc_ref[...]
    col = lax.broadcasted_iota(jnp.int32, grow.shape, 1)
    grow_id = lax.broadcasted_iota(jnp.int32, grow.shape, 0)
    lf_rows = jnp.where(grow_id >= heads, _log_sigmoid(grow), 0.0)
    li_rows = grow
    if valid < chunk:
        lf_rows = jnp.where(col < valid, lf_rows, 0.0)
        li_rows = jnp.where(col < valid, li_rows, NEG)
    hi, lo = _split_bf16(lf_rows)
    b_rows = _dot(hi, triu) + _dot(lo, triu)

    g = g_ref[...]
    for h in range(heads):
        sl = slice(LANES * h, LANES * (h + 1))
        b_col = b_mat[:, GATE_F + h:GATE_F + h + 1]
        li_col = gcol[:, GATE_I + h:GATE_I + h + 1]
        if valid < chunk:
            li_col = jnp.where(row < valid, li_col, NEG)
        b_row = b_rows[heads + h:heads + h + 1, :]
        li_row = li_rows[h:h + 1, :]
        d = jnp.where(causal, b_col - b_row + li_row, NEG)
        m_prev = m_sc[h:h + 1, 0:1]
        inter = b_col + m_prev
        mt = jnp.maximum(inter, jnp.max(d, axis=-1, keepdims=True))
        wi = jnp.exp(inter - mt)
        q = _pad_rows(q_ref[:, sl], chunk)
        ks = _pad_rows(k_ref[:, sl], chunk) * scale
        vb = _pad_rows(v_ref[:, sl], chunk).astype(BF16)
        qb = q.astype(BF16)
        p = _dot_nt(qb, ks.astype(BF16)) * jnp.exp(d - mt)
        c_prev = c_sc[h]
        n_prev = n_sc[h:h + 1, :]
        num = wi * _dot(qb, c_prev.astype(BF16)) + _dot(p.astype(BF16), vb)
        den = wi * jnp.sum(q * n_prev, axis=-1, keepdims=True) + jnp.sum(p, axis=-1, keepdims=True)
        hh = num / jnp.maximum(jnp.abs(den), jnp.exp(-mt))
        og = _sigmoid(og_ref[:, sl])
        y = og * hh[:rows_in]
        ms = jnp.mean(y * y, axis=-1, keepdims=True)
        o_ref[:, sl] = y * lax.rsqrt(ms + EPS) * g

        m_new = mt[chunk - 1:chunk, :]
        b_last = b_col[chunk - 1:chunk, :]
        a = jnp.exp(b_last + m_prev - m_new)
        ws = jnp.exp(b_last - b_col + li_col - m_new)
        kw = ks * ws
        c_sc[h] = a * c_prev + _dot_tn(kw.astype(BF16), vb)
        n_sc[h:h + 1, :] = a * n_prev + jnp.sum(kw, axis=0, keepdims=True)
        m_sc[h:h + 1, :] = jnp.broadcast_to(m_new, (1, LANES))

    @pl.when(c_idx == pl.num_programs(1) - 1)
    def _finish():
        cf_ref[0] = c_sc[...]
        nf_ref[0] = n_sc[...]
        mf_ref[0] = m_sc[...]


def _mlstm(z, zs, gt, gate_b, g, c0, n0, m0, seq, chunk, col0):
    bsz, heads = c0.shape[0], c0.shape[1]
    w = heads * LANES
    rows_in = min(seq, chunk)
    nc = seq // rows_in
    valid = rows_in
    gt_w = gt.shape[-1] // nc
    kern = functools.partial(_mlstm_kernel, heads=heads, chunk=chunk, valid=valid, scale=LANES ** -0.5)
    gbl = jnp.zeros((1, LANES), F32)
    gbl = gbl.at[0, GATE_I:GATE_I + heads].set(gate_b[0]).at[0, GATE_F:GATE_F + heads].set(gate_b[1])
    gbc = gate_b.reshape(2 * heads, 1)
    m0b = jnp.broadcast_to(m0[:, :, None], (bsz, heads, LANES))

    def zspec(blk):
        return pl.BlockSpec((rows_in, w), lambda b, c: (b * nc + c, col0 + blk))

    state = lambda shape: pl.BlockSpec((1,) + shape, lambda b, c: (b,) + (0,) * len(shape))
    return pl.pallas_call(
        kern,
        out_shape=(jax.ShapeDtypeStruct((bsz * seq, w), F32),
                   jax.ShapeDtypeStruct((bsz, heads, LANES, LANES), F32),
                   jax.ShapeDtypeStruct((bsz, heads, LANES), F32),
                   jax.ShapeDtypeStruct((bsz, heads, LANES), F32)),
        grid=(bsz, nc),
        in_specs=[zspec(0), zspec(1), zspec(2), zspec(3),
                  pl.BlockSpec((rows_in, LANES), lambda b, c: (b * nc + c, 0)),
                  pl.BlockSpec((1, 2 * heads, gt_w), lambda b, c: (b, 0, c)),
                  pl.BlockSpec((1, LANES), lambda b, c: (0, 0)),
                  pl.BlockSpec((2 * heads, 1), lambda b, c: (0, 0)),
                  pl.BlockSpec((1, LANES), lambda b, c: (0, 0)),
                  state((heads, LANES, LANES)), state((heads, LANES)), state((heads, LANES))],
        out_specs=(pl.BlockSpec((rows_in, w), lambda b, c: (b * nc + c, 0)),
                   state((heads, LANES, LANES)), state((heads, LANES)), state((heads, LANES))),
        scratch_shapes=[pltpu.VMEM((heads, LANES, LANES), F32),
                        pltpu.VMEM((heads, LANES), F32),
                        pltpu.VMEM((heads, LANES), F32)],
        compiler_params=_cparams("parallel", "arbitrary"),
        name="mlstm",
    )(z, z, z, z, zs, gt, gbl, gbc, g.reshape(1, LANES), c0, n0, m0b)


def _gla_levels(chunk):
    n, out = chunk, []
    while n >= 2:
        out.append(n)
        n //= 2
    return out


def _gla_weights(chunk):
    t = np.arange(chunk)[:, None]
    s = np.arange(chunk)[None, :]
    blocks = [(s <= t).astype(np.float32), (s > t).astype(np.float32)]
    for n in _gla_levels(chunk):
        mid = (t // n) * n + n // 2 - 1
        blocks.append(((s > mid) & (s <= t)).astype(np.float32) - ((s > t) & (s <= mid)).astype(np.float32))
    return jnp.asarray(np.concatenate(blocks, axis=0), BF16)


def _gla_kernel(q_ref, k_ref, v_ref, gg_ref, zs_ref, wa_ref, ba_ref, ws_ref, g_ref, s0_ref,
                o_ref, sf_ref, s_sc, *, heads, chunk, valid, dk, scale):
    c_idx = pl.program_id(1)

    @pl.when(c_idx == 0)
    def _init():
        s_sc[...] = s0_ref[0]

    rows_in = q_ref.shape[0]
    row = lax.broadcasted_iota(jnp.int32, (chunk, 1), 0)
    t_i = lax.broadcasted_iota(jnp.int32, (chunk, chunk), 0)
    s_i = lax.broadcasted_iota(jnp.int32, (chunk, chunk), 1)

    zs = _pad_rows(zs_ref[...], chunk)
    la = _log_sigmoid(_dot(zs.astype(BF16), wa_ref[...]) + ba_ref[...]) * (1.0 / GLA_TAU)
    if valid < chunk:
        la = jnp.where(row < valid, la, 0.0)
    hi, lo = _split_bf16(la)
    wst = ws_ref[...]
    e_all = _dot(wst, hi) + _dot(wst, lo)
    ones = jnp.ones((chunk, LANES), BF16)
    levels = _gla_levels(chunk)
    g = g_ref[...]

    for h in range(heads):
        ksl = slice(dk * h, dk * (h + 1))
        vsl = slice(LANES * h, LANES * (h + 1))
        q = _pad_rows(q_ref[:, ksl], chunk) * scale
        k = _pad_rows(k_ref[:, ksl], chunk)
        vb = _pad_rows(v_ref[:, vsl], chunk).astype(BF16)
        bc = e_all[0:chunk, ksl]
        rem = e_all[chunk:2 * chunk, ksl]
        a_mat = jnp.zeros((chunk, chunk), F32)
        for li, n in enumerate(levels):
            e = e_all[(2 + li) * chunk:(3 + li) * chunk, ksl]
            second = (row & (n - 1)) >= n // 2
            qt = jnp.where(second, q * jnp.exp(jnp.minimum(e, 0.0)), 0.0)
            kt = jnp.where(second, 0.0, k * jnp.exp(jnp.minimum(-e, 0.0)))
            shift = n.bit_length() - 1
            same = lax.shift_right_logical(t_i, shift) == lax.shift_right_logical(s_i, shift)
            a_mat = a_mat + jnp.where(same, _dot_nt(qt.astype(BF16), kt.astype(BF16)), 0.0)
        s_prev = s_sc[h]
        qd = (q * jnp.exp(bc)).astype(BF16)
        o = _dot(qd, s_prev.astype(BF16)) + _dot(a_mat.astype(BF16), vb)
        o = o + jnp.sum(q * k, axis=-1, keepdims=True) * vb.astype(F32)
        o = o[:rows_in]
        ms = jnp.mean(o * o, axis=-1, keepdims=True)
        gate = gg_ref[:, vsl]
        o_ref[:, vsl] = o * lax.rsqrt(ms + EPS) * g * (gate * _sigmoid(gate))

        la_h = la[:, ksl]
        hi_h, lo_h = _split_bf16(la_h)
        tot = _dot_tn(hi_h, ones) + _dot_tn(lo_h, ones)
        kd = (k * jnp.exp(rem)).astype(BF16)
        s_sc[h] = jnp.exp(tot) * s_prev + _dot_tn(kd, vb)

    @pl.when(c_idx == pl.num_programs(1) - 1)
    def _finish():
        sf_ref[0] = s_sc[...]


def _gla(z, zs, w_alpha, b_alpha, g, s0, seq, chunk, qcol, kcol, vcol, gcol):
    bsz, heads, dk, dv = s0.shape
    rows_in = min(seq, chunk)
    nc = seq // rows_in
    kw = heads * dk
    vw = heads * dv
    wa = jnp.zeros((LANES, kw), F32).at[:w_alpha.shape[0]].set(w_alpha).astype(BF16)
    wst = _gla_weights(chunk)
    kern = functools.partial(_gla_kernel, heads=heads, chunk=chunk, valid=rows_in, dk=dk, scale=dk ** -0.5)
    const = lambda shape: pl.BlockSpec(shape, lambda b, c: (0,) * len(shape))
    return pl.pallas_call(
        kern,
        out_shape=(jax.ShapeDtypeStruct((bsz * seq, vw), F32),
                   jax.ShapeDtypeStruct((bsz, heads, dk, dv), F32)),
        grid=(bsz, nc),
        in_specs=[pl.BlockSpec((rows_in, kw), lambda b, c: (b * nc + c, qcol)),
                  pl.BlockSpec((rows_in, kw), lambda b, c: (b * nc + c, kcol)),
                  pl.BlockSpec((rows_in, vw), lambda b, c: (b * nc + c, vcol)),
                  pl.BlockSpec((rows_in, vw), lambda b, c: (b * nc + c, gcol)),
                  pl.BlockSpec((rows_in, LANES), lambda b, c: (b * nc + c, 0)),
                  const((LANES, kw)), const((1, kw)), const(wst.shape), const((1, LANES)),
                  pl.BlockSpec((1, heads, dk, dv), lambda b, c: (b, 0, 0, 0))],
        out_specs=(pl.BlockSpec((rows_in, vw), lambda b, c: (b * nc + c, 0)),
                   pl.BlockSpec((1, heads, dk, dv), lambda b, c: (b, 0, 0, 0))),
        scratch_shapes=[pltpu.VMEM((heads, dk, dv), F32)],
        compiler_params=_cparams("parallel", "arbitrary"),
        name="gla",
    )(z, z, z, z, zs, wa, b_alpha.reshape(1, kw), wst, g.reshape(1, LANES), s0)


def _conv_gate(ug, uv, cwg, cwv, cbg, cbv, prev):
    def conv(u, cw, cb, which):
        u1, u2 = prev(u, which)
        return cb + cw[0:1] * u2 + cw[1:2] * u1 + cw[2:3] * u
    cg = conv(ug, cwg, cbg, 0)
    cv = conv(uv, cwv, cbv, 1)
    return cg * _sigmoid(cg) * cv


def _ffn_up_prompt_kernel(h_ref, wg_ref, wv_ref, cwg_ref, cwv_ref, cbg_ref, cbv_ref,
                          act_ref, tg_ref, tv_ref, carry_sc, *, tiles_per_seq):
    i = pl.program_id(1)
    first = (i % tiles_per_seq) == 0
    hb = h_ref[...]
    ug = _dot(hb, wg_ref[...])
    uv = _dot(hb, wv_ref[...])
    tm = ug.shape[0]
    row = lax.broadcasted_iota(jnp.int32, (tm, 1), 0)

    @pl.when(first)
    def _reset():
        carry_sc[...] = jnp.zeros(carry_sc.shape, F32)

    def prev(u, which):
        c = carry_sc[which]
        c1 = c[7:8]
        c2 = c[6:7]
        u1 = jnp.where(row == 0, c1, pltpu.roll(u, 1, 0))
        u2 = jnp.where(row == 0, c2, jnp.where(row == 1, c1, pltpu.roll(u, 2, 0)))
        carry_sc[which] = u[tm - 8:tm]
        return u1, u2

    act_ref[...] = _conv_gate(ug, uv, cwg_ref[...], cwv_ref[...], cbg_ref[...], cbv_ref[...], prev).astype(BF16)
    tg_ref[0] = ug[tm - 8:tm]
    tv_ref[0] = uv[tm - 8:tm]


def _ffn_up_prompt(h2, w_up, conv_w, conv_b, bsz, seq):
    m, d = h2.shape
    dff = w_up.shape[1] // 2
    tm, tn = 512, 512
    nj = dff // tn
    tps = seq // tm
    kern = functools.partial(_ffn_up_prompt_kernel, tiles_per_seq=tps)
    wspec = lambda off: pl.BlockSpec((d, tn), lambda j, i: (0, j + off))
    cspec = lambda r, off: pl.BlockSpec((r, tn), lambda j, i: (0, j + off))
    tail = pl.BlockSpec((1, 8, tn), lambda j, i: (i // tps, 0, j))
    cb = conv_b.reshape(1, -1)
    return pl.pallas_call(
        kern,
        out_shape=(jax.ShapeDtypeStruct((m, dff), BF16),
                   jax.ShapeDtypeStruct((bsz, 8, dff), F32),
                   jax.ShapeDtypeStruct((bsz, 8, dff), F32)),
        grid=(nj, m // tm),
        in_specs=[pl.BlockSpec((tm, d), lambda j, i: (i, 0)),
                  wspec(0), wspec(nj), cspec(3, 0), cspec(3, nj), cspec(1, 0), cspec(1, nj)],
        out_specs=(pl.BlockSpec((tm, tn), lambda j, i: (i, j)), tail, tail),
        scratch_shapes=[pltpu.VMEM((2, 8, tn), F32)],
        compiler_params=_cparams("parallel", "arbitrary"),
        name="ffn_up_prompt",
    )(h2, w_up, w_up, conv_w, conv_w, cb, cb)


def _ffn_up_sample_kernel(h_ref, wg_ref, wv_ref, cwg_ref, cwv_ref, cbg_ref, cbv_ref,
                          p1g_ref, p1v_ref, p2g_ref, p2v_ref, act_ref, ug_ref, uv_ref, *, seq):
    hb = h_ref[...]
    ug = _dot(hb, wg_ref[...])
    uv = _dot(hb, wv_ref[...])
    tm = ug.shape[0]
    pos = _mod_pow2(lax.broadcasted_iota(jnp.int32, (tm, 1), 0), seq)
    p1 = (p1g_ref, p1v_ref)
    p2 = (p2g_ref, p2v_ref)

    def prev(u, which):
        u1 = jnp.where(pos >= 1, pltpu.roll(u, 1, 0), p1[which][...])
        u2 = jnp.where(pos >= 2, pltpu.roll(u, 2, 0), p2[which][...])
        return u1, u2

    act_ref[...] = _conv_gate(ug, uv, cwg_ref[...], cwv_ref[...], cbg_ref[...], cbv_ref[...], prev).astype(BF16)
    ug_ref[...] = ug
    uv_ref[...] = uv


def _ffn_up_sample(h2, w_up, conv_w, conv_b, conv_state, seq):
    m, d = h2.shape
    dff = w_up.shape[1] // 2
    tn = 512
    nj = dff // tn
    bsz = m // seq
    p1 = jnp.zeros((bsz, seq, 2 * dff), F32).at[:, 0].set(conv_state[:, 1]).reshape(m, 2 * dff)
    p2 = jnp.zeros((bsz, seq, 2 * dff), F32).at[:, 0].set(conv_state[:, 0]).at[:, 1].set(conv_state[:, 1])
    p2 = p2.reshape(m, 2 * dff)
    kern = functools.partial(_ffn_up_sample_kernel, seq=seq)
    wspec = lambda off: pl.BlockSpec((d, tn), lambda j: (0, j + off))
    cspec = lambda r, off: pl.BlockSpec((r, tn), lambda j: (0, j + off))
    cb = conv_b.reshape(1, -1)
    ospec = pl.BlockSpec((m, tn), lambda j: (0, j))
    return pl.pallas_call(
        kern,
        out_shape=(jax.ShapeDtypeStruct((m, dff), BF16),
                   jax.ShapeDtypeStruct((m, dff), F32),
                   jax.ShapeDtypeStruct((m, dff), F32)),
        grid=(nj,),
        in_specs=[pl.BlockSpec((m, d), lambda j: (0, 0)),
                  wspec(0), wspec(nj), cspec(3, 0), cspec(3, nj), cspec(1, 0), cspec(1, nj),
                  cspec(m, 0), cspec(m, nj), cspec(m, 0), cspec(m, nj)],
        out_specs=(ospec, ospec, ospec),
        compiler_params=_cparams("parallel"),
        name="ffn_up_sample",
    )(h2, w_up, w_up, conv_w, conv_w, cb, cb, p1, p1, p2, p2)


def _layer(x, l, w, dims, attn_fn, mstate, gstate, conv_state, bsz, seq):
    (heads_a, dk_a, heads_b, heads_c, dk_c) = dims
    wa = heads_a * LANES
    wb = heads_b * LANES
    h = _rmsnorm(x, w["norm_mix_g"][l])
    z = _matmul(h, w["w_main"][l], name="proj_in")
    zs = _matmul(h, w["w_small"][l], name="proj_in_small")
    qn, kn, knb, vb = _qknorm(z, w["q_norm_g"][l], w["k_norm_g"][l], heads_a, dk_a)
    lam_init = 0.8 - 0.6 * math.exp(-0.3 * l)
    oa = attn_fn(l, qn, knb, vb, lam_init)

    chunk_b = SCAN_CHUNK if seq % SCAN_CHUNK == 0 else SAMPLE_CHUNK
    gates = zs[:, GATE_I:GATE_I + 2 * heads_b].reshape(bsz, seq, 2 * heads_b)
    gt = jnp.swapaxes(gates, 1, 2)
    if seq < chunk_b:
        gt = jnp.pad(gt, ((0, 0), (0, 0), (0, chunk_b - seq)))
    ob, c_f, n_f, m_f = _mlstm(z, zs, gt, w["mlstm_gate_b"][l], w["mlstm_norm_g"][l],
                               mstate[0], mstate[1], mstate[2], seq, chunk_b, (3 * wa) // wb)

    chunk_c = GLA_CHUNK if seq % GLA_CHUNK == 0 else SAMPLE_CHUNK
    kw = heads_c * dk_c
    vw = heads_c * LANES
    c0 = 3 * wa + 4 * wb
    oc, s_f = _gla(z, zs, w["gla_w_alpha"][l], w["gla_b_alpha"][l], w["gla_norm_g"][l], gstate,
                   seq, chunk_c, c0 // kw, c0 // kw + 1, (c0 + 2 * kw) // vw, (c0 + 2 * kw) // vw + 1)

    mix = jnp.concatenate([oa, ob, oc], axis=-1).astype(BF16)
    x1 = _matmul(mix, w["w_out"][l], res=x, name="proj_out")
    h2 = _rmsnorm(x1, w["norm_ffn_g"][l])
    if conv_state is None:
        act, tg, tv = _ffn_up_prompt(h2, w["w_up"][l], w["ffn_conv_w"][l], w["ffn_conv_b"][l], bsz, seq)
        conv_rows = jnp.concatenate([tg[:, 6:8], tv[:, 6:8]], axis=-1)
    else:
        act, ug, uv = _ffn_up_sample(h2, w["w_up"][l], w["ffn_conv_w"][l], w["ffn_conv_b"][l], conv_state, seq)
        u = jnp.concatenate([ug, uv], axis=-1).reshape(bsz, seq, -1)
        conv_rows = u[:, seq - 2:]
    x2 = _matmul(act, w["w_down"][l], res=x1, name="proj_down")

    k_rows = kn.reshape(bsz, seq, heads_a, LANES)
    v_rows = z[:, 2 * wa:3 * wa].reshape(bsz, seq, heads_a, LANES)
    return x2, (k_rows, v_rows, c_f, n_f, m_f[:, :, 0], s_f, conv_rows)


def kernel(x_prompt, x_sample, cache_k, cache_v, page_table, state_mlstm_C, state_mlstm_n, state_mlstm_m, state_gla_S, state_ffn_conv, norm_mix_g, w_in, q_norm_g, k_norm_g, diff_lambda, diff_subln_g, rel_bias, mlstm_gate_b, mlstm_norm_g, gla_w_alpha, gla_b_alpha, gla_norm_g, w_out, norm_ffn_g, ffn_w_up, ffn_conv_w, ffn_conv_b, ffn_w_down):
    depth = w_in.shape[0]
    bp, sp, d_model = x_prompt.shape
    bs, ss, _ = x_sample.shape
    heads_a, dv_a = cache_v.shape[3], cache_v.shape[4]
    dk_a = cache_k.shape[4] // 2
    heads_b, dk_b, dv_b = state_mlstm_C.shape[2:]
    heads_c, dk_c, dv_c = state_gla_S.shape[2:]
    rank = gla_w_alpha.shape[1]
    page = cache_k.shape[2]
    assert dv_a == LANES and 2 * dk_a == LANES and dk_b == LANES and dv_b == LANES and dv_c == LANES
    assert page >= MAX_DISTANCE and ATTN_TILE >= MAX_DISTANCE and rank <= GATE_I
    wa, wb = heads_a * LANES, heads_b * LANES
    n_main = 3 * wa + 4 * wb + 2 * heads_c * dk_c + 2 * heads_c * dv_c
    gate0 = 3 * wa + 4 * wb
    c0 = gate0 + 2 * heads_b
    assert w_in.shape[2] == n_main + 2 * heads_b + rank

    w_main = jnp.concatenate([w_in[:, :, :gate0], w_in[:, :, c0:c0 + n_main - gate0]], axis=-1).astype(BF16)
    w_small = jnp.zeros((depth, d_model, LANES), F32)
    w_small = w_small.at[:, :, :rank].set(w_in[:, :, n_main + 2 * heads_b:])
    w_small = w_small.at[:, :, GATE_I:GATE_I + 2 * heads_b].set(w_in[:, :, gate0:c0]).astype(BF16)
    w = dict(norm_mix_g=norm_mix_g, w_main=w_main, w_small=w_small, q_norm_g=q_norm_g, k_norm_g=k_norm_g,
             mlstm_gate_b=mlstm_gate_b, mlstm_norm_g=mlstm_norm_g, gla_w_alpha=gla_w_alpha,
             gla_b_alpha=gla_b_alpha, gla_norm_g=gla_norm_g, w_out=w_out.astype(BF16),
             norm_ffn_g=norm_ffn_g, w_up=ffn_w_up.astype(BF16), ffn_conv_w=ffn_conv_w,
             ffn_conv_b=ffn_conv_b, w_down=ffn_w_down.astype(BF16))
    dims = (heads_a, dk_a, heads_b, heads_c, dk_c)

    bias_p = _prompt_bias(rel_bias, ATTN_TILE)
    bias_pg, bias_new = _decode_bias(rel_bias, page, ss)
    kc = cache_k.reshape(depth, cache_k.shape[1], page * heads_a, LANES)
    vc = cache_v.reshape(depth, cache_v.shape[1], page * heads_a, LANES)

    def prompt_attn(l, qn, knb, vb, lam_init):
        shp = (bp, sp, wa)
        o = _prompt_attention(qn.reshape(shp), knb.reshape(shp), vb.reshape(shp), bias_p,
                              diff_lambda[l], diff_subln_g[l], lam_init, dk_a)
        return o.reshape(bp * sp, wa)

    def sample_attn(l, qn, knb, vb, lam_init):
        shp = (bs, ss, wa)
        o = _decode_attention(l, qn.reshape(shp), knb.reshape(shp), vb.reshape(shp), kc, vc, page_table,
                              bias_pg, bias_new, diff_lambda[l], diff_subln_g[l], lam_init, dk_a)
        return o.reshape(bs * ss, wa)

    xp = x_prompt.reshape(bp * sp, d_model)
    xs = x_sample.reshape(bs * ss, d_model)
    zero_m = (jnp.zeros((bp, heads_b, dk_b, dv_b), F32), jnp.zeros((bp, heads_b, dk_b), F32),
              jnp.zeros((bp, heads_b), F32))
    zero_g = jnp.zeros((bp, heads_c, dk_c, dv_c), F32)
    rows_p, rows_s = [], []
    for l in range(depth):
        xp, rp = _layer(xp, l, w, dims, prompt_attn, zero_m, zero_g, None, bp, sp)
        rows_p.append(rp)
        xs, rs = _layer(xs, l, w, dims, sample_attn,
                        (state_mlstm_C[l], state_mlstm_n[l], state_mlstm_m[l]), state_gla_S[l],
                        state_ffn_conv[l], bs, ss)
        rows_s.append(rs)

    def field(rows, i):
        return jnp.stack([r[i] for r in rows], axis=0)

    return (xp.reshape(bp, sp, d_model), xs.reshape(bs, ss, d_model),
            *[field(rows_p, i) for i in range(7)], *[field(rows_s, i) for i in range(7)])
```

```python
import functools
import math

import numpy as np
import jax
import jax.numpy as jnp
from jax import lax
from jax.experimental import pallas as pl
from jax.experimental.pallas import tpu as pltpu

F32 = jnp.float32
BF16 = jnp.bfloat16

LANES = 128
VMEM_LIMIT = 52 * 1024 * 1024
EPS = 1e-6
NEG = -1e30
GLA_TAU = 16.0
MAX_DISTANCE = 128
ATTN_TILE = 256
SCAN_CHUNK = 256
GLA_CHUNK = 128
SAMPLE_CHUNK = 128
DECODE_PAGES_PER_STEP = 4


def _cparams(*sem):
    return pltpu.CompilerParams(dimension_semantics=sem, vmem_limit_bytes=VMEM_LIMIT)


def _split_bf16(x):
    hi = x.astype(BF16)
    lo = (x - hi.astype(F32)).astype(BF16)
    return hi, lo


def _dot(a, b):
    return jnp.dot(a, b, preferred_element_type=F32)


def _dot_nt(a, b):
    return lax.dot_general(a, b, (((1,), (1,)), ((), ())), preferred_element_type=F32)


def _dot_tn(a, b):
    return lax.dot_general(a, b, (((0,), (0,)), ((), ())), preferred_element_type=F32)


def _log_sigmoid(x):
    return jnp.minimum(x, 0.0) - jnp.log(1.0 + jnp.exp(-jnp.abs(x)))


def _sigmoid(x):
    return 1.0 / (1.0 + jnp.exp(-x))


def _div_pow2(x, n):
    assert n & (n - 1) == 0
    return lax.shift_right_logical(x, n.bit_length() - 1)


def _mod_pow2(x, n):
    assert n & (n - 1) == 0
    return x & (n - 1)


def _pad_rows(x, rows):
    if x.shape[0] == rows:
        return x
    return jnp.concatenate([x, jnp.zeros((rows - x.shape[0], x.shape[1]), x.dtype)], axis=0)


def _rmsnorm_kernel(x_ref, g_ref, o_ref):
    x = x_ref[...]
    ms = jnp.mean(x * x, axis=-1, keepdims=True)
    o_ref[...] = (x * lax.rsqrt(ms + EPS) * g_ref[...]).astype(o_ref.dtype)


def _rmsnorm(x, g):
    m, d = x.shape
    tm = min(m, 512)
    return pl.pallas_call(
        _rmsnorm_kernel,
        out_shape=jax.ShapeDtypeStruct((m, d), BF16),
        grid=(m // tm,),
        in_specs=[pl.BlockSpec((tm, d), lambda i: (i, 0)),
                  pl.BlockSpec((1, d), lambda i: (0, 0))],
        out_specs=pl.BlockSpec((tm, d), lambda i: (i, 0)),
        compiler_params=_cparams("parallel"),
        name="rmsnorm",
    )(x, g.reshape(1, d))


def _mm_kernel(a_ref, w_ref, o_ref):
    o_ref[...] = _dot(a_ref[...], w_ref[...]).astype(o_ref.dtype)


def _mm_res_kernel(a_ref, w_ref, r_ref, o_ref):
    o_ref[...] = (r_ref[...] + _dot(a_ref[...], w_ref[...])).astype(o_ref.dtype)


def _matmul(a, w, layer, res=None, name="matmul"):
    m, k = a.shape
    n = w.shape[2]
    tm = min(m, 1024 if k <= 2048 else 512)
    tn = 512 if n % 512 == 0 else (256 if n % 256 == 0 else LANES)
    assert m % tm == 0 and n % tn == 0
    in_specs = [pl.BlockSpec((tm, k), lambda i, j: (i, 0)),
                pl.BlockSpec((None, k, tn), lambda i, j: (layer, 0, j))]
    args = [a, w]
    kern = _mm_kernel
    if res is not None:
        in_specs.append(pl.BlockSpec((tm, tn), lambda i, j: (i, j)))
        args.append(res)
        kern = _mm_res_kernel
    return pl.pallas_call(
        kern,
        out_shape=jax.ShapeDtypeStruct((m, n), F32),
        grid=(m // tm, n // tn),
        in_specs=in_specs,
        out_specs=pl.BlockSpec((tm, tn), lambda i, j: (i, j)),
        compiler_params=_cparams("parallel", "parallel"),
        name=name,
    )(*args)


def _proj_out_kernel(oa_ref, ob_ref, oc_ref, w_ref, r_ref, o_ref):
    ka, kb = oa_ref.shape[1], ob_ref.shape[1]
    acc = _dot(oa_ref[...].astype(BF16), w_ref[0:ka, :])
    acc += _dot(ob_ref[...].astype(BF16), w_ref[ka:ka + kb, :])
    acc += _dot(oc_ref[...].astype(BF16), w_ref[ka + kb:, :])
    o_ref[...] = r_ref[...] + acc


def _proj_out(oa, ob, oc, w, layer, res):
    m = oa.shape[0]
    k, n = w.shape[1], w.shape[2]
    tm = min(m, 512)
    tn = 512
    assert m % tm == 0 and n % tn == 0 and oa.shape[1] + ob.shape[1] + oc.shape[1] == k
    lhs = lambda x: pl.BlockSpec((tm, x.shape[1]), lambda i, j: (i, 0))
    return pl.pallas_call(
        _proj_out_kernel,
        out_shape=jax.ShapeDtypeStruct((m, n), F32),
        grid=(m // tm, n // tn),
        in_specs=[lhs(oa), lhs(ob), lhs(oc),
                  pl.BlockSpec((None, k, tn), lambda i, j: (layer, 0, j)),
                  pl.BlockSpec((tm, tn), lambda i, j: (i, j))],
        out_specs=pl.BlockSpec((tm, tn), lambda i, j: (i, j)),
        compiler_params=_cparams("parallel", "parallel"),
        name="proj_out",
    )(oa, ob, oc, w, res)


def _qknorm_kernel(aq_ref, ak_ref, av_ref, qg_ref, kg_ref, bd_ref,
                   qn_ref, kn_ref, knb_ref, vb_ref, *, heads, inv_dk, scale, v_transposed):
    bd = bd_ref[...]
    qg = qg_ref[...]
    kg = kg_ref[...]

    def norm(x, g):
        hi, lo = _split_bf16(x * x)
        ss = _dot(hi, bd) + _dot(lo, bd)
        return x * lax.rsqrt(ss * inv_dk + EPS) * g

    for h in range(heads):
        sl = slice(LANES * h, LANES * (h + 1))
        qn_ref[:, sl] = (norm(aq_ref[:, sl], qg) * scale).astype(BF16)
        kn = norm(ak_ref[:, sl], kg)
        kn_ref[:, sl] = kn
        knb_ref[:, sl] = kn.astype(BF16)
        if v_transposed:
            vb_ref[0, sl, :] = av_ref[:, sl].T.astype(BF16)
    if not v_transposed:
        vb_ref[...] = av_ref[...].astype(BF16)


def _qknorm(z, qg, kg, heads, dk, seq, v_transposed):
    m = z.shape[0]
    w = heads * LANES
    tm = min(m, 256)
    assert m % tm == 0
    lane = np.arange(LANES)
    bd = jnp.asarray((lane[:, None] // dk) == (lane[None, :] // dk), BF16)
    reps = LANES // dk
    kern = functools.partial(_qknorm_kernel, heads=heads, inv_dk=1.0 / dk, scale=dk ** -0.5,
                             v_transposed=v_transposed)
    rows = pl.BlockSpec((tm, w), lambda i: (i, 0))
    if v_transposed:
        tps = seq // tm
        v_shape = jax.ShapeDtypeStruct((m // seq, w, seq), BF16)
        v_spec = pl.BlockSpec((1, w, tm), lambda i: (i // tps, 0, i % tps))
    else:
        v_shape = jax.ShapeDtypeStruct((m, w), BF16)
        v_spec = rows
    return pl.pallas_call(
        kern,
        out_shape=(jax.ShapeDtypeStruct((m, w), BF16), jax.ShapeDtypeStruct((m, w), F32),
                   jax.ShapeDtypeStruct((m, w), BF16), v_shape),
        grid=(m // tm,),
        in_specs=[pl.BlockSpec((tm, w), lambda i: (i, 0)),
                  pl.BlockSpec((tm, w), lambda i: (i, 1)),
                  pl.BlockSpec((tm, w), lambda i: (i, 2)),
                  pl.BlockSpec((1, LANES), lambda i: (0, 0)),
                  pl.BlockSpec((1, LANES), lambda i: (0, 0)),
                  pl.BlockSpec((LANES, LANES), lambda i: (0, 0))],
        out_specs=(rows, rows, rows, v_spec),
        compiler_params=_cparams("parallel"),
        name="qknorm",
    )(z, z, z, jnp.tile(qg, reps).reshape(1, LANES), jnp.tile(kg, reps).reshape(1, LANES), bd)


def _bucket(n, n_buckets):
    max_exact = n_buckets // 2
    nf = jnp.maximum(n, max_exact).astype(F32)
    large = max_exact + (jnp.log(nf / max_exact) / math.log(MAX_DISTANCE / max_exact)
                         * (n_buckets - max_exact)).astype(jnp.int32)
    large = jnp.minimum(large, n_buckets - 1)
    return jnp.where(n < max_exact, jnp.maximum(n, 0), large)


def _prompt_bias_kernel(rb_ref, o_ref, *, tile, n_buckets):
    h = pl.program_id(0)
    j = lax.broadcasted_iota(jnp.int32, (tile, tile), 0)
    i = lax.broadcasted_iota(jnp.int32, (tile, tile), 1)
    far = jnp.full((tile, tile), MAX_DISTANCE, jnp.int32)
    for t, n in enumerate((i - j, tile + i - j, far)):
        b = _bucket(n, n_buckets)
        val = jnp.full((tile, tile), rb_ref[0, h], F32)
        for k in range(1, n_buckets):
            val = jnp.where(b == k, rb_ref[k, h], val)
        o_ref[0, t] = jnp.where(n >= 0, val, NEG)


def _prompt_bias(rel_bias, tile):
    n_buckets, heads = rel_bias.shape
    kern = functools.partial(_prompt_bias_kernel, tile=tile, n_buckets=n_buckets)
    return pl.pallas_call(
        kern,
        out_shape=jax.ShapeDtypeStruct((heads, 3, tile, tile), F32),
        grid=(heads,),
        in_specs=[pl.BlockSpec(memory_space=pltpu.SMEM)],
        out_specs=pl.BlockSpec((1, 3, tile, tile), lambda h: (h, 0, 0, 0)),
        compiler_params=_cparams("parallel"),
        name="prompt_bias",
    )(rel_bias)


def _decode_bias_kernel(rb_ref, pg_ref, new_ref, *, page, ppb, dec_seq, heads, n_buckets):
    grp = 2 * dec_seq

    def lookup(n, h):
        b = _bucket(n, n_buckets)
        val = jnp.full(n.shape, rb_ref[0, h], F32)
        for k in range(1, n_buckets):
            val = jnp.where(b == k, rb_ref[k, h], val)
        return jnp.where(n >= 0, val, NEG)

    qi = _mod_pow2(lax.broadcasted_iota(jnp.int32, (grp, page), 0), dec_seq)
    j = lax.broadcasted_iota(jnp.int32, (grp, page), 1)
    far = jnp.full((grp, page), MAX_DISTANCE, jnp.int32)
    for h in range(heads):
        rs = slice(grp * h, grp * (h + 1))
        far_b = lookup(far, h)
        for c in range(ppb):
            pg_ref[0, rs, page * c:page * (c + 1)] = far_b
            pg_ref[1, rs, page * c:page * (c + 1)] = far_b if c < ppb - 1 else lookup(page + qi - j, h)
        new_ref[rs, :] = lookup(jnp.where(j < dec_seq, qi - j, -1), h)


def _decode_bias(rel_bias, page, ppb, dec_seq):
    n_buckets, heads = rel_bias.shape
    rows = 2 * dec_seq * heads
    kern = functools.partial(_decode_bias_kernel, page=page, ppb=ppb, dec_seq=dec_seq, heads=heads,
                             n_buckets=n_buckets)
    return pl.pallas_call(
        kern,
        out_shape=(jax.ShapeDtypeStruct((2, rows, ppb * page), F32),
                   jax.ShapeDtypeStruct((rows, page), F32)),
        in_specs=[pl.BlockSpec(memory_space=pltpu.SMEM)],
        name="decode_bias",
    )(rel_bias)


def _diff_lambda(dl, lam_init):
    s1 = jnp.sum(dl[0:1] * dl[1:2], axis=-1, keepdims=True)
    s2 = jnp.sum(dl[2:3] * dl[3:4], axis=-1, keepdims=True)
    return jnp.exp(s1) - jnp.exp(s2) + lam_init


def _subln(o, g, lam_init):
    ms = jnp.mean(o * o, axis=-1, keepdims=True)
    return o * lax.rsqrt(ms + EPS) * g * (1.0 - lam_init)


def _flash_update(s, m_prev, l_prev):
    m_new = jnp.maximum(m_prev, jnp.max(s, axis=-1, keepdims=True))
    alpha = jnp.exp(m_prev - m_new)
    p = jnp.exp(s - m_new)
    l_new = alpha * l_prev + jnp.sum(p, axis=-1, keepdims=True)
    return p, alpha, m_new, l_new


def _attn_kernel(q_ref, k_ref, vt_ref, bias_ref, dl_ref, g_ref, o_ref, m_sc, l_sc, acc_sc,
                 *, heads, tile, dk, lam_init):
    qi = pl.program_id(1)
    ki = pl.program_id(2)
    n_sub = tile // LANES

    @pl.when(ki == 0)
    def _init():
        m_sc[...] = jnp.full(m_sc.shape, -jnp.inf, F32)
        l_sc[...] = jnp.zeros(l_sc.shape, F32)
        acc_sc[...] = jnp.zeros(acc_sc.shape, F32)

    @pl.when(ki <= qi)
    def _step():
        lane = lax.broadcasted_iota(jnp.int32, (LANES, LANES), 1)
        for h in range(heads):
            sl = slice(LANES * h, LANES * (h + 1))
            kh = k_ref[0, :, sl]
            vht = vt_ref[0, sl, :]
            for c in range(2 * n_sub):
                qrows = slice(LANES * (c % n_sub), LANES * (c % n_sub + 1))
                cs = slice(LANES * c, LANES * (c + 1))
                qc = q_ref[0, qrows, sl]
                own_map = (lane < dk) if c < n_sub else (lane >= dk)
                qc = jnp.where(own_map, qc, jnp.zeros_like(qc))
                s = _dot_nt(kh, qc) + bias_ref[h, 0, :, qrows]
                m_prev = m_sc[h, :, cs]
                m_new = jnp.maximum(m_prev, jnp.max(s, axis=0, keepdims=True))
                alpha = jnp.exp(m_prev - m_new)
                p = jnp.exp(s - m_new)
                l_sc[h, :, cs] = alpha * l_sc[h, :, cs] + jnp.sum(p, axis=0, keepdims=True)
                acc_sc[h, :, cs] = alpha * acc_sc[h, :, cs] + _dot(vht, p.astype(BF16))
                m_sc[h, :, cs] = m_new

    @pl.when(ki == qi)
    def _finish():
        lam = _diff_lambda(dl_ref[...], lam_init)
        g_col = g_ref[...]
        for h in range(heads):
            acc = acc_sc[h]
            l = l_sc[h]
            ot = acc[:, :tile] / l[:, :tile] - lam * (acc[:, tile:] / l[:, tile:])
            ms = jnp.mean(ot * ot, axis=0, keepdims=True)
            ot = ot * lax.rsqrt(ms + EPS) * g_col * (1.0 - lam_init)
            o_ref[0, :, LANES * h:LANES * (h + 1)] = ot.T


def _prompt_attention(qn, knb, vt, bias, dl, g, lam_init, dk):
    bsz, seq, w = qn.shape
    heads = w // LANES
    tile = bias.shape[-1]
    nq = seq // tile
    assert seq % tile == 0 and tile % LANES == 0
    kern = functools.partial(_attn_kernel, heads=heads, tile=tile, dk=dk, lam_init=lam_init)

    def bias_idx(b, qi, ki):
        return (0, jnp.where(ki >= qi, 0, jnp.where(ki == qi - 1, 1, 2)), 0, 0)

    return pl.pallas_call(
        kern,
        out_shape=jax.ShapeDtypeStruct((bsz, seq, w), F32),
        grid=(bsz, nq, nq),
        in_specs=[pl.BlockSpec((1, tile, w), lambda b, qi, ki: (b, qi, 0)),
                  pl.BlockSpec((1, tile, w), lambda b, qi, ki: (b, jnp.minimum(ki, qi), 0)),
                  pl.BlockSpec((1, w, tile), lambda b, qi, ki: (b, 0, jnp.minimum(ki, qi))),
                  pl.BlockSpec((heads, 1, tile, tile), bias_idx),
                  pl.BlockSpec(dl.shape, lambda b, qi, ki: (0, 0)),
                  pl.BlockSpec((LANES, 1), lambda b, qi, ki: (0, 0))],
        out_specs=pl.BlockSpec((1, tile, w), lambda b, qi, ki: (b, qi, 0)),
        scratch_shapes=[pltpu.VMEM((heads, 1, 2 * tile), F32),
                        pltpu.VMEM((heads, 1, 2 * tile), F32),
                        pltpu.VMEM((heads, LANES, 2 * tile), F32)],
        compiler_params=_cparams("parallel", "parallel", "arbitrary"),
        name="prompt_attention",
    )(qn, knb, vt, bias, dl, g.reshape(LANES, 1))


def _decode_attn_kernel(pt_ref, q_ref, kn_ref, vn_ref, *refs, heads, dec_seq, page, ppb, dk, lam_init):
    del pt_ref
    kc_refs, vc_refs = refs[:ppb], refs[ppb:2 * ppb]
    bias_ref, bnew_ref, dl_ref, g_ref, o_ref, wq_sc, wqb_sc, m_sc, l_sc, acc_sc = refs[2 * ppb:]
    p_idx = pl.program_id(1)
    rows = 2 * dec_seq * heads
    grp = 2 * dec_seq
    width = heads * LANES

    @pl.when(p_idx == 0)
    def _init():
        q = q_ref[0].astype(F32)
        lane = lax.broadcasted_iota(jnp.int32, (dec_seq, LANES), 1)
        pieces = []
        for h in range(heads):
            qh = q[:, LANES * h:LANES * (h + 1)]
            pieces += [jnp.where(lane < dk, qh, 0.0), jnp.where(lane >= dk, qh, 0.0)]
        wq = jnp.concatenate(pieces, axis=0)
        wq_sc[...] = wq.astype(BF16)
        c = lax.broadcasted_iota(jnp.int32, (rows, width), 1)
        r = lax.broadcasted_iota(jnp.int32, (rows, width), 0)
        own = _div_pow2(r, grp) == _div_pow2(c, LANES)
        wqb_sc[...] = jnp.where(own, jnp.concatenate([wq] * heads, axis=1), 0.0).astype(BF16)
        m_sc[...] = jnp.full(m_sc.shape, -jnp.inf, F32)
        l_sc[...] = jnp.zeros(l_sc.shape, F32)
        acc_sc[...] = jnp.zeros(acc_sc.shape, F32)

    def head_rows(ref, h):
        return ref[pl.ds(h, page, stride=heads), :].astype(BF16)

    s = jnp.concatenate(
        [jnp.concatenate([_dot_nt(wq_sc[grp * h:grp * (h + 1), :], head_rows(kc, h)) for h in range(heads)],
                         axis=0) for kc in kc_refs], axis=1) + bias_ref[0]
    p, alpha, m_new, l_new = _flash_update(s, m_sc[...], l_sc[...])
    m_sc[...] = m_new
    l_sc[...] = l_new
    pb = p.astype(BF16)
    for h in range(heads):
        rs = slice(grp * h, grp * (h + 1))
        acc = alpha[rs] * acc_sc[rs, :]
        for j, vc in enumerate(vc_refs):
            acc += _dot(pb[rs, page * j:page * (j + 1)], head_rows(vc, h))
        acc_sc[rs, :] = acc

    @pl.when(p_idx == pl.num_programs(1) - 1)
    def _finish():
        kn = _pad_rows(kn_ref[0].astype(F32), page).astype(BF16)
        vn = _pad_rows(vn_ref[0].astype(F32), page)
        s = _dot_nt(wqb_sc[...], kn) + bnew_ref[...]
        p, alpha, _, l_fin = _flash_update(s, m_sc[...], l_sc[...])
        pb = p.astype(BF16)
        lam = _diff_lambda(dl_ref[...], lam_init)
        g = g_ref[...]
        for h in range(heads):
            rs = slice(grp * h, grp * (h + 1))
            vh = vn[:, LANES * h:LANES * (h + 1)].astype(BF16)
            acc = alpha[rs] * acc_sc[rs, :] + _dot(pb[rs], vh)
            l = l_fin[rs]
            o = acc[:dec_seq] / l[:dec_seq] - lam * (acc[dec_seq:] / l[dec_seq:])
            o_ref[0, :, LANES * h:LANES * (h + 1)] = _subln(o, g, lam_init)


def _decode_attention(layer, qn, knb, vb, cache_k, cache_v, page_table, bias_pg, bias_new, dl, g, lam_init, dk):
    bsz, dec_seq, w = qn.shape
    heads = w // LANES
    page = cache_k.shape[2] // heads
    ppb = bias_pg.shape[-1] // page
    n_steps = page_table.shape[1] // ppb
    assert page_table.shape[1] % ppb == 0
    rows = 2 * dec_seq * heads
    kern = functools.partial(_decode_attn_kernel, heads=heads, dec_seq=dec_seq, page=page, ppb=ppb, dk=dk,
                             lam_init=lam_init)
    new_spec = pl.BlockSpec((1, dec_seq, w), lambda b, p, pt: (b, 0, 0))
    cache_specs = [pl.BlockSpec((None, None, page * heads, LANES),
                                lambda b, p, pt, j=j: (layer, pt[b, p * ppb + j], 0, 0)) for j in range(ppb)]
    grid_spec = pltpu.PrefetchScalarGridSpec(
        num_scalar_prefetch=1,
        grid=(bsz, n_steps),
        in_specs=[new_spec, new_spec, new_spec, *cache_specs, *cache_specs,
                  pl.BlockSpec((1, rows, ppb * page),
                               lambda b, p, pt: (jnp.where(p == n_steps - 1, 1, 0), 0, 0)),
                  pl.BlockSpec((rows, page), lambda b, p, pt: (0, 0)),
                  pl.BlockSpec(dl.shape, lambda b, p, pt: (0, 0)),
                  pl.BlockSpec((1, LANES), lambda b, p, pt: (0, 0))],
        out_specs=pl.BlockSpec((1, dec_seq, w), lambda b, p, pt: (b, 0, 0)),
        scratch_shapes=[pltpu.VMEM((rows, LANES), BF16),
                        pltpu.VMEM((rows, w), BF16),
                        pltpu.VMEM((rows, 1), F32),
                        pltpu.VMEM((rows, 1), F32),
                        pltpu.VMEM((rows, LANES), F32)],
    )
    return pl.pallas_call(
        kern,
        out_shape=jax.ShapeDtypeStruct((bsz, dec_seq, w), F32),
        grid_spec=grid_spec,
        compiler_params=_cparams("parallel", "arbitrary"),
        name="decode_attention",
    )(page_table, qn, knb, vb, *([cache_k] * ppb), *([cache_v] * ppb), bias_pg, bias_new, dl,
      g.reshape(1, LANES))


GATE_I = 16
GATE_F = 20


def _mlstm_kernel(q_ref, k_ref, v_ref, og_ref, zs_ref, gt_ref, gbl_ref, gbc_ref, g_ref,
                  c0_ref, n0_ref, m0_ref,
                  o_ref, cf_ref, nf_ref, mf_ref, c_sc, n_sc, m_sc,
                  *, heads, chunk, valid, scale):
    c_idx = pl.program_id(1)

    @pl.when(c_idx == 0)
    def _init():
        c_sc[...] = c0_ref[0]
        n_sc[...] = n0_ref[0]
        m_sc[...] = m0_ref[0]

    rows_in = q_ref.shape[0]
    row = lax.broadcasted_iota(jnp.int32, (chunk, 1), 0)
    t_i = lax.broadcasted_iota(jnp.int32, (chunk, chunk), 0)
    s_i = lax.broadcasted_iota(jnp.int32, (chunk, chunk), 1)
    causal = t_i >= s_i
    tril = causal.astype(BF16)
    triu = (t_i <= s_i).astype(BF16)

    gcol = _pad_rows(zs_ref[...], chunk) + gbl_ref[...]
    lf_mat = _log_sigmoid(gcol)
    if valid < chunk:
        lf_mat = jnp.where(row < valid, lf_mat, 0.0)
    hi, lo = _split_bf16(lf_mat)
    b_mat = _dot(tril, hi) + _dot(tril, lo)

    grow = gt_ref[0] + gbc_ref[...]
    col = lax.broadcasted_iota(jnp.int32, grow.shape, 1)
    grow_id = lax.broadcasted_iota(jnp.int32, grow.shape, 0)
    lf_rows = jnp.where(grow_id >= heads, _log_sigmoid(grow), 0.0)
    li_rows = grow
    if valid < chunk:
        lf_rows = jnp.where(col < valid, lf_rows, 0.0)
        li_rows = jnp.where(col < valid, li_rows, NEG)
    hi, lo = _split_bf16(lf_rows)
    b_rows = _dot(hi, triu) + _dot(lo, triu)

    g = g_ref[...]
    for h in range(heads):
        sl = slice(LANES * h, LANES * (h + 1))
        b_col = b_mat[:, GATE_F + h:GATE_F + h + 1]
        li_col = gcol[:, GATE_I + h:GATE_I + h + 1]
        if valid < chunk:
            li_col = jnp.where(row < valid, li_col, NEG)
        b_row = b_rows[heads + h:heads + h + 1, :]
        li_row = li_rows[h:h + 1, :]
        d = jnp.where(causal, b_col - b_row + li_row, NEG)
        m_prev = m_sc[h:h + 1, 0:1]
        inter = b_col + m_prev
        mt = jnp.maximum(inter, jnp.max(d, axis=-1, keepdims=True))
        wi = jnp.exp(inter - mt)
        q = _pad_rows(q_ref[:, sl], chunk)
        ks = _pad_rows(k_ref[:, sl], chunk) * scale
        vb = _pad_rows(v_ref[:, sl], chunk).astype(BF16)
        qb = q.astype(BF16)
        p = _dot_nt(qb, ks.astype(BF16)) * jnp.exp(d - mt)
        c_prev = c_sc[h]
        n_prev = n_sc[h:h + 1, :]
        num = wi * _dot(qb, c_prev.astype(BF16)) + _dot(p.astype(BF16), vb)
        den = wi * jnp.sum(q * n_prev, axis=-1, keepdims=True) + jnp.sum(p, axis=-1, keepdims=True)
        hh = num / jnp.maximum(jnp.abs(den), jnp.exp(-mt))
        og = _sigmoid(og_ref[:, sl])
        y = og * hh[:rows_in]
        ms = jnp.mean(y * y, axis=-1, keepdims=True)
        o_ref[:, sl] = y * lax.rsqrt(ms + EPS) * g

        m_new = mt[chunk - 1:chunk, :]
        b_last = b_col[chunk - 1:chunk, :]
        a = jnp.exp(b_last + m_prev - m_new)
        ws = jnp.exp(b_last - b_col + li_col - m_new)
        kw = ks * ws
        c_sc[h] = a * c_prev + _dot_tn(kw.astype(BF16), vb)
        n_sc[h:h + 1, :] = a * n_prev + jnp.sum(kw, axis=0, keepdims=True)
        m_sc[h:h + 1, :] = jnp.broadcast_to(m_new, (1, LANES))

    @pl.when(c_idx == pl.num_programs(1) - 1)
    def _finish():
        cf_ref[0] = c_sc[...]
        nf_ref[0] = n_sc[...]
        mf_ref[0] = m_sc[...]


def _mlstm(z, zs, gt, gate_b, g, c0, n0, m0, seq, chunk, col0):
    bsz, heads = c0.shape[0], c0.shape[1]
    w = heads * LANES
    rows_in = min(seq, chunk)
    nc = seq // rows_in
    valid = rows_in
    gt_w = gt.shape[-1] // nc
    kern = functools.partial(_mlstm_kernel, heads=heads, chunk=chunk, valid=valid, scale=LANES ** -0.5)
    gbl = jnp.zeros((1, LANES), F32)
    gbl = gbl.at[0, GATE_I:GATE_I + heads].set(gate_b[0]).at[0, GATE_F:GATE_F + heads].set(gate_b[1])
    gbc = gate_b.reshape(2 * heads, 1)
    m0b = jnp.broadcast_to(m0[:, :, None], (bsz, heads, LANES))

    def zspec(blk):
        return pl.BlockSpec((rows_in, w), lambda b, c: (b * nc + c, col0 + blk))

    state = lambda shape: pl.BlockSpec((1,) + shape, lambda b, c: (b,) + (0,) * len(shape))
    return pl.pallas_call(
        kern,
        out_shape=(jax.ShapeDtypeStruct((bsz * seq, w), F32),
                   jax.ShapeDtypeStruct((bsz, heads, LANES, LANES), F32),
                   jax.ShapeDtypeStruct((bsz, heads, LANES), F32),
                   jax.ShapeDtypeStruct((bsz, heads, LANES), F32)),
        grid=(bsz, nc),
        in_specs=[zspec(0), zspec(1), zspec(2), zspec(3),
                  pl.BlockSpec((rows_in, LANES), lambda b, c: (b * nc + c, 0)),
                  pl.BlockSpec((1, 2 * heads, gt_w), lambda b, c: (b, 0, c)),
                  pl.BlockSpec((1, LANES), lambda b, c: (0, 0)),
                  pl.BlockSpec((2 * heads, 1), lambda b, c: (0, 0)),
                  pl.BlockSpec((1, LANES), lambda b, c: (0, 0)),
                  state((heads, LANES, LANES)), state((heads, LANES)), state((heads, LANES))],
        out_specs=(pl.BlockSpec((rows_in, w), lambda b, c: (b * nc + c, 0)),
                   state((heads, LANES, LANES)), state((heads, LANES)), state((heads, LANES))),
        scratch_shapes=[pltpu.VMEM((heads, LANES, LANES), F32),
                        pltpu.VMEM((heads, LANES), F32),
                        pltpu.VMEM((heads, LANES), F32)],
        compiler_params=_cparams("parallel", "arbitrary"),
        name="mlstm",
    )(z, z, z, z, zs, gt, gbl, gbc, g.reshape(1, LANES), c0, n0, m0b)


def _gla_levels(chunk):
    n, out = chunk, []
    while n >= 2:
        out.append(n)
        n //= 2
    return out


def _gla_weights(chunk):
    t = np.arange(chunk)[:, None]
    s = np.arange(chunk)[None, :]
    blocks = [(s <= t).astype(np.float32), (s > t).astype(np.float32)]
    for n in _gla_levels(chunk):
        mid = (t // n) * n + n // 2 - 1
        blocks.append(((s > mid) & (s <= t)).astype(np.float32) - ((s > t) & (s <= mid)).astype(np.float32))
    return jnp.asarray(np.concatenate(blocks, axis=0), BF16)


def _gla_kernel(q_ref, k_ref, v_ref, gg_ref, zs_ref, wa_ref, ba_ref, ws_ref, g_ref, s0_ref,
                o_ref, sf_ref, s_sc, *, heads, chunk, valid, dk, scale):
    c_idx = pl.program_id(1)

    @pl.when(c_idx == 0)
    def _init():
        s_sc[...] = s0_ref[0]

    rows_in = q_ref.shape[0]
    row = lax.broadcasted_iota(jnp.int32, (chunk, 1), 0)
    t_i = lax.broadcasted_iota(jnp.int32, (chunk, chunk), 0)
    s_i = lax.broadcasted_iota(jnp.int32, (chunk, chunk), 1)

    zs = _pad_rows(zs_ref[...], chunk)
    la = _log_sigmoid(_dot(zs.astype(BF16), wa_ref[...]) + ba_ref[...]) * (1.0 / GLA_TAU)
    if valid < chunk:
        la = jnp.where(row < valid, la, 0.0)
    hi, lo = _split_bf16(la)
    wst = ws_ref[...]
    e_all = _dot(wst, hi) + _dot(wst, lo)
    ones = jnp.ones((chunk, LANES), BF16)
    levels = _gla_levels(chunk)
    g = g_ref[...]

    for h in range(heads):
        ksl = slice(dk * h, dk * (h + 1))
        vsl = slice(LANES * h, LANES * (h + 1))
        q = _pad_rows(q_ref[:, ksl], chunk) * scale
        k = _pad_rows(k_ref[:, ksl], chunk)
        vb = _pad_rows(v_ref[:, vsl], chunk).astype(BF16)
        bc = e_all[0:chunk, ksl]
        rem = e_all[chunk:2 * chunk, ksl]
        a_mat = jnp.zeros((chunk, chunk), F32)
        for li, n in enumerate(levels):
            e = e_all[(2 + li) * chunk:(3 + li) * chunk, ksl]
            second = (row & (n - 1)) >= n // 2
            qt = jnp.where(second, q * jnp.exp(jnp.minimum(e, 0.0)), 0.0)
            kt = jnp.where(second, 0.0, k * jnp.exp(jnp.minimum(-e, 0.0)))
            shift = n.bit_length() - 1
            same = lax.shift_right_logical(t_i, shift) == lax.shift_right_logical(s_i, shift)
            a_mat = a_mat + jnp.where(same, _dot_nt(qt.astype(BF16), kt.astype(BF16)), 0.0)
        s_prev = s_sc[h]
        qd = (q * jnp.exp(bc)).astype(BF16)
        o = _dot(qd, s_prev.astype(BF16)) + _dot(a_mat.astype(BF16), vb)
        o = o + jnp.sum(q * k, axis=-1, keepdims=True) * vb.astype(F32)
        o = o[:rows_in]
        ms = jnp.mean(o * o, axis=-1, keepdims=True)
        gate = gg_ref[:, vsl]
        o_ref[:, vsl] = o * lax.rsqrt(ms + EPS) * g * (gate * _sigmoid(gate))

        la_h = la[:, ksl]
        hi_h, lo_h = _split_bf16(la_h)
        tot = _dot_tn(hi_h, ones) + _dot_tn(lo_h, ones)
        kd = (k * jnp.exp(rem)).astype(BF16)
        s_sc[h] = jnp.exp(tot) * s_prev + _dot_tn(kd, vb)

    @pl.when(c_idx == pl.num_programs(1) - 1)
    def _finish():
        sf_ref[0] = s_sc[...]


def _gla(z, zs, w_alpha, b_alpha, g, s0, seq, chunk, qcol, kcol, vcol, gcol):
    bsz, heads, dk, dv = s0.shape
    rows_in = min(seq, chunk)
    nc = seq // rows_in
    kw = heads * dk
    vw = heads * dv
    wa = jnp.zeros((LANES, kw), F32).at[:w_alpha.shape[0]].set(w_alpha).astype(BF16)
    wst = _gla_weights(chunk)
    kern = functools.partial(_gla_kernel, heads=heads, chunk=chunk, valid=rows_in, dk=dk, scale=dk ** -0.5)
    const = lambda shape: pl.BlockSpec(shape, lambda b, c: (0,) * len(shape))
    return pl.pallas_call(
        kern,
        out_shape=(jax.ShapeDtypeStruct((bsz * seq, vw), F32),
                   jax.ShapeDtypeStruct((bsz, heads, dk, dv), F32)),
        grid=(bsz, nc),
        in_specs=[pl.BlockSpec((rows_in, kw), lambda b, c: (b * nc + c, qcol)),
                  pl.BlockSpec((rows_in, kw), lambda b, c: (b * nc + c, kcol)),
                  pl.BlockSpec((rows_in, vw), lambda b, c: (b * nc + c, vcol)),
                  pl.BlockSpec((rows_in, vw), lambda b, c: (b * nc + c, gcol)),
                  pl.BlockSpec((rows_in, LANES), lambda b, c: (b * nc + c, 0)),
                  const((LANES, kw)), const((1, kw)), const(wst.shape), const((1, LANES)),
                  pl.BlockSpec((1, heads, dk, dv), lambda b, c: (b, 0, 0, 0))],
        out_specs=(pl.BlockSpec((rows_in, vw), lambda b, c: (b * nc + c, 0)),
                   pl.BlockSpec((1, heads, dk, dv), lambda b, c: (b, 0, 0, 0))),
        scratch_shapes=[pltpu.VMEM((heads, dk, dv), F32)],
        compiler_params=_cparams("parallel", "arbitrary"),
        name="gla",
    )(z, z, z, z, zs, wa, b_alpha.reshape(1, kw), wst, g.reshape(1, LANES), s0)


def _conv_gate(ug, uv, cwg, cwv, cbg, cbv, prev):
    def conv(u, cw, cb, which):
        u1, u2 = prev(u, which)
        return cb + cw[0:1] * u2 + cw[1:2] * u1 + cw[2:3] * u
    cg = conv(ug, cwg, cbg, 0)
    cv = conv(uv, cwv, cbv, 1)
    return cg * _sigmoid(cg) * cv


FFN_SUB = 256


def _ffn_up_prompt_kernel(h_ref, wg_ref, wv_ref, cwg_ref, cwv_ref, cbg_ref, cbv_ref,
                          act_ref, tg_ref, tv_ref, carry_sc, *, tiles_per_seq):
    i = pl.program_id(1)
    first = (i % tiles_per_seq) == 0
    hb = h_ref[...]
    tm = hb.shape[0]
    row8 = lax.broadcasted_iota(jnp.int32, (8, 1), 0)

    @pl.when(first)
    def _reset():
        carry_sc[...] = jnp.zeros(carry_sc.shape, F32)

    def conv_half(which, u, cw_ref, cb_ref, t_ref, cs):
        cw = cw_ref[:, cs]
        cb = cb_ref[:, cs]
        r1 = pltpu.roll(u, 1, 0)
        r2 = pltpu.roll(u, 2, 0)
        body = cb + cw[0:1] * r2 + cw[1:2] * r1 + cw[2:3] * u
        c = carry_sc[which, :, cs]
        t1 = jnp.where(row8 == 0, c[7:8], r1[0:8])
        t2 = jnp.where(row8 == 0, c[6:7], jnp.where(row8 == 1, c[7:8], r2[0:8]))
        top = cb + cw[0:1] * t2 + cw[1:2] * t1 + cw[2:3] * u[0:8]
        carry_sc[which, :, cs] = u[tm - 8:tm]
        t_ref[0, :, cs] = u[tm - 8:tm]
        return jnp.concatenate([top, body[8:]], axis=0)

    subs = [slice(c0, c0 + FFN_SUB) for c0 in range(0, wg_ref.shape[1], FFN_SUB)]
    ups = [(_dot(hb, wg_ref[:, cs]), _dot(hb, wv_ref[:, cs])) for cs in subs]
    for cs, (ug, uv) in zip(subs, ups):
        cg = conv_half(0, ug, cwg_ref, cbg_ref, tg_ref, cs)
        cv = conv_half(1, uv, cwv_ref, cbv_ref, tv_ref, cs)
        act_ref[:, cs] = (cg * _sigmoid(cg) * cv).astype(BF16)


def _ffn_up_prompt(h2, w_up, layer, conv_w, conv_b, bsz, seq):
    m, d = h2.shape
    dff = w_up.shape[2] // 2
    tm, tn = 512, 512
    assert seq % tm == 0 and dff % tn == 0 and tn % FFN_SUB == 0
    nj = dff // tn
    tps = seq // tm
    kern = functools.partial(_ffn_up_prompt_kernel, tiles_per_seq=tps)
    wspec = lambda off: pl.BlockSpec((None, d, tn), lambda j, i: (layer, 0, j + off))
    cspec = lambda r, off: pl.BlockSpec((r, tn), lambda j, i: (0, j + off))
    tail = pl.BlockSpec((1, 8, tn), lambda j, i: (i // tps, 0, j))
    cb = conv_b.reshape(1, -1)
    return pl.pallas_call(
        kern,
        out_shape=(jax.ShapeDtypeStruct((m, dff), BF16),
                   jax.ShapeDtypeStruct((bsz, 8, dff), F32),
                   jax.ShapeDtypeStruct((bsz, 8, dff), F32)),
        grid=(nj, m // tm),
        in_specs=[pl.BlockSpec((tm, d), lambda j, i: (i, 0)),
                  wspec(0), wspec(nj), cspec(3, 0), cspec(3, nj), cspec(1, 0), cspec(1, nj)],
        out_specs=(pl.BlockSpec((tm, tn), lambda j, i: (i, j)), tail, tail),
        scratch_shapes=[pltpu.VMEM((2, 8, tn), F32)],
        compiler_params=_cparams("parallel", "arbitrary"),
        name="ffn_up_prompt",
    )(h2, w_up, w_up, conv_w, conv_w, cb, cb)


def _ffn_up_sample_kernel(h_ref, wg_ref, wv_ref, cwg_ref, cwv_ref, cbg_ref, cbv_ref,
                          p1g_ref, p1v_ref, p2g_ref, p2v_ref, act_ref, ug_ref, uv_ref, *, seq):
    hb = h_ref[...]
    ug = _dot(hb, wg_ref[...])
    uv = _dot(hb, wv_ref[...])
    tm = ug.shape[0]
    pos = _mod_pow2(lax.broadcasted_iota(jnp.int32, (tm, 1), 0), seq)
    p1 = (p1g_ref, p1v_ref)
    p2 = (p2g_ref, p2v_ref)

    def prev(u, which):
        u1 = jnp.where(pos >= 1, pltpu.roll(u, 1, 0), p1[which][...])
        u2 = jnp.where(pos >= 2, pltpu.roll(u, 2, 0), p2[which][...])
        return u1, u2

    act_ref[...] = _conv_gate(ug, uv, cwg_ref[...], cwv_ref[...], cbg_ref[...], cbv_ref[...], prev).astype(BF16)
    ug_ref[...] = ug
    uv_ref[...] = uv


def _ffn_up_sample(h2, w_up, layer, conv_w, conv_b, conv_state, seq):
    m, d = h2.shape
    dff = w_up.shape[2] // 2
    tn = 512
    assert dff % tn == 0
    nj = dff // tn
    bsz = m // seq
    p1 = jnp.zeros((bsz, seq, 2 * dff), F32).at[:, 0].set(conv_state[:, 1]).reshape(m, 2 * dff)
    p2 = jnp.zeros((bsz, seq, 2 * dff), F32).at[:, 0].set(conv_state[:, 0]).at[:, 1].set(conv_state[:, 1])
    p2 = p2.reshape(m, 2 * dff)
    kern = functools.partial(_ffn_up_sample_kernel, seq=seq)
    wspec = lambda off: pl.BlockSpec((None, d, tn), lambda j: (layer, 0, j + off))
    cspec = lambda r, off: pl.BlockSpec((r, tn), lambda j: (0, j + off))
    cb = conv_b.reshape(1, -1)
    ospec = pl.BlockSpec((m, tn), lambda j: (0, j))
    return pl.pallas_call(
        kern,
        out_shape=(jax.ShapeDtypeStruct((m, dff), BF16),
                   jax.ShapeDtypeStruct((m, dff), F32),
                   jax.ShapeDtypeStruct((m, dff), F32)),
        grid=(nj,),
        in_specs=[pl.BlockSpec((m, d), lambda j: (0, 0)),
                  wspec(0), wspec(nj), cspec(3, 0), cspec(3, nj), cspec(1, 0), cspec(1, nj),
                  cspec(m, 0), cspec(m, nj), cspec(m, 0), cspec(m, nj)],
        out_specs=(ospec, ospec, ospec),
        compiler_params=_cparams("parallel"),
        name="ffn_up_sample",
    )(h2, w_up, w_up, conv_w, conv_w, cb, cb, p1, p1, p2, p2)


def _layer(x, l, w, dims, attn_fn, mstate, gstate, conv_state, bsz, seq):
    (heads_a, dk_a, heads_b, heads_c, dk_c) = dims
    wa = heads_a * LANES
    wb = heads_b * LANES
    prompt = conv_state is None
    h = _rmsnorm(x, w["norm_mix_g"][l])
    z = _matmul(h, w["w_main"], l, name="proj_in")
    zs = _matmul(h, w["w_small"], l, name="proj_in_small")
    qn, kn, knb, vb = _qknorm(z, w["q_norm_g"][l], w["k_norm_g"][l], heads_a, dk_a, seq, v_transposed=prompt)
    lam_init = 0.8 - 0.6 * math.exp(-0.3 * l)
    oa = attn_fn(l, qn, knb, vb, lam_init)

    chunk_b = SCAN_CHUNK if seq % SCAN_CHUNK == 0 else SAMPLE_CHUNK
    gates = zs[:, GATE_I:GATE_I + 2 * heads_b].reshape(bsz, seq, 2 * heads_b)
    gt = jnp.swapaxes(gates, 1, 2)
    if seq < chunk_b:
        gt = jnp.pad(gt, ((0, 0), (0, 0), (0, chunk_b - seq)))
    ob, c_f, n_f, m_f = _mlstm(z, zs, gt, w["mlstm_gate_b"][l], w["mlstm_norm_g"][l],
                               mstate[0], mstate[1], mstate[2], seq, chunk_b, (3 * wa) // wb)

    chunk_c = GLA_CHUNK if seq % GLA_CHUNK == 0 else SAMPLE_CHUNK
    kw = heads_c * dk_c
    vw = heads_c * LANES
    c0 = 3 * wa + 4 * wb
    oc, s_f = _gla(z, zs, w["gla_w_alpha"][l], w["gla_b_alpha"][l], w["gla_norm_g"][l], gstate,
                   seq, chunk_c, c0 // kw, c0 // kw + 1, (c0 + 2 * kw) // vw, (c0 + 2 * kw) // vw + 1)

    x1 = _proj_out(oa, ob, oc, w["w_out"], l, x)
    h2 = _rmsnorm(x1, w["norm_ffn_g"][l])
    if prompt:
        act, tg, tv = _ffn_up_prompt(h2, w["w_up"], l, w["ffn_conv_w"][l], w["ffn_conv_b"][l], bsz, seq)
        conv_rows = jnp.concatenate([tg[:, 6:8], tv[:, 6:8]], axis=-1)
    else:
        act, ug, uv = _ffn_up_sample(h2, w["w_up"], l, w["ffn_conv_w"][l], w["ffn_conv_b"][l], conv_state, seq)
        u = jnp.concatenate([ug, uv], axis=-1).reshape(bsz, seq, -1)
        conv_rows = u[:, seq - 2:]
    x2 = _matmul(act, w["w_down"], l, res=x1, name="proj_down")

    k_rows = kn.reshape(bsz, seq, heads_a, LANES)
    v_rows = z[:, 2 * wa:3 * wa].reshape(bsz, seq, heads_a, LANES)
    return x2, (k_rows, v_rows, c_f, n_f, m_f[:, :, 0], s_f, conv_rows)


def kernel(x_prompt, x_sample, cache_k, cache_v, page_table, state_mlstm_C, state_mlstm_n, state_mlstm_m, state_gla_S, state_ffn_conv, norm_mix_g, w_in, q_norm_g, k_norm_g, diff_lambda, diff_subln_g, rel_bias, mlstm_gate_b, mlstm_norm_g, gla_w_alpha, gla_b_alpha, gla_norm_g, w_out, norm_ffn_g, ffn_w_up, ffn_conv_w, ffn_conv_b, ffn_w_down):
    depth = w_in.shape[0]
    bp, sp, d_model = x_prompt.shape
    bs, ss, _ = x_sample.shape
    heads_a, dv_a = cache_v.shape[3], cache_v.shape[4]
    dk_a = cache_k.shape[4] // 2
    heads_b, dk_b, dv_b = state_mlstm_C.shape[2:]
    heads_c, dk_c, dv_c = state_gla_S.shape[2:]
    rank = gla_w_alpha.shape[1]
    page = cache_k.shape[2]
    assert dv_a == LANES and 2 * dk_a == LANES and dk_b == LANES and dv_b == LANES and dv_c == LANES
    assert page >= MAX_DISTANCE and ATTN_TILE >= MAX_DISTANCE and rank <= GATE_I
    wa, wb = heads_a * LANES, heads_b * LANES
    n_main = 3 * wa + 4 * wb + 2 * heads_c * dk_c + 2 * heads_c * dv_c
    gate0 = 3 * wa + 4 * wb
    c0 = gate0 + 2 * heads_b
    assert w_in.shape[2] == n_main + 2 * heads_b + rank

    w_main = jnp.concatenate([w_in[:, :, :gate0], w_in[:, :, c0:c0 + n_main - gate0]], axis=-1).astype(BF16)
    w_small = jnp.zeros((depth, d_model, LANES), F32)
    w_small = w_small.at[:, :, :rank].set(w_in[:, :, n_main + 2 * heads_b:])
    w_small = w_small.at[:, :, GATE_I:GATE_I + 2 * heads_b].set(w_in[:, :, gate0:c0]).astype(BF16)
    w = dict(norm_mix_g=norm_mix_g, w_main=w_main, w_small=w_small, q_norm_g=q_norm_g, k_norm_g=k_norm_g,
             mlstm_gate_b=mlstm_gate_b, mlstm_norm_g=mlstm_norm_g, gla_w_alpha=gla_w_alpha,
             gla_b_alpha=gla_b_alpha, gla_norm_g=gla_norm_g, w_out=w_out.astype(BF16),
             norm_ffn_g=norm_ffn_g, w_up=ffn_w_up.astype(BF16), ffn_conv_w=ffn_conv_w,
             ffn_conv_b=ffn_conv_b, w_down=ffn_w_down.astype(BF16))
    dims = (heads_a, dk_a, heads_b, heads_c, dk_c)

    bias_p = _prompt_bias(rel_bias, ATTN_TILE)
    bias_pg, bias_new = _decode_bias(rel_bias, page, DECODE_PAGES_PER_STEP, ss)
    kc = cache_k.reshape(depth, cache_k.shape[1], page * heads_a, LANES)
    vc = cache_v.reshape(depth, cache_v.shape[1], page * heads_a, LANES)

    def prompt_attn(l, qn, knb, vt, lam_init):
        shp = (bp, sp, wa)
        o = _prompt_attention(qn.reshape(shp), knb.reshape(shp), vt, bias_p,
                              diff_lambda[l], diff_subln_g[l], lam_init, dk_a)
        return o.reshape(bp * sp, wa)

    def sample_attn(l, qn, knb, vb, lam_init):
        shp = (bs, ss, wa)
        o = _decode_attention(l, qn.reshape(shp), knb.reshape(shp), vb.reshape(shp), kc, vc, page_table,
                              bias_pg, bias_new, diff_lambda[l], diff_subln_g[l], lam_init, dk_a)
        return o.reshape(bs * ss, wa)

    xp = x_prompt.reshape(bp * sp, d_model)
    xs = x_sample.reshape(bs * ss, d_model)
    zero_m = (jnp.zeros((bp, heads_b, dk_b, dv_b), F32), jnp.zeros((bp, heads_b, dk_b), F32),
              jnp.zeros((bp, heads_b), F32))
    zero_g = jnp.zeros((bp, heads_c, dk_c, dv_c), F32)
    rows_p, rows_s = [], []
    for l in range(depth):
        xp, rp = _layer(xp, l, w, dims, prompt_attn, zero_m, zero_g, None, bp, sp)
        rows_p.append(rp)
        xs, rs = _layer(xs, l, w, dims, sample_attn,
                        (state_mlstm_C[l], state_mlstm_n[l], state_mlstm_m[l]), state_gla_S[l],
                        state_ffn_conv[l], bs, ss)
        rows_s.append(rs)

    def field(rows, i):
        return jnp.stack([r[i] for r in rows], axis=0)

    return (xp.reshape(bp, sp, d_model), xs.reshape(bs, ss, d_model),
            *[field(rows_p, i) for i in range(7)], *[field(rows_s, i) for i in range(7)])
```

```python
import functools
import math

import numpy as np
import jax
import jax.numpy as jnp
from jax import lax
from jax.experimental import pallas as pl
from jax.experimental.pallas import tpu as pltpu

F32 = jnp.float32
BF16 = jnp.bfloat16

LANES = 128
VMEM_LIMIT = 52 * 1024 * 1024
EPS = 1e-6
NEG = -1e30
GLA_TAU = 16.0
MAX_DISTANCE = 128
ATTN_TILE = 256
SCAN_CHUNK = 256
GLA_CHUNK = 128
SAMPLE_CHUNK = 128
DECODE_PAGES_PER_STEP = 8


def _cparams(*sem, flags=None):
    return pltpu.CompilerParams(dimension_semantics=sem, vmem_limit_bytes=VMEM_LIMIT, flags=flags)


def _split_bf16(x):
    hi = x.astype(BF16)
    lo = (x - hi.astype(F32)).astype(BF16)
    return hi, lo


def _dot(a, b):
    return jnp.dot(a, b, preferred_element_type=F32)


def _dot_nt(a, b):
    return lax.dot_general(a, b, (((1,), (1,)), ((), ())), preferred_element_type=F32)


def _dot_tn(a, b):
    return lax.dot_general(a, b, (((0,), (0,)), ((), ())), preferred_element_type=F32)


def _log_sigmoid(x):
    return jnp.minimum(x, 0.0) - jnp.log(1.0 + jnp.exp(-jnp.abs(x)))


def _sigmoid(x):
    return 1.0 / (1.0 + jnp.exp(-x))


def _div_pow2(x, n):
    assert n & (n - 1) == 0
    return lax.shift_right_logical(x, n.bit_length() - 1)


def _mod_pow2(x, n):
    assert n & (n - 1) == 0
    return x & (n - 1)


def _mixer_dtype(block_rows):
    return BF16 if block_rows % 16 == 0 else F32


def _pad_rows(x, rows):
    if x.shape[0] == rows:
        return x
    return jnp.concatenate([x, jnp.zeros((rows - x.shape[0], x.shape[1]), x.dtype)], axis=0)


def _proj_down_kernel(a_ref, w_ref, r_ref, o_ref):
    o_ref[...] = r_ref[...] + _dot(a_ref[...], w_ref[...])


def _proj_down(a, w, layer, res):
    m, k = a.shape
    n = w.shape[2]
    tm = min(m, 512)
    tn = 512
    assert m % tm == 0 and n % tn == 0
    return pl.pallas_call(
        _proj_down_kernel,
        out_shape=jax.ShapeDtypeStruct((m, n), F32),
        grid=(m // tm, n // tn),
        in_specs=[pl.BlockSpec((tm, k), lambda i, j: (i, 0)),
                  pl.BlockSpec((None, k, tn), lambda i, j: (layer, 0, j)),
                  pl.BlockSpec((tm, tn), lambda i, j: (i, j))],
        out_specs=pl.BlockSpec((tm, tn), lambda i, j: (i, j)),
        compiler_params=_cparams("parallel", "parallel"),
        name="proj_down",
    )(a, w, res)


def _rms_rows(x, g):
    ms = jnp.mean(x * x, axis=-1, keepdims=True)
    return x * lax.rsqrt(ms + EPS) * g


def _proj_in_kernel(x_ref, g_ref, w_ref, ws_ref, z_ref, zs_ref, h_sc):
    @pl.when(pl.program_id(1) == 0)
    def _norm():
        h_sc[...] = _rms_rows(x_ref[...], g_ref[...]).astype(BF16)
        zs_ref[...] = _dot(h_sc[...], ws_ref[...])

    z_ref[...] = _dot(h_sc[...], w_ref[...])


def _proj_in(x, g, w_main, w_small, layer):
    m, d = x.shape
    n = w_main.shape[2]
    ns = w_small.shape[2]
    tm = min(m, 1024)
    tn = 512
    assert m % tm == 0 and n % tn == 0
    return pl.pallas_call(
        _proj_in_kernel,
        out_shape=(jax.ShapeDtypeStruct((m, n), F32), jax.ShapeDtypeStruct((m, ns), F32)),
        grid=(m // tm, n // tn),
        in_specs=[pl.BlockSpec((tm, d), lambda i, j: (i, 0)),
                  pl.BlockSpec((1, d), lambda i, j: (0, 0)),
                  pl.BlockSpec((None, d, tn), lambda i, j: (layer, 0, j)),
                  pl.BlockSpec((None, d, ns), lambda i, j: (layer, 0, 0))],
        out_specs=(pl.BlockSpec((tm, tn), lambda i, j: (i, j)),
                   pl.BlockSpec((tm, ns), lambda i, j: (i, 0))),
        scratch_shapes=[pltpu.VMEM((tm, d), BF16)],
        compiler_params=_cparams("parallel", "arbitrary"),
        name="proj_in",
    )(x, g.reshape(1, d), w_main, w_small)


def _proj_out_kernel(oa_ref, ob_ref, oc_ref, w_ref, r_ref, g_ref, x_ref, h_ref):
    ka, kb = oa_ref.shape[1], ob_ref.shape[1]
    acc = _dot(oa_ref[...].astype(BF16), w_ref[0:ka, :])
    acc += _dot(ob_ref[...].astype(BF16), w_ref[ka:ka + kb, :])
    acc += _dot(oc_ref[...].astype(BF16), w_ref[ka + kb:, :])
    x = r_ref[...] + acc
    x_ref[...] = x
    h_ref[...] = _rms_rows(x, g_ref[...]).astype(BF16)


def _proj_out(oa, ob, oc, w, layer, res, g):
    m = oa.shape[0]
    k, n = w.shape[1], w.shape[2]
    tm = min(m, 256)
    assert m % tm == 0 and oa.shape[1] + ob.shape[1] + oc.shape[1] == k
    rows = lambda width: pl.BlockSpec((tm, width), lambda i: (i, 0))
    return pl.pallas_call(
        _proj_out_kernel,
        out_shape=(jax.ShapeDtypeStruct((m, n), F32), jax.ShapeDtypeStruct((m, n), BF16)),
        grid=(m // tm,),
        in_specs=[rows(oa.shape[1]), rows(ob.shape[1]), rows(oc.shape[1]),
                  pl.BlockSpec((None, k, n), lambda i: (layer, 0, 0)),
                  rows(n),
                  pl.BlockSpec((1, n), lambda i: (0, 0))],
        out_specs=(rows(n), rows(n)),
        compiler_params=_cparams("parallel"),
        name="proj_out",
    )(oa, ob, oc, w, res, g.reshape(1, n))


def _qknorm_kernel(aq_ref, ak_ref, av_ref, qg_ref, kg_ref, bd_ref,
                   qn_ref, kn_ref, knb_ref, vb_ref, *, heads, inv_dk, scale, v_transposed):
    bd = bd_ref[...]
    qg = qg_ref[...]
    kg = kg_ref[...]

    def norm(x, g):
        hi, lo = _split_bf16(x * x)
        ss = _dot(hi, bd) + _dot(lo, bd)
        return x * lax.rsqrt(ss * inv_dk + EPS) * g

    for h in range(heads):
        sl = slice(LANES * h, LANES * (h + 1))
        qn_ref[:, sl] = (norm(aq_ref[:, sl], qg) * scale).astype(BF16)
        kn = norm(ak_ref[:, sl], kg)
        kn_ref[:, sl] = kn
        knb_ref[:, sl] = kn.astype(BF16)
        if v_transposed:
            vb_ref[0, sl, :] = av_ref[:, sl].T.astype(BF16)
    if not v_transposed:
        vb_ref[...] = av_ref[...].astype(BF16)


def _qknorm(z, qg, kg, heads, dk, seq, v_transposed):
    m = z.shape[0]
    w = heads * LANES
    tm = min(m, 256)
    assert m % tm == 0
    lane = np.arange(LANES)
    bd = jnp.asarray((lane[:, None] // dk) == (lane[None, :] // dk), BF16)
    reps = LANES // dk
    kern = functools.partial(_qknorm_kernel, heads=heads, inv_dk=1.0 / dk, scale=dk ** -0.5,
                             v_transposed=v_transposed)
    rows = pl.BlockSpec((tm, w), lambda i: (i, 0))
    if v_transposed:
        tps = seq // tm
        v_shape = jax.ShapeDtypeStruct((m // seq, w, seq), BF16)
        v_spec = pl.BlockSpec((1, w, tm), lambda i: (i // tps, 0, i % tps))
    else:
        v_shape = jax.ShapeDtypeStruct((m, w), BF16)
        v_spec = rows
    return pl.pallas_call(
        kern,
        out_shape=(jax.ShapeDtypeStruct((m, w), BF16), jax.ShapeDtypeStruct((m, w), F32),
                   jax.ShapeDtypeStruct((m, w), BF16), v_shape),
        grid=(m // tm,),
        in_specs=[pl.BlockSpec((tm, w), lambda i: (i, 0)),
                  pl.BlockSpec((tm, w), lambda i: (i, 1)),
                  pl.BlockSpec((tm, w), lambda i: (i, 2)),
                  pl.BlockSpec((1, LANES), lambda i: (0, 0)),
                  pl.BlockSpec((1, LANES), lambda i: (0, 0)),
                  pl.BlockSpec((LANES, LANES), lambda i: (0, 0))],
        out_specs=(rows, rows, rows, v_spec),
        compiler_params=_cparams("parallel"),
        name="qknorm",
    )(z, z, z, jnp.tile(qg, reps).reshape(1, LANES), jnp.tile(kg, reps).reshape(1, LANES), bd)


def _bucket(n, n_buckets):
    max_exact = n_buckets // 2
    nf = jnp.maximum(n, max_exact).astype(F32)
    large = max_exact + (jnp.log(nf / max_exact) / math.log(MAX_DISTANCE / max_exact)
                         * (n_buckets - max_exact)).astype(jnp.int32)
    large = jnp.minimum(large, n_buckets - 1)
    return jnp.where(n < max_exact, jnp.maximum(n, 0), large)


def _prompt_bias_kernel(rb_ref, o_ref, *, tile, n_buckets):
    h = pl.program_id(0)
    j = lax.broadcasted_iota(jnp.int32, (tile, tile), 0)
    i = lax.broadcasted_iota(jnp.int32, (tile, tile), 1)
    far = jnp.full((tile, tile), MAX_DISTANCE, jnp.int32)
    for t, n in enumerate((i - j, tile + i - j, far)):
        b = _bucket(n, n_buckets)
        val = jnp.full((tile, tile), rb_ref[0, h], F32)
        for k in range(1, n_buckets):
            val = jnp.where(b == k, rb_ref[k, h], val)
        o_ref[0, t] = jnp.where(n >= 0, val, NEG)


def _prompt_bias(rel_bias, tile):
    n_buckets, heads = rel_bias.shape
    kern = functools.partial(_prompt_bias_kernel, tile=tile, n_buckets=n_buckets)
    return pl.pallas_call(
        kern,
        out_shape=jax.ShapeDtypeStruct((heads, 3, tile, tile), F32),
        grid=(heads,),
        in_specs=[pl.BlockSpec(memory_space=pltpu.SMEM)],
        out_specs=pl.BlockSpec((1, 3, tile, tile), lambda h: (h, 0, 0, 0)),
        compiler_params=_cparams("parallel"),
        name="prompt_bias",
    )(rel_bias)


def _decode_bias_kernel(rb_ref, pg_ref, new_ref, *, page, ppb, dec_seq, heads, n_buckets):
    grp = 2 * dec_seq

    def lookup(n, h):
        b = _bucket(n, n_buckets)
        val = jnp.full(n.shape, rb_ref[0, h], F32)
        for k in range(1, n_buckets):
            val = jnp.where(b == k, rb_ref[k, h], val)
        return jnp.where(n >= 0, val, NEG)

    qi = _mod_pow2(lax.broadcasted_iota(jnp.int32, (grp, page), 0), dec_seq)
    j = lax.broadcasted_iota(jnp.int32, (grp, page), 1)
    far = jnp.full((grp, page), MAX_DISTANCE, jnp.int32)
    for h in range(heads):
        rs = slice(grp * h, grp * (h + 1))
        far_b = lookup(far, h)
        for c in range(ppb):
            pg_ref[0, rs, page * c:page * (c + 1)] = far_b
            pg_ref[1, rs, page * c:page * (c + 1)] = far_b if c < ppb - 1 else lookup(page + qi - j, h)
        new_ref[rs, :] = lookup(jnp.where(j < dec_seq, qi - j, -1), h)


def _decode_bias(rel_bias, page, ppb, dec_seq):
    n_buckets, heads = rel_bias.shape
    rows = 2 * dec_seq * heads
    kern = functools.partial(_decode_bias_kernel, page=page, ppb=ppb, dec_seq=dec_seq, heads=heads,
                             n_buckets=n_buckets)
    return pl.pallas_call(
        kern,
        out_shape=(jax.ShapeDtypeStruct((2, rows, ppb * page), F32),
                   jax.ShapeDtypeStruct((rows, page), F32)),
        in_specs=[pl.BlockSpec(memory_space=pltpu.SMEM)],
        name="decode_bias",
    )(rel_bias)


def _diff_lambda(dl, lam_init):
    s1 = jnp.sum(dl[0:1] * dl[1:2], axis=-1, keepdims=True)
    s2 = jnp.sum(dl[2:3] * dl[3:4], axis=-1, keepdims=True)
    return jnp.exp(s1) - jnp.exp(s2) + lam_init


def _subln(o, g, lam_init):
    ms = jnp.mean(o * o, axis=-1, keepdims=True)
    return o * lax.rsqrt(ms + EPS) * g * (1.0 - lam_init)


def _flash_update(s, m_prev, l_prev):
    m_new = jnp.maximum(m_prev, jnp.max(s, axis=-1, keepdims=True))
    alpha = jnp.exp(m_prev - m_new)
    p = jnp.exp(s - m_new)
    l_new = alpha * l_prev + jnp.sum(p, axis=-1, keepdims=True)
    return p, alpha, m_new, l_new


def _attn_kernel(qt_ref, kt_ref, q_ref, k_ref, vt_ref, bias_ref, dl_ref, g_ref, o_ref, m_sc, l_sc, acc_sc,
                 *, heads, tile, dk, lam_init):
    qi = qt_ref[pl.program_id(1)]
    ki = kt_ref[pl.program_id(1)]
    n_sub = tile // LANES

    @pl.when(ki == 0)
    def _init():
        m_sc[...] = jnp.full(m_sc.shape, -jnp.inf, F32)
        l_sc[...] = jnp.zeros(l_sc.shape, F32)
        acc_sc[...] = jnp.zeros(acc_sc.shape, F32)

    lane = lax.broadcasted_iota(jnp.int32, (LANES, LANES), 1)
    for h in range(heads):
        sl = slice(LANES * h, LANES * (h + 1))
        kh = k_ref[0, :, sl]
        vht = vt_ref[0, sl, :]
        for c in range(2 * n_sub):
            qrows = slice(LANES * (c % n_sub), LANES * (c % n_sub + 1))
            cs = slice(LANES * c, LANES * (c + 1))
            qc = q_ref[0, qrows, sl]
            own_map = (lane < dk) if c < n_sub else (lane >= dk)
            qc = jnp.where(own_map, qc, jnp.zeros_like(qc))
            s = _dot_nt(kh, qc) + bias_ref[h, 0, :, qrows]
            m_prev = m_sc[h, :, cs]
            m_new = jnp.maximum(m_prev, jnp.max(s, axis=0, keepdims=True))
            alpha = jnp.exp(m_prev - m_new)
            p = jnp.exp(s - m_new)
            l_sc[h, :, cs] = alpha * l_sc[h, :, cs] + jnp.sum(p, axis=0, keepdims=True)
            acc_sc[h, :, cs] = alpha * acc_sc[h, :, cs] + _dot(vht, p.astype(BF16))
            m_sc[h, :, cs] = m_new

    @pl.when(ki == qi)
    def _finish():
        lam = _diff_lambda(dl_ref[...], lam_init)
        g_col = g_ref[...]
        for h in range(heads):
            acc = acc_sc[h]
            l = l_sc[h]
            ot = acc[:, :tile] / l[:, :tile] - lam * (acc[:, tile:] / l[:, tile:])
            ms = jnp.mean(ot * ot, axis=0, keepdims=True)
            ot = ot * lax.rsqrt(ms + EPS) * g_col * (1.0 - lam_init)
            o_ref[0, :, LANES * h:LANES * (h + 1)] = ot.T.astype(o_ref.dtype)


def _prompt_attention(qn, knb, vt, bias, dl, g, lam_init, dk):
    bsz, seq, w = qn.shape
    heads = w // LANES
    tile = bias.shape[-1]
    nq = seq // tile
    assert seq % tile == 0 and tile % LANES == 0
    kern = functools.partial(_attn_kernel, heads=heads, tile=tile, dk=dk, lam_init=lam_init)
    pairs = [(qi, ki) for qi in range(nq) for ki in range(qi + 1)]
    q_tab = jnp.asarray([p[0] for p in pairs], jnp.int32)
    k_tab = jnp.asarray([p[1] for p in pairs], jnp.int32)

    def bias_idx(b, t, qt, kt):
        return (0, jnp.where(kt[t] == qt[t], 0, jnp.where(kt[t] == qt[t] - 1, 1, 2)), 0, 0)

    grid_spec = pltpu.PrefetchScalarGridSpec(
        num_scalar_prefetch=2,
        grid=(bsz, len(pairs)),
        in_specs=[pl.BlockSpec((1, tile, w), lambda b, t, qt, kt: (b, qt[t], 0)),
                  pl.BlockSpec((1, tile, w), lambda b, t, qt, kt: (b, kt[t], 0)),
                  pl.BlockSpec((1, w, tile), lambda b, t, qt, kt: (b, 0, kt[t])),
                  pl.BlockSpec((heads, 1, tile, tile), bias_idx),
                  pl.BlockSpec(dl.shape, lambda b, t, qt, kt: (0, 0)),
                  pl.BlockSpec((LANES, 1), lambda b, t, qt, kt: (0, 0))],
        out_specs=pl.BlockSpec((1, tile, w), lambda b, t, qt, kt: (b, qt[t], 0)),
        scratch_shapes=[pltpu.VMEM((heads, 1, 2 * tile), F32),
                        pltpu.VMEM((heads, 1, 2 * tile), F32),
                        pltpu.VMEM((heads, LANES, 2 * tile), F32)],
    )
    return pl.pallas_call(
        kern,
        out_shape=jax.ShapeDtypeStruct((bsz, seq, w), BF16),
        grid_spec=grid_spec,
        compiler_params=_cparams("parallel", "arbitrary"),
        name="prompt_attention",
    )(q_tab, k_tab, qn, knb, vt, bias, dl, g.reshape(LANES, 1))


def _decode_attn_kernel(pt_ref, q_ref, kn_ref, vn_ref, *refs, heads, dec_seq, page, ppb, dk, lam_init):
    del pt_ref
    kc_refs, vc_refs = refs[:ppb], refs[ppb:2 * ppb]
    bias_ref, bnew_ref, dl_ref, g_ref, o_ref, wq_sc, wqb_sc, m_sc, l_sc, acc_sc = refs[2 * ppb:]
    p_idx = pl.program_id(1)
    rows = 2 * dec_seq * heads
    grp = 2 * dec_seq
    width = heads * LANES

    @pl.when(p_idx == 0)
    def _init():
        q = q_ref[0].astype(F32)
        lane = lax.broadcasted_iota(jnp.int32, (dec_seq, LANES), 1)
        pieces = []
        for h in range(heads):
            qh = q[:, LANES * h:LANES * (h + 1)]
            pieces += [jnp.where(lane < dk, qh, 0.0), jnp.where(lane >= dk, qh, 0.0)]
        wq = jnp.concatenate(pieces, axis=0)
        wq_sc[...] = wq.astype(BF16)
        c = lax.broadcasted_iota(jnp.int32, (rows, width), 1)
        r = lax.broadcasted_iota(jnp.int32, (rows, width), 0)
        own = _div_pow2(r, grp) == _div_pow2(c, LANES)
        wqb_sc[...] = jnp.where(own, jnp.concatenate([wq] * heads, axis=1), 0.0).astype(BF16)
        m_sc[...] = jnp.full(m_sc.shape, -jnp.inf, F32)
        l_sc[...] = jnp.zeros(l_sc.shape, F32)
        acc_sc[...] = jnp.zeros(acc_sc.shape, F32)

    def head_rows(ref, h):
        return ref[pl.ds(h, page, stride=heads), :].astype(BF16)

    s = jnp.concatenate(
        [jnp.concatenate([_dot_nt(wq_sc[grp * h:grp * (h + 1), :], head_rows(kc, h)) for h in range(heads)],
                         axis=0) for kc in kc_refs], axis=1) + bias_ref[0]
    p, alpha, m_new, l_new = _flash_update(s, m_sc[...], l_sc[...])
    m_sc[...] = m_new
    l_sc[...] = l_new
    pb = p.astype(BF16)
    for h in range(heads):
        rs = slice(grp * h, grp * (h + 1))
        acc = alpha[rs] * acc_sc[rs, :]
        for j, vc in enumerate(vc_refs):
            acc += _dot(pb[rs, page * j:page * (j + 1)], head_rows(vc, h))
        acc_sc[rs, :] = acc

    @pl.when(p_idx == pl.num_programs(1) - 1)
    def _finish():
        kn = _pad_rows(kn_ref[0].astype(F32), page).astype(BF16)
        vn = _pad_rows(vn_ref[0].astype(F32), page)
        s = _dot_nt(wqb_sc[...], kn) + bnew_ref[...]
        p, alpha, _, l_fin = _flash_update(s, m_sc[...], l_sc[...])
        pb = p.astype(BF16)
        lam = _diff_lambda(dl_ref[...], lam_init)
        g = g_ref[...]
        for h in range(heads):
            rs = slice(grp * h, grp * (h + 1))
            vh = vn[:, LANES * h:LANES * (h + 1)].astype(BF16)
            acc = alpha[rs] * acc_sc[rs, :] + _dot(pb[rs], vh)
            l = l_fin[rs]
            o = acc[:dec_seq] / l[:dec_seq] - lam * (acc[dec_seq:] / l[dec_seq:])
            o_ref[0, :, LANES * h:LANES * (h + 1)] = _subln(o, g, lam_init)


def _decode_attention(layer, qn, knb, vb, cache_k, cache_v, page_table, bias_pg, bias_new, dl, g, lam_init, dk):
    bsz, dec_seq, w = qn.shape
    heads = w // LANES
    page = cache_k.shape[2] // heads
    ppb = bias_pg.shape[-1] // page
    n_steps = page_table.shape[1] // ppb
    assert page_table.shape[1] % ppb == 0
    rows = 2 * dec_seq * heads
    kern = functools.partial(_decode_attn_kernel, heads=heads, dec_seq=dec_seq, page=page, ppb=ppb, dk=dk,
                             lam_init=lam_init)
    new_spec = pl.BlockSpec((1, dec_seq, w), lambda b, p, pt: (b, 0, 0))
    cache_specs = [pl.BlockSpec((None, None, page * heads, LANES),
                                lambda b, p, pt, j=j: (layer, pt[b, p * ppb + j], 0, 0)) for j in range(ppb)]
    grid_spec = pltpu.PrefetchScalarGridSpec(
        num_scalar_prefetch=1,
        grid=(bsz, n_steps),
        in_specs=[new_spec, new_spec, new_spec, *cache_specs, *cache_specs,
                  pl.BlockSpec((1, rows, ppb * page),
                               lambda b, p, pt: (jnp.where(p == n_steps - 1, 1, 0), 0, 0)),
                  pl.BlockSpec((rows, page), lambda b, p, pt: (0, 0)),
                  pl.BlockSpec(dl.shape, lambda b, p, pt: (0, 0)),
                  pl.BlockSpec((1, LANES), lambda b, p, pt: (0, 0))],
        out_specs=pl.BlockSpec((1, dec_seq, w), lambda b, p, pt: (b, 0, 0)),
        scratch_shapes=[pltpu.VMEM((rows, LANES), BF16),
                        pltpu.VMEM((rows, w), BF16),
                        pltpu.VMEM((rows, 1), F32),
                        pltpu.VMEM((rows, 1), F32),
                        pltpu.VMEM((rows, LANES), F32)],
    )
    return pl.pallas_call(
        kern,
        out_shape=jax.ShapeDtypeStruct((bsz, dec_seq, w), F32),
        grid_spec=grid_spec,
        compiler_params=_cparams("parallel", "arbitrary"),
        name="decode_attention",
    )(page_table, qn, knb, vb, *([cache_k] * ppb), *([cache_v] * ppb), bias_pg, bias_new, dl,
      g.reshape(1, LANES))


GATE_I = 16
GATE_F = 20


def _mlstm_kernel(q_ref, k_ref, v_ref, og_ref, zs_ref, gt_ref, gbl_ref, gbc_ref, g_ref,
                  c0_ref, n0_ref, m0_ref,
                  o_ref, cf_ref, nf_ref, mf_ref, c_sc, n_sc, m_sc,
                  *, heads, chunk, valid, scale):
    c_idx = pl.program_id(1)

    @pl.when(c_idx == 0)
    def _init():
        c_sc[...] = c0_ref[0]
        n_sc[...] = n0_ref[0]
        m_sc[...] = m0_ref[0]

    rows_in = q_ref.shape[0]
    row = lax.broadcasted_iota(jnp.int32, (chunk, 1), 0)
    t_i = lax.broadcasted_iota(jnp.int32, (chunk, chunk), 0)
    s_i = lax.broadcasted_iota(jnp.int32, (chunk, chunk), 1)
    causal = t_i >= s_i
    tril = causal.astype(BF16)
    triu = (t_i <= s_i).astype(BF16)

    gcol = _pad_rows(zs_ref[...], chunk) + gbl_ref[...]
    lf_mat = _log_sigmoid(gcol)
    if valid < chunk:
        lf_mat = jnp.where(row < valid, lf_mat, 0.0)
    hi, lo = _split_bf16(lf_mat)
    b_mat = _dot(tril, hi) + _dot(tril, lo)

    grow = gt_ref[0] + gbc_ref[...]
    col = lax.broadcasted_iota(jnp.int32, grow.shape, 1)
    grow_id = lax.broadcasted_iota(jnp.int32, grow.shape, 0)
    lf_rows = jnp.where(grow_id >= heads, _log_sigmoid(grow), 0.0)
    li_rows = grow
    if valid < chunk:
        lf_rows = jnp.where(col < valid, lf_rows, 0.0)
        li_rows = jnp.where(col < valid, li_rows, NEG)
    hi, lo = _split_bf16(lf_rows)
    b_rows = _dot(hi, triu) + _dot(lo, triu)

    g = g_ref[...]
    for h in range(heads):
        sl = slice(LANES * h, LANES * (h + 1))
        b_col = b_mat[:, GATE_F + h:GATE_F + h + 1]
        li_col = gcol[:, GATE_I + h:GATE_I + h + 1]
        if valid < chunk:
            li_col = jnp.where(row < valid, li_col, NEG)
        b_row = b_rows[heads + h:heads + h + 1, :]
        li_row = li_rows[h:h + 1, :]
        d = jnp.where(causal, b_col - b_row + li_row, NEG)
        m_prev = m_sc[h:h + 1, 0:1]
        inter = b_col + m_prev
        mt = jnp.maximum(inter, jnp.max(d, axis=-1, keepdims=True))
        wi = jnp.exp(inter - mt)
        q = _pad_rows(q_ref[:, sl], chunk)
        ks = _pad_rows(k_ref[:, sl], chunk) * scale
        vb = _pad_rows(v_ref[:, sl], chunk).astype(BF16)
        qb = q.astype(BF16)
        p = _dot_nt(qb, ks.astype(BF16)) * jnp.exp(d - mt)
        c_prev = c_sc[h]
        n_prev = n_sc[h:h + 1, :]
        num = wi * _dot(qb, c_prev.astype(BF16)) + _dot(p.astype(BF16), vb)
        den = wi * jnp.sum(q * n_prev, axis=-1, keepdims=True) + jnp.sum(p, axis=-1, keepdims=True)
        hh = num / jnp.maximum(jnp.abs(den), jnp.exp(-mt))
        og = _sigmoid(og_ref[:, sl])
        y = og * hh[:rows_in]
        ms = jnp.mean(y * y, axis=-1, keepdims=True)
        o_ref[:, sl] = (y * lax.rsqrt(ms + EPS) * g).astype(o_ref.dtype)

        m_new = mt[chunk - 1:chunk, :]
        b_last = b_col[chunk - 1:chunk, :]
        a = jnp.exp(b_last + m_prev - m_new)
        ws = jnp.exp(b_last - b_col + li_col - m_new)
        kw = ks * ws
        c_sc[h] = a * c_prev + _dot_tn(kw.astype(BF16), vb)
        n_sc[h:h + 1, :] = a * n_prev + jnp.sum(kw, axis=0, keepdims=True)
        m_sc[h:h + 1, :] = jnp.broadcast_to(m_new, (1, LANES))

    @pl.when(c_idx == pl.num_programs(1) - 1)
    def _finish():
        cf_ref[0] = c_sc[...]
        nf_ref[0] = n_sc[...]
        mf_ref[0] = m_sc[...]


def _mlstm(z, zs, gt, gate_b, g, c0, n0, m0, seq, chunk, col0):
    bsz, heads = c0.shape[0], c0.shape[1]
    w = heads * LANES
    rows_in = min(seq, chunk)
    nc = seq // rows_in
    valid = rows_in
    gt_w = gt.shape[-1] // nc
    kern = functools.partial(_mlstm_kernel, heads=heads, chunk=chunk, valid=valid, scale=LANES ** -0.5)
    gbl = jnp.zeros((1, LANES), F32)
    gbl = gbl.at[0, GATE_I:GATE_I + heads].set(gate_b[0]).at[0, GATE_F:GATE_F + heads].set(gate_b[1])
    gbc = gate_b.reshape(2 * heads, 1)
    m0b = jnp.broadcast_to(m0[:, :, None], (bsz, heads, LANES))

    def zspec(blk):
        return pl.BlockSpec((rows_in, w), lambda b, c: (b * nc + c, col0 + blk))

    state = lambda shape: pl.BlockSpec((1,) + shape, lambda b, c: (b,) + (0,) * len(shape))
    return pl.pallas_call(
        kern,
        out_shape=(jax.ShapeDtypeStruct((bsz * seq, w), _mixer_dtype(rows_in)),
                   jax.ShapeDtypeStruct((bsz, heads, LANES, LANES), F32),
                   jax.ShapeDtypeStruct((bsz, heads, LANES), F32),
                   jax.ShapeDtypeStruct((bsz, heads, LANES), F32)),
        grid=(bsz, nc),
        in_specs=[zspec(0), zspec(1), zspec(2), zspec(3),
                  pl.BlockSpec((rows_in, LANES), lambda b, c: (b * nc + c, 0)),
                  pl.BlockSpec((1, 2 * heads, gt_w), lambda b, c: (b, 0, c)),
                  pl.BlockSpec((1, LANES), lambda b, c: (0, 0)),
                  pl.BlockSpec((2 * heads, 1), lambda b, c: (0, 0)),
                  pl.BlockSpec((1, LANES), lambda b, c: (0, 0)),
                  state((heads, LANES, LANES)), state((heads, LANES)), state((heads, LANES))],
        out_specs=(pl.BlockSpec((rows_in, w), lambda b, c: (b * nc + c, 0)),
                   state((heads, LANES, LANES)), state((heads, LANES)), state((heads, LANES))),
        scratch_shapes=[pltpu.VMEM((heads, LANES, LANES), F32),
                        pltpu.VMEM((heads, LANES), F32),
                        pltpu.VMEM((heads, LANES), F32)],
        compiler_params=_cparams("parallel", "arbitrary"),
        name="mlstm",
    )(z, z, z, z, zs, gt, gbl, gbc, g.reshape(1, LANES), c0, n0, m0b)


def _gla_levels(chunk):
    n, out = chunk, []
    while n >= 2:
        out.append(n)
        n //= 2
    return out


def _gla_weights(chunk):
    t = np.arange(chunk)[:, None]
    s = np.arange(chunk)[None, :]
    blocks = [(s <= t).astype(np.float32), (s > t).astype(np.float32)]
    for n in _gla_levels(chunk):
        mid = (t // n) * n + n // 2 - 1
        blocks.append(((s > mid) & (s <= t)).astype(np.float32) - ((s > t) & (s <= mid)).astype(np.float32))
    return jnp.asarray(np.concatenate(blocks, axis=0), BF16)


def _gla_kernel(q_ref, k_ref, v_ref, gg_ref, zs_ref, wa_ref, ba_ref, ws_ref, g_ref, s0_ref,
                o_ref, sf_ref, s_sc, *, heads, chunk, valid, dk, scale):
    c_idx = pl.program_id(1)

    @pl.when(c_idx == 0)
    def _init():
        s_sc[...] = s0_ref[0]

    rows_in = q_ref.shape[0]
    row = lax.broadcasted_iota(jnp.int32, (chunk, 1), 0)
    t_i = lax.broadcasted_iota(jnp.int32, (chunk, chunk), 0)
    s_i = lax.broadcasted_iota(jnp.int32, (chunk, chunk), 1)

    zs = _pad_rows(zs_ref[...], chunk)
    la = _log_sigmoid(_dot(zs.astype(BF16), wa_ref[...]) + ba_ref[...]) * (1.0 / GLA_TAU)
    if valid < chunk:
        la = jnp.where(row < valid, la, 0.0)
    hi, lo = _split_bf16(la)
    wst = ws_ref[...]
    e_all = _dot(wst, hi) + _dot(wst, lo)
    ones = jnp.ones((chunk, LANES), BF16)
    levels = _gla_levels(chunk)
    g = g_ref[...]

    for h in range(heads):
        ksl = slice(dk * h, dk * (h + 1))
        vsl = slice(LANES * h, LANES * (h + 1))
        q = _pad_rows(q_ref[:, ksl], chunk) * scale
        k = _pad_rows(k_ref[:, ksl], chunk)
        vb = _pad_rows(v_ref[:, vsl], chunk).astype(BF16)
        bc = e_all[0:chunk, ksl]
        rem = e_all[chunk:2 * chunk, ksl]
        a_mat = jnp.zeros((chunk, chunk), F32)
        for li, n in enumerate(levels):
            e = e_all[(2 + li) * chunk:(3 + li) * chunk, ksl]
            second = (row & (n - 1)) >= n // 2
            qt = jnp.where(second, q * jnp.exp(jnp.minimum(e, 0.0)), 0.0)
            kt = jnp.where(second, 0.0, k * jnp.exp(jnp.minimum(-e, 0.0)))
            shift = n.bit_length() - 1
            same = lax.shift_right_logical(t_i, shift) == lax.shift_right_logical(s_i, shift)
            a_mat = a_mat + jnp.where(same, _dot_nt(qt.astype(BF16), kt.astype(BF16)), 0.0)
        s_prev = s_sc[h]
        qd = (q * jnp.exp(bc)).astype(BF16)
        o = _dot(qd, s_prev.astype(BF16)) + _dot(a_mat.astype(BF16), vb)
        o = o + jnp.sum(q * k, axis=-1, keepdims=True) * vb.astype(F32)
        o = o[:rows_in]
        ms = jnp.mean(o * o, axis=-1, keepdims=True)
        gate = gg_ref[:, vsl]
        o_ref[:, vsl] = (o * lax.rsqrt(ms + EPS) * g * (gate * _sigmoid(gate))).astype(o_ref.dtype)

        la_h = la[:, ksl]
        hi_h, lo_h = _split_bf16(la_h)
        tot = _dot_tn(hi_h, ones) + _dot_tn(lo_h, ones)
        kd = (k * jnp.exp(rem)).astype(BF16)
        s_sc[h] = jnp.exp(tot) * s_prev + _dot_tn(kd, vb)

    @pl.when(c_idx == pl.num_programs(1) - 1)
    def _finish():
        sf_ref[0] = s_sc[...]


def _gla(z, zs, w_alpha, b_alpha, g, s0, seq, chunk, qcol, kcol, vcol, gcol):
    bsz, heads, dk, dv = s0.shape
    rows_in = min(seq, chunk)
    nc = seq // rows_in
    kw = heads * dk
    vw = heads * dv
    wa = jnp.zeros((LANES, kw), F32).at[:w_alpha.shape[0]].set(w_alpha).astype(BF16)
    wst = _gla_weights(chunk)
    kern = functools.partial(_gla_kernel, heads=heads, chunk=chunk, valid=rows_in, dk=dk, scale=dk ** -0.5)
    const = lambda shape: pl.BlockSpec(shape, lambda b, c: (0,) * len(shape))
    return pl.pallas_call(
        kern,
        out_shape=(jax.ShapeDtypeStruct((bsz * seq, vw), _mixer_dtype(rows_in)),
                   jax.ShapeDtypeStruct((bsz, heads, dk, dv), F32)),
        grid=(bsz, nc),
        in_specs=[pl.BlockSpec((rows_in, kw), lambda b, c: (b * nc + c, qcol)),
                  pl.BlockSpec((rows_in, kw), lambda b, c: (b * nc + c, kcol)),
                  pl.BlockSpec((rows_in, vw), lambda b, c: (b * nc + c, vcol)),
                  pl.BlockSpec((rows_in, vw), lambda b, c: (b * nc + c, gcol)),
                  pl.BlockSpec((rows_in, LANES), lambda b, c: (b * nc + c, 0)),
                  const((LANES, kw)), const((1, kw)), const(wst.shape), const((1, LANES)),
                  pl.BlockSpec((1, heads, dk, dv), lambda b, c: (b, 0, 0, 0))],
        out_specs=(pl.BlockSpec((rows_in, vw), lambda b, c: (b * nc + c, 0)),
                   pl.BlockSpec((1, heads, dk, dv), lambda b, c: (b, 0, 0, 0))),
        scratch_shapes=[pltpu.VMEM((heads, dk, dv), F32)],
        compiler_params=_cparams("parallel", "arbitrary"),
        name="gla",
    )(z, z, z, z, zs, wa, b_alpha.reshape(1, kw), wst, g.reshape(1, LANES), s0)


def _conv_gate(ug, uv, cwg, cwv, cbg, cbv, prev):
    def conv(u, cw, cb, which):
        u1, u2 = prev(u, which)
        return cb + cw[0:1] * u2 + cw[1:2] * u1 + cw[2:3] * u
    cg = conv(ug, cwg, cbg, 0)
    cv = conv(uv, cwv, cbv, 1)
    return cg * _sigmoid(cg) * cv


FFN_SUB = 256


def _ffn_up_prompt_kernel(h_ref, wg_ref, wv_ref, cwg_ref, cwv_ref, cbg_ref, cbv_ref,
                          act_ref, tg_ref, tv_ref, u_sc, w_sc, *, tiles_per_seq):
    i = pl.program_id(1)
    first = (i % tiles_per_seq) == 0
    hb = h_ref[...]
    tm = hb.shape[0]

    @pl.when(i == 0)
    def _cast_weights():
        w_sc[0] = wg_ref[...].astype(BF16)
        w_sc[1] = wv_ref[...].astype(BF16)

    @pl.when(first)
    def _reset():
        u_sc[:, 0:8, :] = jnp.zeros((2, 8, u_sc.shape[2]), F32)

    def conv_half(which, u, cw_ref, cb_ref, t_ref, cs):
        cw = cw_ref[:, cs]
        u_sc[which, 8:8 + tm, cs] = u
        u1 = u_sc[which, 7:7 + tm, cs]
        u2 = u_sc[which, 6:6 + tm, cs]
        conv = cb_ref[:, cs] + cw[0:1] * u2 + cw[1:2] * u1 + cw[2:3] * u
        u_sc[which, 0:8, cs] = u[tm - 8:tm]
        t_ref[0, :, cs] = u[tm - 8:tm]
        return conv

    subs = [slice(c0, c0 + FFN_SUB) for c0 in range(0, wg_ref.shape[1], FFN_SUB)]
    for cs in subs:
        cg = conv_half(0, _dot(hb, w_sc[0, :, cs]), cwg_ref, cbg_ref, tg_ref, cs)
        cv = conv_half(1, _dot(hb, w_sc[1, :, cs]), cwv_ref, cbv_ref, tv_ref, cs)
        act_ref[:, cs] = (cg * _sigmoid(cg) * cv).astype(BF16)


def _ffn_up_prompt(h2, w_up, layer, conv_w, conv_b, bsz, seq):
    m, d = h2.shape
    dff = w_up.shape[2] // 2
    tm, tn = 512, 512
    assert seq % tm == 0 and dff % tn == 0 and tn % FFN_SUB == 0
    nj = dff // tn
    tps = seq // tm
    kern = functools.partial(_ffn_up_prompt_kernel, tiles_per_seq=tps)
    wspec = lambda off: pl.BlockSpec((None, d, tn), lambda j, i: (layer, 0, j + off))
    cspec = lambda r, off: pl.BlockSpec((r, tn), lambda j, i: (0, j + off))
    tail = pl.BlockSpec((1, 8, tn), lambda j, i: (i // tps, 0, j))
    cb = conv_b.reshape(1, -1)
    return pl.pallas_call(
        kern,
        out_shape=(jax.ShapeDtypeStruct((m, dff), BF16),
                   jax.ShapeDtypeStruct((bsz, 8, dff), F32),
                   jax.ShapeDtypeStruct((bsz, 8, dff), F32)),
        grid=(nj, m // tm),
        in_specs=[pl.BlockSpec((tm, d), lambda j, i: (i, 0)),
                  wspec(0), wspec(nj), cspec(3, 0), cspec(3, nj), cspec(1, 0), cspec(1, nj)],
        out_specs=(pl.BlockSpec((tm, tn), lambda j, i: (i, j)), tail, tail),
        scratch_shapes=[pltpu.VMEM((2, tm + 8, tn), F32), pltpu.VMEM((2, d, tn), BF16)],
        compiler_params=_cparams("parallel", "arbitrary"),
        name="ffn_up_prompt",
    )(h2, w_up, w_up, conv_w, conv_w, cb, cb)


def _ffn_up_sample_kernel(h_ref, wg_ref, wv_ref, cwg_ref, cwv_ref, cbg_ref, cbv_ref,
                          p1g_ref, p1v_ref, p2g_ref, p2v_ref, act_ref, ug_ref, uv_ref, *, seq):
    hb = h_ref[...]
    ug = _dot(hb, wg_ref[...].astype(BF16))
    uv = _dot(hb, wv_ref[...].astype(BF16))
    tm = ug.shape[0]
    pos = _mod_pow2(lax.broadcasted_iota(jnp.int32, (tm, 1), 0), seq)
    p1 = (p1g_ref, p1v_ref)
    p2 = (p2g_ref, p2v_ref)

    def prev(u, which):
        u1 = jnp.where(pos >= 1, pltpu.roll(u, 1, 0), p1[which][...])
        u2 = jnp.where(pos >= 2, pltpu.roll(u, 2, 0), p2[which][...])
        return u1, u2

    act_ref[...] = _conv_gate(ug, uv, cwg_ref[...], cwv_ref[...], cbg_ref[...], cbv_ref[...], prev).astype(BF16)
    ug_ref[...] = ug
    uv_ref[...] = uv


def _ffn_up_sample(h2, w_up, layer, conv_w, conv_b, conv_state, seq):
    m, d = h2.shape
    dff = w_up.shape[2] // 2
    tn = 512
    assert dff % tn == 0
    nj = dff // tn
    bsz = m // seq
    p1 = jnp.zeros((bsz, seq, 2 * dff), F32).at[:, 0].set(conv_state[:, 1]).reshape(m, 2 * dff)
    p2 = jnp.zeros((bsz, seq, 2 * dff), F32).at[:, 0].set(conv_state[:, 0]).at[:, 1].set(conv_state[:, 1])
    p2 = p2.reshape(m, 2 * dff)
    kern = functools.partial(_ffn_up_sample_kernel, seq=seq)
    wspec = lambda off: pl.BlockSpec((None, d, tn), lambda j: (layer, 0, j + off))
    cspec = lambda r, off: pl.BlockSpec((r, tn), lambda j: (0, j + off))
    cb = conv_b.reshape(1, -1)
    ospec = pl.BlockSpec((m, tn), lambda j: (0, j))
    return pl.pallas_call(
        kern,
        out_shape=(jax.ShapeDtypeStruct((m, dff), BF16),
                   jax.ShapeDtypeStruct((m, dff), F32),
                   jax.ShapeDtypeStruct((m, dff), F32)),
        grid=(nj,),
        in_specs=[pl.BlockSpec((m, d), lambda j: (0, 0)),
                  wspec(0), wspec(nj), cspec(3, 0), cspec(3, nj), cspec(1, 0), cspec(1, nj),
                  cspec(m, 0), cspec(m, nj), cspec(m, 0), cspec(m, nj)],
        out_specs=(ospec, ospec, ospec),
        compiler_params=_cparams("parallel"),
        name="ffn_up_sample",
    )(h2, w_up, w_up, conv_w, conv_w, cb, cb, p1, p1, p2, p2)


def _layer(x, l, w, dims, attn_fn, mstate, gstate, conv_state, bsz, seq):
    (heads_a, dk_a, heads_b, heads_c, dk_c) = dims
    wa = heads_a * LANES
    wb = heads_b * LANES
    prompt = conv_state is None
    z, zs = _proj_in(x, w["norm_mix_g"][l], w["w_main"], w["w_small"], l)
    qn, kn, knb, vb = _qknorm(z, w["q_norm_g"][l], w["k_norm_g"][l], heads_a, dk_a, seq, v_transposed=prompt)
    lam_init = 0.8 - 0.6 * math.exp(-0.3 * l)
    oa = attn_fn(l, qn, knb, vb, lam_init)

    chunk_b = SCAN_CHUNK if seq % SCAN_CHUNK == 0 else SAMPLE_CHUNK
    gates = zs[:, GATE_I:GATE_I + 2 * heads_b].reshape(bsz, seq, 2 * heads_b)
    gt = jnp.swapaxes(gates, 1, 2)
    if seq < chunk_b:
        gt = jnp.pad(gt, ((0, 0), (0, 0), (0, chunk_b - seq)))
    ob, c_f, n_f, m_f = _mlstm(z, zs, gt, w["mlstm_gate_b"][l], w["mlstm_norm_g"][l],
                               mstate[0], mstate[1], mstate[2], seq, chunk_b, (3 * wa) // wb)

    chunk_c = GLA_CHUNK if seq % GLA_CHUNK == 0 else SAMPLE_CHUNK
    kw = heads_c * dk_c
    vw = heads_c * LANES
    c0 = 3 * wa + 4 * wb
    oc, s_f = _gla(z, zs, w["gla_w_alpha"][l], w["gla_b_alpha"][l], w["gla_norm_g"][l], gstate,
                   seq, chunk_c, c0 // kw, c0 // kw + 1, (c0 + 2 * kw) // vw, (c0 + 2 * kw) // vw + 1)

    x1, h2 = _proj_out(oa, ob, oc, w["w_out"], l, x, w["norm_ffn_g"][l])
    if prompt:
        act, tg, tv = _ffn_up_prompt(h2, w["w_up"], l, w["ffn_conv_w"][l], w["ffn_conv_b"][l], bsz, seq)
        conv_rows = jnp.concatenate([tg[:, 6:8], tv[:, 6:8]], axis=-1)
    else:
        act, ug, uv = _ffn_up_sample(h2, w["w_up"], l, w["ffn_conv_w"][l], w["ffn_conv_b"][l], conv_state, seq)
        u = jnp.concatenate([ug, uv], axis=-1).reshape(bsz, seq, -1)
        conv_rows = u[:, seq - 2:]
    x2 = _proj_down(act, w["w_down"], l, x1)

    k_rows = kn.reshape(bsz, seq, heads_a, LANES)
    v_rows = z[:, 2 * wa:3 * wa].reshape(bsz, seq, heads_a, LANES)
    return x2, (k_rows, v_rows, c_f, n_f, m_f[:, :, 0], s_f, conv_rows)


def kernel(x_prompt, x_sample, cache_k, cache_v, page_table, state_mlstm_C, state_mlstm_n, state_mlstm_m, state_gla_S, state_ffn_conv, norm_mix_g, w_in, q_norm_g, k_norm_g, diff_lambda, diff_subln_g, rel_bias, mlstm_gate_b, mlstm_norm_g, gla_w_alpha, gla_b_alpha, gla_norm_g, w_out, norm_ffn_g, ffn_w_up, ffn_conv_w, ffn_conv_b, ffn_w_down):
    depth = w_in.shape[0]
    bp, sp, d_model = x_prompt.shape
    bs, ss, _ = x_sample.shape
    heads_a, dv_a = cache_v.shape[3], cache_v.shape[4]
    dk_a = cache_k.shape[4] // 2
    heads_b, dk_b, dv_b = state_mlstm_C.shape[2:]
    heads_c, dk_c, dv_c = state_gla_S.shape[2:]
    rank = gla_w_alpha.shape[1]
    page = cache_k.shape[2]
    assert dv_a == LANES and 2 * dk_a == LANES and dk_b == LANES and dv_b == LANES and dv_c == LANES
    assert page >= MAX_DISTANCE and ATTN_TILE >= MAX_DISTANCE and rank <= GATE_I
    wa, wb = heads_a * LANES, heads_b * LANES
    n_main = 3 * wa + 4 * wb + 2 * heads_c * dk_c + 2 * heads_c * dv_c
    gate0 = 3 * wa + 4 * wb
    c0 = gate0 + 2 * heads_b
    assert w_in.shape[2] == n_main + 2 * heads_b + rank

    w_main = jnp.concatenate([w_in[:, :, :gate0], w_in[:, :, c0:c0 + n_main - gate0]], axis=-1).astype(BF16)
    w_small = jnp.zeros((depth, d_model, LANES), F32)
    w_small = w_small.at[:, :, :rank].set(w_in[:, :, n_main + 2 * heads_b:])
    w_small = w_small.at[:, :, GATE_I:GATE_I + 2 * heads_b].set(w_in[:, :, gate0:c0]).astype(BF16)
    w = dict(norm_mix_g=norm_mix_g, w_main=w_main, w_small=w_small, q_norm_g=q_norm_g, k_norm_g=k_norm_g,
             mlstm_gate_b=mlstm_gate_b, mlstm_norm_g=mlstm_norm_g, gla_w_alpha=gla_w_alpha,
             gla_b_alpha=gla_b_alpha, gla_norm_g=gla_norm_g, w_out=w_out.astype(BF16),
             norm_ffn_g=norm_ffn_g, w_up=ffn_w_up, ffn_conv_w=ffn_conv_w,
             ffn_conv_b=ffn_conv_b, w_down=ffn_w_down.astype(BF16))
    dims = (heads_a, dk_a, heads_b, heads_c, dk_c)

    bias_p = _prompt_bias(rel_bias, ATTN_TILE)
    bias_pg, bias_new = _decode_bias(rel_bias, page, DECODE_PAGES_PER_STEP, ss)
    kc = cache_k.reshape(depth, cache_k.shape[1], page * heads_a, LANES)
    vc = cache_v.reshape(depth, cache_v.shape[1], page * heads_a, LANES)

    def prompt_attn(l, qn, knb, vt, lam_init):
        shp = (bp, sp, wa)
        o = _prompt_attention(qn.reshape(shp), knb.reshape(shp), vt, bias_p,
                              diff_lambda[l], diff_subln_g[l], lam_init, dk_a)
        return o.reshape(bp * sp, wa)

    def sample_attn(l, qn, knb, vb, lam_init):
        shp = (bs, ss, wa)
        o = _decode_attention(l, qn.reshape(shp), knb.reshape(shp), vb.reshape(shp), kc, vc, page_table,
                              bias_pg, bias_new, diff_lambda[l], diff_subln_g[l], lam_init, dk_a)
        return o.reshape(bs * ss, wa)

    xp = x_prompt.reshape(bp * sp, d_model)
    xs = x_sample.reshape(bs * ss, d_model)
    zero_m = (jnp.zeros((bp, heads_b, dk_b, dv_b), F32), jnp.zeros((bp, heads_b, dk_b), F32),
              jnp.zeros((bp, heads_b), F32))
    zero_g = jnp.zeros((bp, heads_c, dk_c, dv_c), F32)
    rows_p, rows_s = [], []
    for l in range(depth):
        xp, rp = _layer(xp, l, w, dims, prompt_attn, zero_m, zero_g, None, bp, sp)
        rows_p.append(rp)
        xs, rs = _layer(xs, l, w, dims, sample_attn,
                        (state_mlstm_C[l], state_mlstm_n[l], state_mlstm_m[l]), state_gla_S[l],
                        state_ffn_conv[l], bs, ss)
        rows_s.append(rs)

    def field(rows, i):
        return jnp.stack([r[i] for r in rows], axis=0)

    return (xp.reshape(bp, sp, d_model), xs.reshape(bs, ss, d_model),
            *[field(rows_p, i) for i in range(7)], *[field(rows_s, i) for i in range(7)])
```

```python
import functools
import math

import numpy as np
import jax
import jax.numpy as jnp
from jax import lax
from jax.experimental import pallas as pl
from jax.experimental.pallas import tpu as pltpu

F32 = jnp.float32
BF16 = jnp.bfloat16

LANES = 128
VMEM_LIMIT = 52 * 1024 * 1024
EPS = 1e-6
NEG = -1e30
GLA_TAU = 16.0
MAX_DISTANCE = 128
ATTN_TILE = 256
SCAN_CHUNK = 256
GLA_CHUNK = 128
SAMPLE_CHUNK = 128
DECODE_PAGES_PER_STEP = 8


def _cparams(*sem, flags=None):
    return pltpu.CompilerParams(dimension_semantics=sem, vmem_limit_bytes=VMEM_LIMIT, flags=flags)


def _split_bf16(x):
    hi = x.astype(BF16)
    lo = (x - hi.astype(F32)).astype(BF16)
    return hi, lo


def _dot(a, b):
    return jnp.dot(a, b, preferred_element_type=F32)


def _dot_nt(a, b):
    return lax.dot_general(a, b, (((1,), (1,)), ((), ())), preferred_element_type=F32)


def _dot_tn(a, b):
    return lax.dot_general(a, b, (((0,), (0,)), ((), ())), preferred_element_type=F32)


def _log_sigmoid(x):
    return jnp.minimum(x, 0.0) - jnp.log(1.0 + jnp.exp(-jnp.abs(x)))


def _sigmoid(x):
    return 1.0 / (1.0 + jnp.exp(-x))


def _div_pow2(x, n):
    assert n & (n - 1) == 0
    return lax.shift_right_logical(x, n.bit_length() - 1)


def _mod_pow2(x, n):
    assert n & (n - 1) == 0
    return x & (n - 1)


def _mixer_dtype(block_rows):
    return BF16 if block_rows % 16 == 0 else F32


def _pad_rows(x, rows):
    if x.shape[0] == rows:
        return x
    return jnp.concatenate([x, jnp.zeros((rows - x.shape[0], x.shape[1]), x.dtype)], axis=0)


def _proj_down_kernel(a_ref, w_ref, r_ref, o_ref):
    o_ref[...] = r_ref[...] + _dot(a_ref[...], w_ref[...])


def _proj_down(a, w, layer, res):
    m, k = a.shape
    n = w.shape[2]
    tm = min(m, 512)
    tn = 512
    assert m % tm == 0 and n % tn == 0
    return pl.pallas_call(
        _proj_down_kernel,
        out_shape=jax.ShapeDtypeStruct((m, n), F32),
        grid=(m // tm, n // tn),
        in_specs=[pl.BlockSpec((tm, k), lambda i, j: (i, 0)),
                  pl.BlockSpec((None, k, tn), lambda i, j: (layer, 0, j)),
                  pl.BlockSpec((tm, tn), lambda i, j: (i, j))],
        out_specs=pl.BlockSpec((tm, tn), lambda i, j: (i, j)),
        compiler_params=_cparams("parallel", "parallel"),
        name="proj_down",
    )(a, w, res)


def _rms_rows(x, g):
    ms = jnp.mean(x * x, axis=-1, keepdims=True)
    return x * lax.rsqrt(ms + EPS) * g


def _proj_in_kernel(x_ref, g_ref, w_ref, ws_ref, z_ref, zs_ref, h_sc):
    @pl.when(pl.program_id(1) == 0)
    def _norm():
        h_sc[...] = _rms_rows(x_ref[...], g_ref[...]).astype(BF16)
        zs_ref[...] = _dot(h_sc[...], ws_ref[...])

    z_ref[...] = _dot(h_sc[...], w_ref[...])


def _proj_in(x, g, w_main, w_small, layer):
    m, d = x.shape
    n = w_main.shape[2]
    ns = w_small.shape[2]
    tm = min(m, 1024)
    tn = 512
    assert m % tm == 0 and n % tn == 0
    return pl.pallas_call(
        _proj_in_kernel,
        out_shape=(jax.ShapeDtypeStruct((m, n), F32), jax.ShapeDtypeStruct((m, ns), F32)),
        grid=(m // tm, n // tn),
        in_specs=[pl.BlockSpec((tm, d), lambda i, j: (i, 0)),
                  pl.BlockSpec((1, d), lambda i, j: (0, 0)),
                  pl.BlockSpec((None, d, tn), lambda i, j: (layer, 0, j)),
                  pl.BlockSpec((None, d, ns), lambda i, j: (layer, 0, 0))],
        out_specs=(pl.BlockSpec((tm, tn), lambda i, j: (i, j)),
                   pl.BlockSpec((tm, ns), lambda i, j: (i, 0))),
        scratch_shapes=[pltpu.VMEM((tm, d), BF16)],
        compiler_params=_cparams("parallel", "arbitrary"),
        name="proj_in",
    )(x, g.reshape(1, d), w_main, w_small)


def _proj_out_kernel(oa_ref, ob_ref, oc_ref, w_ref, r_ref, g_ref, x_ref, h_ref):
    ka, kb = oa_ref.shape[1], ob_ref.shape[1]
    acc = _dot(oa_ref[...].astype(BF16), w_ref[0:ka, :])
    acc += _dot(ob_ref[...].astype(BF16), w_ref[ka:ka + kb, :])
    acc += _dot(oc_ref[...].astype(BF16), w_ref[ka + kb:, :])
    x = r_ref[...] + acc
    x_ref[...] = x
    h_ref[...] = _rms_rows(x, g_ref[...]).astype(BF16)


def _proj_out(oa, ob, oc, w, layer, res, g):
    m = oa.shape[0]
    k, n = w.shape[1], w.shape[2]
    tm = min(m, 256)
    assert m % tm == 0 and oa.shape[1] + ob.shape[1] + oc.shape[1] == k
    rows = lambda width: pl.BlockSpec((tm, width), lambda i: (i, 0))
    return pl.pallas_call(
        _proj_out_kernel,
        out_shape=(jax.ShapeDtypeStruct((m, n), F32), jax.ShapeDtypeStruct((m, n), BF16)),
        grid=(m // tm,),
        in_specs=[rows(oa.shape[1]), rows(ob.shape[1]), rows(oc.shape[1]),
                  pl.BlockSpec((None, k, n), lambda i: (layer, 0, 0)),
                  rows(n),
                  pl.BlockSpec((1, n), lambda i: (0, 0))],
        out_specs=(rows(n), rows(n)),
        compiler_params=_cparams("parallel"),
        name="proj_out",
    )(oa, ob, oc, w, res, g.reshape(1, n))


def _qknorm_kernel(aq_ref, ak_ref, av_ref, qg_ref, kg_ref, bd_ref,
                   qn_ref, kn_ref, knb_ref, vb_ref, vr_ref, *, heads, inv_dk, scale, v_transposed):
    bd = bd_ref[...]
    qg = qg_ref[...]
    kg = kg_ref[...]

    def norm(x, g):
        hi, lo = _split_bf16(x * x)
        ss = _dot(hi, bd) + _dot(lo, bd)
        return x * lax.rsqrt(ss * inv_dk + EPS) * g

    for h in range(heads):
        sl = slice(LANES * h, LANES * (h + 1))
        qn_ref[:, sl] = (norm(aq_ref[:, sl], qg) * scale).astype(BF16)
        kn = norm(ak_ref[:, sl], kg)
        kn_ref[:, sl] = kn
        knb_ref[:, sl] = kn.astype(BF16)
        if v_transposed:
            vb_ref[0, sl, :] = av_ref[:, sl].T.astype(BF16)
    if not v_transposed:
        vb_ref[...] = av_ref[...].astype(BF16)
    vr_ref[...] = av_ref[...]


def _qknorm(z, qg, kg, heads, dk, seq, v_transposed):
    m = z.shape[0]
    w = heads * LANES
    tm = min(m, 256)
    assert m % tm == 0
    lane = np.arange(LANES)
    bd = jnp.asarray((lane[:, None] // dk) == (lane[None, :] // dk), BF16)
    reps = LANES // dk
    kern = functools.partial(_qknorm_kernel, heads=heads, inv_dk=1.0 / dk, scale=dk ** -0.5,
                             v_transposed=v_transposed)
    rows = pl.BlockSpec((tm, w), lambda i: (i, 0))
    if v_transposed:
        tps = seq // tm
        v_shape = jax.ShapeDtypeStruct((m // seq, w, seq), BF16)
        v_spec = pl.BlockSpec((1, w, tm), lambda i: (i // tps, 0, i % tps))
    else:
        v_shape = jax.ShapeDtypeStruct((m, w), BF16)
        v_spec = rows
    return pl.pallas_call(
        kern,
        out_shape=(jax.ShapeDtypeStruct((m, w), BF16), jax.ShapeDtypeStruct((m, w), F32),
                   jax.ShapeDtypeStruct((m, w), BF16), v_shape, jax.ShapeDtypeStruct((m, w), F32)),
        grid=(m // tm,),
        in_specs=[pl.BlockSpec((tm, w), lambda i: (i, 0)),
                  pl.BlockSpec((tm, w), lambda i: (i, 1)),
                  pl.BlockSpec((tm, w), lambda i: (i, 2)),
                  pl.BlockSpec((1, LANES), lambda i: (0, 0)),
                  pl.BlockSpec((1, LANES), lambda i: (0, 0)),
                  pl.BlockSpec((LANES, LANES), lambda i: (0, 0))],
        out_specs=(rows, rows, rows, v_spec, rows),
        compiler_params=_cparams("parallel"),
        name="qknorm",
    )(z, z, z, jnp.tile(qg, reps).reshape(1, LANES), jnp.tile(kg, reps).reshape(1, LANES), bd)


def _bucket(n, n_buckets):
    max_exact = n_buckets // 2
    nf = jnp.maximum(n, max_exact).astype(F32)
    large = max_exact + (jnp.log(nf / max_exact) / math.log(MAX_DISTANCE / max_exact)
                         * (n_buckets - max_exact)).astype(jnp.int32)
    large = jnp.minimum(large, n_buckets - 1)
    return jnp.where(n < max_exact, jnp.maximum(n, 0), large)


def _prompt_bias_kernel(rb_ref, o_ref, *, tile, n_buckets):
    h = pl.program_id(0)
    j = lax.broadcasted_iota(jnp.int32, (tile, tile), 0)
    i = lax.broadcasted_iota(jnp.int32, (tile, tile), 1)
    far = jnp.full((tile, tile), MAX_DISTANCE, jnp.int32)
    for t, n in enumerate((i - j, tile + i - j, far)):
        b = _bucket(n, n_buckets)
        val = jnp.full((tile, tile), rb_ref[0, h], F32)
        for k in range(1, n_buckets):
            val = jnp.where(b == k, rb_ref[k, h], val)
        o_ref[0, t] = jnp.where(n >= 0, val, NEG)


def _prompt_bias(rel_bias, tile):
    n_buckets, heads = rel_bias.shape
    kern = functools.partial(_prompt_bias_kernel, tile=tile, n_buckets=n_buckets)
    return pl.pallas_call(
        kern,
        out_shape=jax.ShapeDtypeStruct((heads, 3, tile, tile), F32),
        grid=(heads,),
        in_specs=[pl.BlockSpec(memory_space=pltpu.SMEM)],
        out_specs=pl.BlockSpec((1, 3, tile, tile), lambda h: (h, 0, 0, 0)),
        compiler_params=_cparams("parallel"),
        name="prompt_bias",
    )(rel_bias)


def _decode_bias_kernel(rb_ref, pg_ref, new_ref, *, page, ppb, dec_seq, heads, n_buckets):
    grp = 2 * dec_seq

    def lookup(n, h):
        b = _bucket(n, n_buckets)
        val = jnp.full(n.shape, rb_ref[0, h], F32)
        for k in range(1, n_buckets):
            val = jnp.where(b == k, rb_ref[k, h], val)
        return jnp.where(n >= 0, val, NEG)

    qi = _mod_pow2(lax.broadcasted_iota(jnp.int32, (grp, page), 0), dec_seq)
    j = lax.broadcasted_iota(jnp.int32, (grp, page), 1)
    far = jnp.full((grp, page), MAX_DISTANCE, jnp.int32)
    for h in range(heads):
        rs = slice(grp * h, grp * (h + 1))
        far_b = lookup(far, h)
        for c in range(ppb):
            pg_ref[0, rs, page * c:page * (c + 1)] = far_b
            pg_ref[1, rs, page * c:page * (c + 1)] = far_b if c < ppb - 1 else lookup(page + qi - j, h)
        new_ref[rs, :] = lookup(jnp.where(j < dec_seq, qi - j, -1), h)


def _decode_bias(rel_bias, page, ppb, dec_seq):
    n_buckets, heads = rel_bias.shape
    rows = 2 * dec_seq * heads
    kern = functools.partial(_decode_bias_kernel, page=page, ppb=ppb, dec_seq=dec_seq, heads=heads,
                             n_buckets=n_buckets)
    return pl.pallas_call(
        kern,
        out_shape=(jax.ShapeDtypeStruct((2, rows, ppb * page), F32),
                   jax.ShapeDtypeStruct((rows, page), F32)),
        in_specs=[pl.BlockSpec(memory_space=pltpu.SMEM)],
        name="decode_bias",
    )(rel_bias)


def _diff_lambda(dl, lam_init):
    s1 = jnp.sum(dl[0:1] * dl[1:2], axis=-1, keepdims=True)
    s2 = jnp.sum(dl[2:3] * dl[3:4], axis=-1, keepdims=True)
    return jnp.exp(s1) - jnp.exp(s2) + lam_init


def _subln(o, g, lam_init):
    ms = jnp.mean(o * o, axis=-1, keepdims=True)
    return o * lax.rsqrt(ms + EPS) * g * (1.0 - lam_init)


def _flash_update(s, m_prev, l_prev):
    m_new = jnp.maximum(m_prev, jnp.max(s, axis=-1, keepdims=True))
    alpha = jnp.exp(m_prev - m_new)
    p = jnp.exp(s - m_new)
    l_new = alpha * l_prev + jnp.sum(p, axis=-1, keepdims=True)
    return p, alpha, m_new, l_new


def _attn_kernel(qt_ref, kt_ref, q_ref, k_ref, vt_ref, bias_ref, dl_ref, g_ref, o_ref, m_sc, l_sc, acc_sc,
                 *, heads, tile, dk, lam_init):
    qi = qt_ref[pl.program_id(1)]
    ki = kt_ref[pl.program_id(1)]
    n_sub = tile // LANES

    @pl.when(ki == 0)
    def _init():
        m_sc[...] = jnp.full(m_sc.shape, -jnp.inf, F32)
        l_sc[...] = jnp.zeros(l_sc.shape, F32)
        acc_sc[...] = jnp.zeros(acc_sc.shape, F32)

    lane = lax.broadcasted_iota(jnp.int32, (LANES, LANES), 1)
    for h in range(heads):
        sl = slice(LANES * h, LANES * (h + 1))
        kh = k_ref[0, :, sl]
        vht = vt_ref[0, sl, :]
        for c in range(2 * n_sub):
            qrows = slice(LANES * (c % n_sub), LANES * (c % n_sub + 1))
            cs = slice(LANES * c, LANES * (c + 1))
            qc = q_ref[0, qrows, sl]
            own_map = (lane < dk) if c < n_sub else (lane >= dk)
            qc = jnp.where(own_map, qc, jnp.zeros_like(qc))
            s = _dot_nt(kh, qc) + bias_ref[h, 0, :, qrows]
            m_prev = m_sc[h, :, cs]
            m_new = jnp.maximum(m_prev, jnp.max(s, axis=0, keepdims=True))
            alpha = jnp.exp(m_prev - m_new)
            p = jnp.exp(s - m_new)
            l_sc[h, :, cs] = alpha * l_sc[h, :, cs] + jnp.sum(p, axis=0, keepdims=True)
            acc_sc[h, :, cs] = alpha * acc_sc[h, :, cs] + _dot(vht, p.astype(BF16))
            m_sc[h, :, cs] = m_new

    @pl.when(ki == qi)
    def _finish():
        lam = _diff_lambda(dl_ref[...], lam_init)
        g_col = g_ref[...]
        for h in range(heads):
            acc = acc_sc[h]
            l = l_sc[h]
            ot = acc[:, :tile] / l[:, :tile] - lam * (acc[:, tile:] / l[:, tile:])
            ms = jnp.mean(ot * ot, axis=0, keepdims=True)
            ot = ot * lax.rsqrt(ms + EPS) * g_col * (1.0 - lam_init)
            o_ref[0, :, LANES * h:LANES * (h + 1)] = ot.T.astype(o_ref.dtype)


def _prompt_attention(qn, knb, vt, bias, dl, g, lam_init, dk):
    bsz, seq, w = qn.shape
    heads = w // LANES
    tile = bias.shape[-1]
    nq = seq // tile
    assert seq % tile == 0 and tile % LANES == 0
    kern = functools.partial(_attn_kernel, heads=heads, tile=tile, dk=dk, lam_init=lam_init)
    pairs = [(qi, ki) for qi in range(nq) for ki in range(qi + 1)]
    q_tab = jnp.asarray([p[0] for p in pairs], jnp.int32)
    k_tab = jnp.asarray([p[1] for p in pairs], jnp.int32)

    def bias_idx(b, t, qt, kt):
        return (0, jnp.where(kt[t] == qt[t], 0, jnp.where(kt[t] == qt[t] - 1, 1, 2)), 0, 0)

    grid_spec = pltpu.PrefetchScalarGridSpec(
        num_scalar_prefetch=2,
        grid=(bsz, len(pairs)),
        in_specs=[pl.BlockSpec((1, tile, w), lambda b, t, qt, kt: (b, qt[t], 0)),
                  pl.BlockSpec((1, tile, w), lambda b, t, qt, kt: (b, kt[t], 0)),
                  pl.BlockSpec((1, w, tile), lambda b, t, qt, kt: (b, 0, kt[t])),
                  pl.BlockSpec((heads, 1, tile, tile), bias_idx),
                  pl.BlockSpec(dl.shape, lambda b, t, qt, kt: (0, 0)),
                  pl.BlockSpec((LANES, 1), lambda b, t, qt, kt: (0, 0))],
        out_specs=pl.BlockSpec((1, tile, w), lambda b, t, qt, kt: (b, qt[t], 0)),
        scratch_shapes=[pltpu.VMEM((heads, 1, 2 * tile), F32),
                        pltpu.VMEM((heads, 1, 2 * tile), F32),
                        pltpu.VMEM((heads, LANES, 2 * tile), F32)],
    )
    return pl.pallas_call(
        kern,
        out_shape=jax.ShapeDtypeStruct((bsz, seq, w), BF16),
        grid_spec=grid_spec,
        compiler_params=_cparams("parallel", "arbitrary"),
        name="prompt_attention",
    )(q_tab, k_tab, qn, knb, vt, bias, dl, g.reshape(LANES, 1))


def _decode_attn_kernel(pt_ref, q_ref, kn_ref, vn_ref, *refs, heads, dec_seq, page, ppb, dk, lam_init):
    del pt_ref
    kc_refs, vc_refs = refs[:ppb], refs[ppb:2 * ppb]
    bias_ref, bnew_ref, dl_ref, g_ref, o_ref, wq_sc, wqb_sc, m_sc, l_sc, acc_sc = refs[2 * ppb:]
    p_idx = pl.program_id(1)
    rows = 2 * dec_seq * heads
    grp = 2 * dec_seq
    width = heads * LANES

    @pl.when(p_idx == 0)
    def _init():
        q = q_ref[0].astype(F32)
        lane = lax.broadcasted_iota(jnp.int32, (dec_seq, LANES), 1)
        pieces = []
        for h in range(heads):
            qh = q[:, LANES * h:LANES * (h + 1)]
            pieces += [jnp.where(lane < dk, qh, 0.0), jnp.where(lane >= dk, qh, 0.0)]
        wq = jnp.concatenate(pieces, axis=0)
        wq_sc[...] = wq.astype(BF16)
        c = lax.broadcasted_iota(jnp.int32, (rows, width), 1)
        r = lax.broadcasted_iota(jnp.int32, (rows, width), 0)
        own = _div_pow2(r, grp) == _div_pow2(c, LANES)
        wqb_sc[...] = jnp.where(own, jnp.concatenate([wq] * heads, axis=1), 0.0).astype(BF16)
        m_sc[...] = jnp.full(m_sc.shape, -jnp.inf, F32)
        l_sc[...] = jnp.zeros(l_sc.shape, F32)
        acc_sc[...] = jnp.zeros(acc_sc.shape, F32)

    def head_rows(ref, h):
        return ref[pl.ds(h, page, stride=heads), :].astype(BF16)

    s = jnp.concatenate(
        [jnp.concatenate([_dot_nt(wq_sc[grp * h:grp * (h + 1), :], head_rows(kc, h)) for h in range(heads)],
                         axis=0) for kc in kc_refs], axis=1) + bias_ref[0]
    p, alpha, m_new, l_new = _flash_update(s, m_sc[...], l_sc[...])
    m_sc[...] = m_new
    l_sc[...] = l_new
    pb = p.astype(BF16)
    for h in range(heads):
        rs = slice(grp * h, grp * (h + 1))
        acc = alpha[rs] * acc_sc[rs, :]
        for j, vc in enumerate(vc_refs):
            acc += _dot(pb[rs, page * j:page * (j + 1)], head_rows(vc, h))
        acc_sc[rs, :] = acc

    @pl.when(p_idx == pl.num_programs(1) - 1)
    def _finish():
        kn = _pad_rows(kn_ref[0].astype(F32), page).astype(BF16)
        vn = _pad_rows(vn_ref[0].astype(F32), page)
        s = _dot_nt(wqb_sc[...], kn) + bnew_ref[...]
        p, alpha, _, l_fin = _flash_update(s, m_sc[...], l_sc[...])
        pb = p.astype(BF16)
        lam = _diff_lambda(dl_ref[...], lam_init)
        g = g_ref[...]
        for h in range(heads):
            rs = slice(grp * h, grp * (h + 1))
            vh = vn[:, LANES * h:LANES * (h + 1)].astype(BF16)
            acc = alpha[rs] * acc_sc[rs, :] + _dot(pb[rs], vh)
            l = l_fin[rs]
            o = acc[:dec_seq] / l[:dec_seq] - lam * (acc[dec_seq:] / l[dec_seq:])
            o_ref[0, :, LANES * h:LANES * (h + 1)] = _subln(o, g, lam_init)


def _decode_attention(layer, qn, knb, vb, cache_k, cache_v, page_table, bias_pg, bias_new, dl, g, lam_init, dk):
    bsz, dec_seq, w = qn.shape
    heads = w // LANES
    page = cache_k.shape[2] // heads
    ppb = bias_pg.shape[-1] // page
    n_steps = page_table.shape[1] // ppb
    assert page_table.shape[1] % ppb == 0
    rows = 2 * dec_seq * heads
    kern = functools.partial(_decode_attn_kernel, heads=heads, dec_seq=dec_seq, page=page, ppb=ppb, dk=dk,
                             lam_init=lam_init)
    new_spec = pl.BlockSpec((1, dec_seq, w), lambda b, p, pt: (b, 0, 0))
    cache_specs = [pl.BlockSpec((None, None, page * heads, LANES),
                                lambda b, p, pt, j=j: (layer, pt[b, p * ppb + j], 0, 0)) for j in range(ppb)]
    grid_spec = pltpu.PrefetchScalarGridSpec(
        num_scalar_prefetch=1,
        grid=(bsz, n_steps),
        in_specs=[new_spec, new_spec, new_spec, *cache_specs, *cache_specs,
                  pl.BlockSpec((1, rows, ppb * page),
                               lambda b, p, pt: (jnp.where(p == n_steps - 1, 1, 0), 0, 0)),
                  pl.BlockSpec((rows, page), lambda b, p, pt: (0, 0)),
                  pl.BlockSpec(dl.shape, lambda b, p, pt: (0, 0)),
                  pl.BlockSpec((1, LANES), lambda b, p, pt: (0, 0))],
        out_specs=pl.BlockSpec((1, dec_seq, w), lambda b, p, pt: (b, 0, 0)),
        scratch_shapes=[pltpu.VMEM((rows, LANES), BF16),
                        pltpu.VMEM((rows, w), BF16),
                        pltpu.VMEM((rows, 1), F32),
                        pltpu.VMEM((rows, 1), F32),
                        pltpu.VMEM((rows, LANES), F32)],
    )
    return pl.pallas_call(
        kern,
        out_shape=jax.ShapeDtypeStruct((bsz, dec_seq, w), F32),
        grid_spec=grid_spec,
        compiler_params=_cparams("parallel", "arbitrary"),
        name="decode_attention",
    )(page_table, qn, knb, vb, *([cache_k] * ppb), *([cache_v] * ppb), bias_pg, bias_new, dl,
      g.reshape(1, LANES))


GATE_I = 16
GATE_F = 20


def _mlstm_kernel(q_ref, k_ref, v_ref, og_ref, zs_ref, gt_ref, gbl_ref, gbc_ref, g_ref,
                  c0_ref, n0_ref, m0_ref,
                  o_ref, cf_ref, nf_ref, mf_ref, c_sc, n_sc, m_sc,
                  *, heads, chunk, valid, scale):
    c_idx = pl.program_id(1)

    @pl.when(c_idx == 0)
    def _init():
        c_sc[...] = c0_ref[0]
        n_sc[...] = n0_ref[0]
        m_sc[...] = m0_ref[0]

    rows_in = q_ref.shape[0]
    row = lax.broadcasted_iota(jnp.int32, (chunk, 1), 0)
    t_i = lax.broadcasted_iota(jnp.int32, (chunk, chunk), 0)
    s_i = lax.broadcasted_iota(jnp.int32, (chunk, chunk), 1)
    causal = t_i >= s_i
    tril = causal.astype(BF16)
    triu = (t_i <= s_i).astype(BF16)

    gcol = _pad_rows(zs_ref[...], chunk) + gbl_ref[...]
    lf_mat = _log_sigmoid(gcol)
    if valid < chunk:
        lf_mat = jnp.where(row < valid, lf_mat, 0.0)
    hi, lo = _split_bf16(lf_mat)
    b_mat = _dot(tril, hi) + _dot(tril, lo)

    grow = gt_ref[0] + gbc_ref[...]
    col = lax.broadcasted_iota(jnp.int32, grow.shape, 1)
    grow_id = lax.broadcasted_iota(jnp.int32, grow.shape, 0)
    lf_rows = jnp.where(grow_id >= heads, _log_sigmoid(grow), 0.0)
    li_rows = grow
    if valid < chunk:
        lf_rows = jnp.where(col < valid, lf_rows, 0.0)
        li_rows = jnp.where(col < valid, li_rows, NEG)
    hi, lo = _split_bf16(lf_rows)
    b_rows = _dot(hi, triu) + _dot(lo, triu)

    g = g_ref[...]
    for h in range(heads):
        sl = slice(LANES * h, LANES * (h + 1))
        b_col = b_mat[:, GATE_F + h:GATE_F + h + 1]
        li_col = gcol[:, GATE_I + h:GATE_I + h + 1]
        if valid < chunk:
            li_col = jnp.where(row < valid, li_col, NEG)
        b_row = b_rows[heads + h:heads + h + 1, :]
        li_row = li_rows[h:h + 1, :]
        d = jnp.where(causal, b_col - b_row + li_row, NEG)
        m_prev = m_sc[h:h + 1, 0:1]
        inter = b_col + m_prev
        mt = jnp.maximum(inter, jnp.max(d, axis=-1, keepdims=True))
        wi = jnp.exp(inter - mt)
        q = _pad_rows(q_ref[:, sl], chunk)
        ks = _pad_rows(k_ref[:, sl], chunk) * scale
        vb = _pad_rows(v_ref[:, sl], chunk).astype(BF16)
        qb = q.astype(BF16)
        p = _dot_nt(qb, ks.astype(BF16)) * jnp.exp(d - mt)
        c_prev = c_sc[h]
        n_prev = n_sc[h:h + 1, :]
        num = wi * _dot(qb, c_prev.astype(BF16)) + _dot(p.astype(BF16), vb)
        den = wi * jnp.sum(q * n_prev, axis=-1, keepdims=True) + jnp.sum(p, axis=-1, keepdims=True)
        hh = num / jnp.maximum(jnp.abs(den), jnp.exp(-mt))
        og = _sigmoid(og_ref[:, sl])
        y = og * hh[:rows_in]
        ms = jnp.mean(y * y, axis=-1, keepdims=True)
        o_ref[:, sl] = (y * lax.rsqrt(ms + EPS) * g).astype(o_ref.dtype)

        m_new = mt[chunk - 1:chunk, :]
        b_last = b_col[chunk - 1:chunk, :]
        a = jnp.exp(b_last + m_prev - m_new)
        ws = jnp.exp(b_last - b_col + li_col - m_new)
        kw = ks * ws
        c_sc[h] = a * c_prev + _dot_tn(kw.astype(BF16), vb)
        n_sc[h:h + 1, :] = a * n_prev + jnp.sum(kw, axis=0, keepdims=True)
        m_sc[h:h + 1, :] = jnp.broadcast_to(m_new, (1, LANES))

    @pl.when(c_idx == pl.num_programs(1) - 1)
    def _finish():
        cf_ref[0] = c_sc[...]
        nf_ref[0] = n_sc[...]
        mf_ref[0] = m_sc[...]


def _mlstm(z, zs, gt, gate_b, g, c0, n0, m0, seq, chunk, col0):
    bsz, heads = c0.shape[0], c0.shape[1]
    w = heads * LANES
    rows_in = min(seq, chunk)
    nc = seq // rows_in
    valid = rows_in
    gt_w = gt.shape[-1] // nc
    kern = functools.partial(_mlstm_kernel, heads=heads, chunk=chunk, valid=valid, scale=LANES ** -0.5)
    gbl = jnp.zeros((1, LANES), F32)
    gbl = gbl.at[0, GATE_I:GATE_I + heads].set(gate_b[0]).at[0, GATE_F:GATE_F + heads].set(gate_b[1])
    gbc = gate_b.reshape(2 * heads, 1)
    m0b = jnp.broadcast_to(m0[:, :, None], (bsz, heads, LANES))

    def zspec(blk):
        return pl.BlockSpec((rows_in, w), lambda b, c: (b * nc + c, col0 + blk))

    state = lambda shape: pl.BlockSpec((1,) + shape, lambda b, c: (b,) + (0,) * len(shape))
    return pl.pallas_call(
        kern,
        out_shape=(jax.ShapeDtypeStruct((bsz * seq, w), _mixer_dtype(rows_in)),
                   jax.ShapeDtypeStruct((bsz, heads, LANES, LANES), F32),
                   jax.ShapeDtypeStruct((bsz, heads, LANES), F32),
                   jax.ShapeDtypeStruct((bsz, heads, LANES), F32)),
        grid=(bsz, nc),
        in_specs=[zspec(0), zspec(1), zspec(2), zspec(3),
                  pl.BlockSpec((rows_in, LANES), lambda b, c: (b * nc + c, 0)),
                  pl.BlockSpec((1, 2 * heads, gt_w), lambda b, c: (b, 0, c)),
                  pl.BlockSpec((1, LANES), lambda b, c: (0, 0)),
                  pl.BlockSpec((2 * heads, 1), lambda b, c: (0, 0)),
                  pl.BlockSpec((1, LANES), lambda b, c: (0, 0)),
                  state((heads, LANES, LANES)), state((heads, LANES)), state((heads, LANES))],
        out_specs=(pl.BlockSpec((rows_in, w), lambda b, c: (b * nc + c, 0)),
                   state((heads, LANES, LANES)), state((heads, LANES)), state((heads, LANES))),
        scratch_shapes=[pltpu.VMEM((heads, LANES, LANES), F32),
                        pltpu.VMEM((heads, LANES), F32),
                        pltpu.VMEM((heads, LANES), F32)],
        compiler_params=_cparams("parallel", "arbitrary"),
        name="mlstm",
    )(z, z, z, z, zs, gt, gbl, gbc, g.reshape(1, LANES), c0, n0, m0b)


def _gla_levels(chunk):
    n, out = chunk, []
    while n >= 2:
        out.append(n)
        n //= 2
    return out


def _gla_weights(chunk):
    t = np.arange(chunk)[:, None]
    s = np.arange(chunk)[None, :]
    blocks = [(s <= t).astype(np.float32), (s > t).astype(np.float32)]
    for n in _gla_levels(chunk):
        mid = (t // n) * n + n // 2 - 1
        blocks.append(((s > mid) & (s <= t)).astype(np.float32) - ((s > t) & (s <= mid)).astype(np.float32))
    return jnp.asarray(np.concatenate(blocks, axis=0), BF16)


def _gla_kernel(q_ref, k_ref, v_ref, gg_ref, zs_ref, wa_ref, ba_ref, ws_ref, g_ref, s0_ref,
                o_ref, sf_ref, s_sc, *, heads, chunk, valid, dk, scale):
    c_idx = pl.program_id(1)
    per = LANES // dk
    assert per == 2 and heads % per == 0

    @pl.when(c_idx == 0)
    def _init():
        for h in range(heads):
            s_sc[h // per, dk * (h % per):dk * (h % per + 1), :] = s0_ref[0, h]

    rows_in = q_ref.shape[0]
    row = lax.broadcasted_iota(jnp.int32, (chunk, 1), 0)
    t_i = lax.broadcasted_iota(jnp.int32, (per * chunk, chunk), 0) & (chunk - 1)
    s_i = lax.broadcasted_iota(jnp.int32, (per * chunk, chunk), 1)
    first_head = lax.broadcasted_iota(jnp.int32, (chunk, LANES), 1) < dk

    zs = _pad_rows(zs_ref[...], chunk)
    la = _log_sigmoid(_dot(zs.astype(BF16), wa_ref[...]) + ba_ref[...]) * (1.0 / GLA_TAU)
    if valid < chunk:
        la = jnp.where(row < valid, la, 0.0)
    hi, lo = _split_bf16(la)
    wst = ws_ref[...]
    e_all = _dot(wst, hi) + _dot(wst, lo)
    ones = jnp.ones((chunk, LANES), BF16)
    levels = _gla_levels(chunk)
    g = g_ref[...]
    second_half = [(row & (n - 1)) >= n // 2 for n in levels]
    same_node = [_div_pow2(t_i, n) == _div_pow2(s_i, n) for n in levels]

    def split_heads(x):
        return [jnp.where(first_head, x, 0.0), jnp.where(first_head, 0.0, x)]

    for j in range(heads // per):
        psl = slice(LANES * j, LANES * (j + 1))
        q = _pad_rows(q_ref[:, psl], chunk) * scale
        k = _pad_rows(k_ref[:, psl], chunk)
        bc = e_all[0:chunk, psl]
        rem = e_all[chunk:2 * chunk, psl]
        a_pair = jnp.zeros((per * chunk, chunk), F32)
        for li in range(len(levels)):
            e = e_all[(2 + li) * chunk:(3 + li) * chunk, psl]
            qt = jnp.where(second_half[li], q * jnp.exp(jnp.minimum(e, 0.0)), 0.0)
            kt = jnp.where(second_half[li], 0.0, k * jnp.exp(jnp.minimum(-e, 0.0)))
            qs = jnp.concatenate(split_heads(qt), axis=0).astype(BF16)
            a_pair = a_pair + jnp.where(same_node[li], _dot_nt(qs, kt.astype(BF16)), 0.0)
        s_prev = s_sc[j]
        s_prev_b = s_prev.astype(BF16)
        qd = split_heads(q * jnp.exp(bc))
        qk = split_heads(q * k)
        kd = split_heads(k * jnp.exp(rem))
        hi_p, lo_p = _split_bf16(la[:, psl])
        tot = _dot_tn(hi_p, ones) + _dot_tn(lo_p, ones)
        s_new = jnp.exp(tot) * s_prev
        for hh in range(per):
            h = per * j + hh
            vsl = slice(LANES * h, LANES * (h + 1))
            vb = _pad_rows(v_ref[:, vsl], chunk).astype(BF16)
            a_h = a_pair[chunk * hh:chunk * (hh + 1)]
            o = _dot(qd[hh].astype(BF16), s_prev_b) + _dot(a_h.astype(BF16), vb)
            o = o + jnp.sum(qk[hh], axis=-1, keepdims=True) * vb.astype(F32)
            o = o[:rows_in]
            ms = jnp.mean(o * o, axis=-1, keepdims=True)
            gate = gg_ref[:, vsl]
            o_ref[:, vsl] = (o * lax.rsqrt(ms + EPS) * g * (gate * _sigmoid(gate))).astype(o_ref.dtype)
            s_new = s_new + _dot_tn(kd[hh].astype(BF16), vb)
        s_sc[j] = s_new

    @pl.when(c_idx == pl.num_programs(1) - 1)
    def _finish():
        for h in range(heads):
            sf_ref[0, h] = s_sc[h // per, dk * (h % per):dk * (h % per + 1), :]


def _gla(z, zs, w_alpha, b_alpha, g, s0, seq, chunk, qcol, kcol, vcol, gcol):
    bsz, heads, dk, dv = s0.shape
    rows_in = min(seq, chunk)
    nc = seq // rows_in
    kw = heads * dk
    vw = heads * dv
    wa = jnp.zeros((LANES, kw), F32).at[:w_alpha.shape[0]].set(w_alpha).astype(BF16)
    wst = _gla_weights(chunk)
    kern = functools.partial(_gla_kernel, heads=heads, chunk=chunk, valid=rows_in, dk=dk, scale=dk ** -0.5)
    const = lambda shape: pl.BlockSpec(shape, lambda b, c: (0,) * len(shape))
    return pl.pallas_call(
        kern,
        out_shape=(jax.ShapeDtypeStruct((bsz * seq, vw), _mixer_dtype(rows_in)),
                   jax.ShapeDtypeStruct((bsz, heads, dk, dv), F32)),
        grid=(bsz, nc),
        in_specs=[pl.BlockSpec((rows_in, kw), lambda b, c: (b * nc + c, qcol)),
                  pl.BlockSpec((rows_in, kw), lambda b, c: (b * nc + c, kcol)),
                  pl.BlockSpec((rows_in, vw), lambda b, c: (b * nc + c, vcol)),
                  pl.BlockSpec((rows_in, vw), lambda b, c: (b * nc + c, gcol)),
                  pl.BlockSpec((rows_in, LANES), lambda b, c: (b * nc + c, 0)),
                  const((LANES, kw)), const((1, kw)), const(wst.shape), const((1, LANES)),
                  pl.BlockSpec((1, heads, dk, dv), lambda b, c: (b, 0, 0, 0))],
        out_specs=(pl.BlockSpec((rows_in, vw), lambda b, c: (b * nc + c, 0)),
                   pl.BlockSpec((1, heads, dk, dv), lambda b, c: (b, 0, 0, 0))),
        scratch_shapes=[pltpu.VMEM((kw // LANES, LANES, dv), F32)],
        compiler_params=_cparams("parallel", "arbitrary"),
        name="gla",
    )(z, z, z, z, zs, wa, b_alpha.reshape(1, kw), wst, g.reshape(1, LANES), s0)


def _conv_gate(ug, uv, cwg, cwv, cbg, cbv, prev):
    def conv(u, cw, cb, which):
        u1, u2 = prev(u, which)
        return cb + cw[0:1] * u2 + cw[1:2] * u1 + cw[2:3] * u
    cg = conv(ug, cwg, cbg, 0)
    cv = conv(uv, cwv, cbv, 1)
    return cg * _sigmoid(cg) * cv


FFN_SUB = 256


def _ffn_up_prompt_kernel(h_ref, wg_ref, wv_ref, cwg_ref, cwv_ref, cbg_ref, cbv_ref,
                          act_ref, tg_ref, tv_ref, u_sc, w_sc, *, tiles_per_seq):
    i = pl.program_id(1)
    first = (i % tiles_per_seq) == 0
    hb = h_ref[...]
    tm = hb.shape[0]

    @pl.when(i == 0)
    def _cast_weights():
        w_sc[0] = wg_ref[...].astype(BF16)
        w_sc[1] = wv_ref[...].astype(BF16)

    @pl.when(first)
    def _reset():
        u_sc[:, 0:8, :] = jnp.zeros((2, 8, u_sc.shape[2]), F32)

    def conv_half(which, u, cw_ref, cb_ref, t_ref, cs):
        cw = cw_ref[:, cs]
        u_sc[which, 8:8 + tm, cs] = u
        u1 = u_sc[which, 7:7 + tm, cs]
        u2 = u_sc[which, 6:6 + tm, cs]
        conv = cb_ref[:, cs] + cw[0:1] * u2 + cw[1:2] * u1 + cw[2:3] * u
        u_sc[which, 0:8, cs] = u[tm - 8:tm]
        t_ref[0, :, cs] = u[tm - 8:tm]
        return conv

    subs = [slice(c0, c0 + FFN_SUB) for c0 in range(0, wg_ref.shape[1], FFN_SUB)]
    for cs in subs:
        cg = conv_half(0, _dot(hb, w_sc[0, :, cs]), cwg_ref, cbg_ref, tg_ref, cs)
        cv = conv_half(1, _dot(hb, w_sc[1, :, cs]), cwv_ref, cbv_ref, tv_ref, cs)
        act_ref[:, cs] = (cg * _sigmoid(cg) * cv).astype(BF16)


def _ffn_up_prompt(h2, w_up, layer, conv_w, conv_b, bsz, seq):
    m, d = h2.shape
    dff = w_up.shape[2] // 2
    tm, tn = 1024, 512
    assert seq % tm == 0 and dff % tn == 0 and tn % FFN_SUB == 0
    nj = dff // tn
    tps = seq // tm
    kern = functools.partial(_ffn_up_prompt_kernel, tiles_per_seq=tps)
    wspec = lambda off: pl.BlockSpec((None, d, tn), lambda j, i: (layer, 0, j + off))
    cspec = lambda r, off: pl.BlockSpec((r, tn), lambda j, i: (0, j + off))
    tail = pl.BlockSpec((1, 8, tn), lambda j, i: (i // tps, 0, j))
    cb = conv_b.reshape(1, -1)
    return pl.pallas_call(
        kern,
        out_shape=(jax.ShapeDtypeStruct((m, dff), BF16),
                   jax.ShapeDtypeStruct((bsz, 8, dff), F32),
                   jax.ShapeDtypeStruct((bsz, 8, dff), F32)),
        grid=(nj, m // tm),
        in_specs=[pl.BlockSpec((tm, d), lambda j, i: (i, 0)),
                  wspec(0), wspec(nj), cspec(3, 0), cspec(3, nj), cspec(1, 0), cspec(1, nj)],
        out_specs=(pl.BlockSpec((tm, tn), lambda j, i: (i, j)), tail, tail),
        scratch_shapes=[pltpu.VMEM((2, tm + 8, tn), F32), pltpu.VMEM((2, d, tn), BF16)],
        compiler_params=_cparams("parallel", "arbitrary"),
        name="ffn_up_prompt",
    )(h2, w_up, w_up, conv_w, conv_w, cb, cb)


def _ffn_up_sample_kernel(h_ref, wg_ref, wv_ref, cwg_ref, cwv_ref, cbg_ref, cbv_ref,
                          p1g_ref, p1v_ref, p2g_ref, p2v_ref, act_ref, ug_ref, uv_ref, *, seq):
    hb = h_ref[...]
    ug = _dot(hb, wg_ref[...].astype(BF16))
    uv = _dot(hb, wv_ref[...].astype(BF16))
    tm = ug.shape[0]
    pos = _mod_pow2(lax.broadcasted_iota(jnp.int32, (tm, 1), 0), seq)
    p1 = (p1g_ref, p1v_ref)
    p2 = (p2g_ref, p2v_ref)

    def prev(u, which):
        u1 = jnp.where(pos >= 1, pltpu.roll(u, 1, 0), p1[which][...])
        u2 = jnp.where(pos >= 2, pltpu.roll(u, 2, 0), p2[which][...])
        return u1, u2

    act_ref[...] = _conv_gate(ug, uv, cwg_ref[...], cwv_ref[...], cbg_ref[...], cbv_ref[...], prev).astype(BF16)
    ug_ref[...] = ug
    uv_ref[...] = uv


def _ffn_up_sample(h2, w_up, layer, conv_w, conv_b, conv_state, seq):
    m, d = h2.shape
    dff = w_up.shape[2] // 2
    tn = 512
    assert dff % tn == 0
    nj = dff // tn
    bsz = m // seq
    p1 = jnp.zeros((bsz, seq, 2 * dff), F32).at[:, 0].set(conv_state[:, 1]).reshape(m, 2 * dff)
    p2 = jnp.zeros((bsz, seq, 2 * dff), F32).at[:, 0].set(conv_state[:, 0]).at[:, 1].set(conv_state[:, 1])
    p2 = p2.reshape(m, 2 * dff)
    kern = functools.partial(_ffn_up_sample_kernel, seq=seq)
    wspec = lambda off: pl.BlockSpec((None, d, tn), lambda j: (layer, 0, j + off))
    cspec = lambda r, off: pl.BlockSpec((r, tn), lambda j: (0, j + off))
    cb = conv_b.reshape(1, -1)
    ospec = pl.BlockSpec((m, tn), lambda j: (0, j))
    return pl.pallas_call(
        kern,
        out_shape=(jax.ShapeDtypeStruct((m, dff), BF16),
                   jax.ShapeDtypeStruct((m, dff), F32),
                   jax.ShapeDtypeStruct((m, dff), F32)),
        grid=(nj,),
        in_specs=[pl.BlockSpec((m, d), lambda j: (0, 0)),
                  wspec(0), wspec(nj), cspec(3, 0), cspec(3, nj), cspec(1, 0), cspec(1, nj),
                  cspec(m, 0), cspec(m, nj), cspec(m, 0), cspec(m, nj)],
        out_specs=(ospec, ospec, ospec),
        compiler_params=_cparams("parallel"),
        name="ffn_up_sample",
    )(h2, w_up, w_up, conv_w, conv_w, cb, cb, p1, p1, p2, p2)


def _layer(x, l, w, dims, attn_fn, mstate, gstate, conv_state, bsz, seq):
    (heads_a, dk_a, heads_b, heads_c, dk_c) = dims
    wa = heads_a * LANES
    wb = heads_b * LANES
    prompt = conv_state is None
    z, zs = _proj_in(x, w["norm_mix_g"][l], w["w_main"], w["w_small"], l)
    qn, kn, knb, vb, vr = _qknorm(z, w["q_norm_g"][l], w["k_norm_g"][l], heads_a, dk_a, seq,
                                  v_transposed=prompt)
    lam_init = 0.8 - 0.6 * math.exp(-0.3 * l)
    oa = attn_fn(l, qn, knb, vb, lam_init)

    chunk_b = SCAN_CHUNK if seq % SCAN_CHUNK == 0 else SAMPLE_CHUNK
    gates = zs[:, GATE_I:GATE_I + 2 * heads_b].reshape(bsz, seq, 2 * heads_b)
    gt = jnp.swapaxes(gates, 1, 2)
    if seq < chunk_b:
        gt = jnp.pad(gt, ((0, 0), (0, 0), (0, chunk_b - seq)))
    ob, c_f, n_f, m_f = _mlstm(z, zs, gt, w["mlstm_gate_b"][l], w["mlstm_norm_g"][l],
                               mstate[0], mstate[1], mstate[2], seq, chunk_b, (3 * wa) // wb)

    chunk_c = GLA_CHUNK if seq % GLA_CHUNK == 0 else SAMPLE_CHUNK
    kw = heads_c * dk_c
    vw = heads_c * LANES
    c0 = 3 * wa + 4 * wb
    oc, s_f = _gla(z, zs, w["gla_w_alpha"][l], w["gla_b_alpha"][l], w["gla_norm_g"][l], gstate,
                   seq, chunk_c, c0 // kw, c0 // kw + 1, (c0 + 2 * kw) // vw, (c0 + 2 * kw) // vw + 1)

    x1, h2 = _proj_out(oa, ob, oc, w["w_out"], l, x, w["norm_ffn_g"][l])
    if prompt:
        act, tg, tv = _ffn_up_prompt(h2, w["w_up"], l, w["ffn_conv_w"][l], w["ffn_conv_b"][l], bsz, seq)
        conv_rows = jnp.concatenate([tg[:, 6:8], tv[:, 6:8]], axis=-1)
    else:
        act, ug, uv = _ffn_up_sample(h2, w["w_up"], l, w["ffn_conv_w"][l], w["ffn_conv_b"][l], conv_state, seq)
        u = jnp.concatenate([ug, uv], axis=-1).reshape(bsz, seq, -1)
        conv_rows = u[:, seq - 2:]
    x2 = _proj_down(act, w["w_down"], l, x1)

    k_rows = kn.reshape(bsz, seq, heads_a, LANES)
    v_rows = vr.reshape(bsz, seq, heads_a, LANES)
    return x2, (k_rows, v_rows, c_f, n_f, m_f[:, :, 0], s_f, conv_rows)


def kernel(x_prompt, x_sample, cache_k, cache_v, page_table, state_mlstm_C, state_mlstm_n, state_mlstm_m, state_gla_S, state_ffn_conv, norm_mix_g, w_in, q_norm_g, k_norm_g, diff_lambda, diff_subln_g, rel_bias, mlstm_gate_b, mlstm_norm_g, gla_w_alpha, gla_b_alpha, gla_norm_g, w_out, norm_ffn_g, ffn_w_up, ffn_conv_w, ffn_conv_b, ffn_w_down):
    depth = w_in.shape[0]
    bp, sp, d_model = x_prompt.shape
    bs, ss, _ = x_sample.shape
    heads_a, dv_a = cache_v.shape[3], cache_v.shape[4]
    dk_a = cache_k.shape[4] // 2
    heads_b, dk_b, dv_b = state_mlstm_C.shape[2:]
    heads_c, dk_c, dv_c = state_gla_S.shape[2:]
    rank = gla_w_alpha.shape[1]
    page = cache_k.shape[2]
    assert dv_a == LANES and 2 * dk_a == LANES and dk_b == LANES and dv_b == LANES and dv_c == LANES
    assert page >= MAX_DISTANCE and ATTN_TILE >= MAX_DISTANCE and rank <= GATE_I
    wa, wb = heads_a * LANES, heads_b * LANES
    n_main = 3 * wa + 4 * wb + 2 * heads_c * dk_c + 2 * heads_c * dv_c
    gate0 = 3 * wa + 4 * wb
    c0 = gate0 + 2 * heads_b
    assert w_in.shape[2] == n_main + 2 * heads_b + rank

    w_main = jnp.concatenate([w_in[:, :, :gate0], w_in[:, :, c0:c0 + n_main - gate0]], axis=-1).astype(BF16)
    w_small = jnp.zeros((depth, d_model, LANES), F32)
    w_small = w_small.at[:, :, :rank].set(w_in[:, :, n_main + 2 * heads_b:])
    w_small = w_small.at[:, :, GATE_I:GATE_I + 2 * heads_b].set(w_in[:, :, gate0:c0]).astype(BF16)
    w = dict(norm_mix_g=norm_mix_g, w_main=w_main, w_small=w_small, q_norm_g=q_norm_g, k_norm_g=k_norm_g,
             mlstm_gate_b=mlstm_gate_b, mlstm_norm_g=mlstm_norm_g, gla_w_alpha=gla_w_alpha,
             gla_b_alpha=gla_b_alpha, gla_norm_g=gla_norm_g, w_out=w_out.astype(BF16),
             norm_ffn_g=norm_ffn_g, w_up=ffn_w_up, ffn_conv_w=ffn_conv_w,
             ffn_conv_b=ffn_conv_b, w_down=ffn_w_down.astype(BF16))
    dims = (heads_a, dk_a, heads_b, heads_c, dk_c)

    bias_p = _prompt_bias(rel_bias, ATTN_TILE)
    bias_pg, bias_new = _decode_bias(rel_bias, page, DECODE_PAGES_PER_STEP, ss)
    kc = cache_k.reshape(depth, cache_k.shape[1], page * heads_a, LANES)
    vc = cache_v.reshape(depth, cache_v.shape[1], page * heads_a, LANES)

    def prompt_attn(l, qn, knb, vt, lam_init):
        shp = (bp, sp, wa)
        o = _prompt_attention(qn.reshape(shp), knb.reshape(shp), vt, bias_p,
                              diff_lambda[l], diff_subln_g[l], lam_init, dk_a)
        return o.reshape(bp * sp, wa)

    def sample_attn(l, qn, knb, vb, lam_init):
        shp = (bs, ss, wa)
        o = _decode_attention(l, qn.reshape(shp), knb.reshape(shp), vb.reshape(shp), kc, vc, page_table,
                              bias_pg, bias_new, diff_lambda[l], diff_subln_g[l], lam_init, dk_a)
        return o.reshape(bs * ss, wa)

    xp = x_prompt.reshape(bp * sp, d_model)
    xs = x_sample.reshape(bs * ss, d_model)
    zero_m = (jnp.zeros((bp, heads_b, dk_b, dv_b), F32), jnp.zeros((bp, heads_b, dk_b), F32),
              jnp.zeros((bp, heads_b), F32))
    zero_g = jnp.zeros((bp, heads_c, dk_c, dv_c), F32)
    rows_p, rows_s = [], []
    for l in range(depth):
        xp, rp = _layer(xp, l, w, dims, prompt_attn, zero_m, zero_g, None, bp, sp)
        rows_p.append(rp)
        xs, rs = _layer(xs, l, w, dims, sample_attn,
                        (state_mlstm_C[l], state_mlstm_n[l], state_mlstm_m[l]), state_gla_S[l],
                        state_ffn_conv[l], bs, ss)
        rows_s.append(rs)

    def field(rows, i):
        return jnp.stack([r[i] for r in rows], axis=0)

    return (xp.reshape(bp, sp, d_model), xs.reshape(bs, ss, d_model),
            *[field(rows_p, i) for i in range(7)], *[field(rows_s, i) for i in range(7)])
```

```python
import functools
import math

import numpy as np
import jax
import jax.numpy as jnp
from jax import lax
from jax.experimental import pallas as pl
from jax.experimental.pallas import tpu as pltpu

F32 = jnp.float32
BF16 = jnp.bfloat16

LANES = 128
VMEM_LIMIT = 52 * 1024 * 1024
EPS = 1e-6
NEG = -1e30
LOG2E = math.log2(math.e)
GLA_TAU = 16.0
MAX_DISTANCE = 128
ATTN_TILE = 256
SCAN_CHUNK = 256
GLA_CHUNK = 128
SAMPLE_CHUNK = 128
DECODE_PAGES_PER_STEP = 8


def _cparams(*sem):
    return pltpu.CompilerParams(dimension_semantics=sem, vmem_limit_bytes=VMEM_LIMIT)


def _split_bf16(x):
    hi = x.astype(BF16)
    lo = (x - hi.astype(F32)).astype(BF16)
    return hi, lo


def _dot(a, b):
    return jnp.dot(a, b, preferred_element_type=F32)


def _dot_nt(a, b):
    return lax.dot_general(a, b, (((1,), (1,)), ((), ())), preferred_element_type=F32)


def _dot_tn(a, b):
    return lax.dot_general(a, b, (((0,), (0,)), ((), ())), preferred_element_type=F32)


def _log_sigmoid(x):
    return jnp.minimum(x, 0.0) - jnp.log(1.0 + jnp.exp(-jnp.abs(x)))


def _sigmoid(x):
    return 1.0 / (1.0 + jnp.exp(-x))


def _div_pow2(x, n):
    assert n & (n - 1) == 0
    return lax.shift_right_logical(x, n.bit_length() - 1)


def _mod_pow2(x, n):
    assert n & (n - 1) == 0
    return x & (n - 1)


def _mixer_dtype(block_rows):
    return BF16 if block_rows % 16 == 0 else F32


def _pad_rows(x, rows):
    if x.shape[0] == rows:
        return x
    return jnp.concatenate([x, jnp.zeros((rows - x.shape[0], x.shape[1]), x.dtype)], axis=0)


def _proj_down_kernel(a_ref, w_ref, r_ref, o_ref):
    o_ref[...] = r_ref[...] + _dot(a_ref[...], w_ref[...])


def _proj_down(a, w, layer, res):
    m, k = a.shape
    n = w.shape[2]
    tm = min(m, 512)
    tn = 512
    assert m % tm == 0 and n % tn == 0
    return pl.pallas_call(
        _proj_down_kernel,
        out_shape=jax.ShapeDtypeStruct((m, n), F32),
        grid=(m // tm, n // tn),
        in_specs=[pl.BlockSpec((tm, k), lambda i, j: (i, 0)),
                  pl.BlockSpec((None, k, tn), lambda i, j: (layer, 0, j)),
                  pl.BlockSpec((tm, tn), lambda i, j: (i, j))],
        out_specs=pl.BlockSpec((tm, tn), lambda i, j: (i, j)),
        compiler_params=_cparams("parallel", "parallel"),
        name="proj_down",
    )(a, w, res)


def _rms_rows(x, g):
    ms = jnp.mean(x * x, axis=-1, keepdims=True)
    return x * lax.rsqrt(ms + EPS) * g


def _proj_in_kernel(x_ref, g_ref, w_ref, ws_ref, z_ref, zs_ref, h_sc):
    @pl.when(pl.program_id(1) == 0)
    def _norm():
        h_sc[...] = _rms_rows(x_ref[...], g_ref[...]).astype(BF16)
        zs_ref[...] = _dot(h_sc[...], ws_ref[...])

    z_ref[...] = _dot(h_sc[...], w_ref[...])


def _proj_in(x, g, w_main, w_small, layer):
    m, d = x.shape
    n = w_main.shape[2]
    ns = w_small.shape[2]
    tm = min(m, 1024)
    tn = 512
    assert m % tm == 0 and n % tn == 0
    return pl.pallas_call(
        _proj_in_kernel,
        out_shape=(jax.ShapeDtypeStruct((m, n), F32), jax.ShapeDtypeStruct((m, ns), F32)),
        grid=(m // tm, n // tn),
        in_specs=[pl.BlockSpec((tm, d), lambda i, j: (i, 0)),
                  pl.BlockSpec((1, d), lambda i, j: (0, 0)),
                  pl.BlockSpec((None, d, tn), lambda i, j: (layer, 0, j)),
                  pl.BlockSpec((None, d, ns), lambda i, j: (layer, 0, 0))],
        out_specs=(pl.BlockSpec((tm, tn), lambda i, j: (i, j)),
                   pl.BlockSpec((tm, ns), lambda i, j: (i, 0))),
        scratch_shapes=[pltpu.VMEM((tm, d), BF16)],
        compiler_params=_cparams("parallel", "arbitrary"),
        name="proj_in",
    )(x, g.reshape(1, d), w_main, w_small)


def _proj_out_kernel(oa_ref, ob_ref, oc_ref, w_ref, r_ref, g_ref, x_ref, h_ref):
    ka, kb = oa_ref.shape[1], ob_ref.shape[1]
    acc = _dot(oa_ref[...].astype(BF16), w_ref[0:ka, :])
    acc += _dot(ob_ref[...].astype(BF16), w_ref[ka:ka + kb, :])
    acc += _dot(oc_ref[...].astype(BF16), w_ref[ka + kb:, :])
    x = r_ref[...] + acc
    x_ref[...] = x
    h_ref[...] = _rms_rows(x, g_ref[...]).astype(BF16)


def _proj_out(oa, ob, oc, w, layer, res, g):
    m = oa.shape[0]
    k, n = w.shape[1], w.shape[2]
    tm = min(m, 256)
    assert m % tm == 0 and oa.shape[1] + ob.shape[1] + oc.shape[1] == k
    rows = lambda width: pl.BlockSpec((tm, width), lambda i: (i, 0))
    return pl.pallas_call(
        _proj_out_kernel,
        out_shape=(jax.ShapeDtypeStruct((m, n), F32), jax.ShapeDtypeStruct((m, n), BF16)),
        grid=(m // tm,),
        in_specs=[rows(oa.shape[1]), rows(ob.shape[1]), rows(oc.shape[1]),
                  pl.BlockSpec((None, k, n), lambda i: (layer, 0, 0)),
                  rows(n),
                  pl.BlockSpec((1, n), lambda i: (0, 0))],
        out_specs=(rows(n), rows(n)),
        compiler_params=_cparams("parallel"),
        name="proj_out",
    )(oa, ob, oc, w, res, g.reshape(1, n))


def _qknorm_kernel(aq_ref, ak_ref, av_ref, qg_ref, kg_ref, bd_ref,
                   qn_ref, kn_ref, knb_ref, vb_ref, vr_ref, *, heads, inv_dk, scale, v_transposed):
    bd = bd_ref[...]
    qg = qg_ref[...]
    kg = kg_ref[...]

    def norm(x, g):
        hi, lo = _split_bf16(x * x)
        ss = _dot(hi, bd) + _dot(lo, bd)
        return x * lax.rsqrt(ss * inv_dk + EPS) * g

    for h in range(heads):
        sl = slice(LANES * h, LANES * (h + 1))
        qn_ref[:, sl] = (norm(aq_ref[:, sl], qg) * scale).astype(BF16)
        kn = norm(ak_ref[:, sl], kg)
        kn_ref[:, sl] = kn
        knb_ref[:, sl] = kn.astype(BF16)
        if v_transposed:
            vb_ref[0, sl, :] = av_ref[:, sl].T.astype(BF16)
    if not v_transposed:
        vb_ref[...] = av_ref[...].astype(BF16)
    vr_ref[...] = av_ref[...]


def _qknorm(z, qg, kg, heads, dk, seq, v_transposed):
    m = z.shape[0]
    w = heads * LANES
    tm = min(m, 256)
    assert m % tm == 0
    lane = np.arange(LANES)
    bd = jnp.asarray((lane[:, None] // dk) == (lane[None, :] // dk), BF16)
    reps = LANES // dk
    kern = functools.partial(_qknorm_kernel, heads=heads, inv_dk=1.0 / dk, scale=dk ** -0.5 * LOG2E,
                             v_transposed=v_transposed)
    rows = pl.BlockSpec((tm, w), lambda i: (i, 0))
    if v_transposed:
        tps = seq // tm
        v_shape = jax.ShapeDtypeStruct((m // seq, w, seq), BF16)
        v_spec = pl.BlockSpec((1, w, tm), lambda i: (i // tps, 0, i % tps))
    else:
        v_shape = jax.ShapeDtypeStruct((m, w), BF16)
        v_spec = rows
    return pl.pallas_call(
        kern,
        out_shape=(jax.ShapeDtypeStruct((m, w), BF16), jax.ShapeDtypeStruct((m, w), F32),
                   jax.ShapeDtypeStruct((m, w), BF16), v_shape, jax.ShapeDtypeStruct((m, w), F32)),
        grid=(m // tm,),
        in_specs=[pl.BlockSpec((tm, w), lambda i: (i, 0)),
                  pl.BlockSpec((tm, w), lambda i: (i, 1)),
                  pl.BlockSpec((tm, w), lambda i: (i, 2)),
                  pl.BlockSpec((1, LANES), lambda i: (0, 0)),
                  pl.BlockSpec((1, LANES), lambda i: (0, 0)),
                  pl.BlockSpec((LANES, LANES), lambda i: (0, 0))],
        out_specs=(rows, rows, rows, v_spec, rows),
        compiler_params=_cparams("parallel"),
        name="qknorm",
    )(z, z, z, jnp.tile(qg, reps).reshape(1, LANES), jnp.tile(kg, reps).reshape(1, LANES), bd)


def _bucket(n, n_buckets):
    max_exact = n_buckets // 2
    nf = jnp.maximum(n, max_exact).astype(F32)
    large = max_exact + (jnp.log(nf / max_exact) / math.log(MAX_DISTANCE / max_exact)
                         * (n_buckets - max_exact)).astype(jnp.int32)
    large = jnp.minimum(large, n_buckets - 1)
    return jnp.where(n < max_exact, jnp.maximum(n, 0), large)


def _prompt_bias_kernel(rb_ref, o_ref, *, tile, n_buckets):
    h = pl.program_id(0)
    j = lax.broadcasted_iota(jnp.int32, (tile, tile), 0)
    i = lax.broadcasted_iota(jnp.int32, (tile, tile), 1)
    far = jnp.full((tile, tile), MAX_DISTANCE, jnp.int32)
    for t, n in enumerate((i - j, tile + i - j, far)):
        b = _bucket(n, n_buckets)
        val = jnp.full((tile, tile), rb_ref[0, h], F32)
        for k in range(1, n_buckets):
            val = jnp.where(b == k, rb_ref[k, h], val)
        o_ref[0, t] = jnp.where(n >= 0, val * LOG2E, NEG)


def _prompt_bias(rel_bias, tile):
    n_buckets, heads = rel_bias.shape
    kern = functools.partial(_prompt_bias_kernel, tile=tile, n_buckets=n_buckets)
    return pl.pallas_call(
        kern,
        out_shape=jax.ShapeDtypeStruct((heads, 3, tile, tile), F32),
        grid=(heads,),
        in_specs=[pl.BlockSpec(memory_space=pltpu.SMEM)],
        out_specs=pl.BlockSpec((1, 3, tile, tile), lambda h: (h, 0, 0, 0)),
        compiler_params=_cparams("parallel"),
        name="prompt_bias",
    )(rel_bias)


def _decode_bias_kernel(rb_ref, pg_ref, new_ref, *, page, ppb, dec_seq, heads, n_buckets):
    grp = 2 * dec_seq

    def lookup(n, h):
        b = _bucket(n, n_buckets)
        val = jnp.full(n.shape, rb_ref[0, h], F32)
        for k in range(1, n_buckets):
            val = jnp.where(b == k, rb_ref[k, h], val)
        return jnp.where(n >= 0, val * LOG2E, NEG)

    qi = _mod_pow2(lax.broadcasted_iota(jnp.int32, (grp, page), 0), dec_seq)
    j = lax.broadcasted_iota(jnp.int32, (grp, page), 1)
    far = jnp.full((grp, page), MAX_DISTANCE, jnp.int32)
    for h in range(heads):
        rs = slice(grp * h, grp * (h + 1))
        far_b = lookup(far, h)
        for c in range(ppb):
            pg_ref[0, rs, page * c:page * (c + 1)] = far_b
            pg_ref[1, rs, page * c:page * (c + 1)] = far_b if c < ppb - 1 else lookup(page + qi - j, h)
        new_ref[rs, :] = lookup(jnp.where(j < dec_seq, qi - j, -1), h)


def _decode_bias(rel_bias, page, ppb, dec_seq):
    n_buckets, heads = rel_bias.shape
    rows = 2 * dec_seq * heads
    kern = functools.partial(_decode_bias_kernel, page=page, ppb=ppb, dec_seq=dec_seq, heads=heads,
                             n_buckets=n_buckets)
    return pl.pallas_call(
        kern,
        out_shape=(jax.ShapeDtypeStruct((2, rows, ppb * page), F32),
                   jax.ShapeDtypeStruct((rows, page), F32)),
        in_specs=[pl.BlockSpec(memory_space=pltpu.SMEM)],
        name="decode_bias",
    )(rel_bias)


def _diff_lambda(dl, lam_init):
    s1 = jnp.sum(dl[0:1] * dl[1:2], axis=-1, keepdims=True)
    s2 = jnp.sum(dl[2:3] * dl[3:4], axis=-1, keepdims=True)
    return jnp.exp(s1) - jnp.exp(s2) + lam_init


def _subln(o, g, lam_init):
    ms = jnp.mean(o * o, axis=-1, keepdims=True)
    return o * lax.rsqrt(ms + EPS) * g * (1.0 - lam_init)


def _flash_update(s, m_prev, l_prev):
    m_new = jnp.maximum(m_prev, jnp.max(s, axis=-1, keepdims=True))
    alpha = jnp.exp2(m_prev - m_new)
    p = jnp.exp2(s - m_new)
    l_new = alpha * l_prev + jnp.sum(p, axis=-1, keepdims=True)
    return p, alpha, m_new, l_new


def _attn_kernel(qt_ref, kt_ref, q_ref, k_ref, vt_ref, bias_ref, dl_ref, g_ref, o_ref, m_sc, l_sc, acc_sc,
                 *, heads, tile, dk, lam_init):
    qi = qt_ref[pl.program_id(1)]
    ki = kt_ref[pl.program_id(1)]
    n_sub = tile // LANES

    @pl.when(ki == 0)
    def _init():
        m_sc[...] = jnp.full(m_sc.shape, -jnp.inf, F32)
        l_sc[...] = jnp.zeros(l_sc.shape, F32)
        acc_sc[...] = jnp.zeros(acc_sc.shape, F32)

    lane = lax.broadcasted_iota(jnp.int32, (LANES, LANES), 1)
    for h in range(heads):
        sl = slice(LANES * h, LANES * (h + 1))
        kh = k_ref[0, :, sl]
        vht = vt_ref[0, sl, :]
        for c in range(2 * n_sub):
            qrows = slice(LANES * (c % n_sub), LANES * (c % n_sub + 1))
            cs = slice(LANES * c, LANES * (c + 1))
            qc = q_ref[0, qrows, sl]
            own_map = (lane < dk) if c < n_sub else (lane >= dk)
            qc = jnp.where(own_map, qc, jnp.zeros_like(qc))
            s = _dot_nt(kh, qc) + bias_ref[h, 0, :, qrows]
            m_prev = m_sc[h, :, cs]
            m_new = jnp.maximum(m_prev, jnp.max(s, axis=0, keepdims=True))
            alpha = jnp.exp2(m_prev - m_new)
            p = jnp.exp2(s - m_new)
            l_sc[h, :, cs] = alpha * l_sc[h, :, cs] + jnp.sum(p, axis=0, keepdims=True)
            acc_sc[h, :, cs] = alpha * acc_sc[h, :, cs] + _dot(vht, p.astype(BF16))
            m_sc[h, :, cs] = m_new

    @pl.when(ki == qi)
    def _finish():
        lam = _diff_lambda(dl_ref[...], lam_init)
        g_col = g_ref[...]
        for h in range(heads):
            acc = acc_sc[h]
            l = l_sc[h]
            ot = acc[:, :tile] / l[:, :tile] - lam * (acc[:, tile:] / l[:, tile:])
            ms = jnp.mean(ot * ot, axis=0, keepdims=True)
            ot = ot * lax.rsqrt(ms + EPS) * g_col * (1.0 - lam_init)
            o_ref[0, :, LANES * h:LANES * (h + 1)] = ot.T.astype(o_ref.dtype)


def _prompt_attention(qn, knb, vt, bias, dl, g, lam_init, dk):
    bsz, seq, w = qn.shape
    heads = w // LANES
    tile = bias.shape[-1]
    nq = seq // tile
    assert seq % tile == 0 and tile % LANES == 0
    kern = functools.partial(_attn_kernel, heads=heads, tile=tile, dk=dk, lam_init=lam_init)
    pairs = [(qi, ki) for qi in range(nq) for ki in range(qi + 1)]
    q_tab = jnp.asarray([p[0] for p in pairs], jnp.int32)
    k_tab = jnp.asarray([p[1] for p in pairs], jnp.int32)

    def bias_idx(b, t, qt, kt):
        return (0, jnp.where(kt[t] == qt[t], 0, jnp.where(kt[t] == qt[t] - 1, 1, 2)), 0, 0)

    grid_spec = pltpu.PrefetchScalarGridSpec(
        num_scalar_prefetch=2,
        grid=(bsz, len(pairs)),
        in_specs=[pl.BlockSpec((1, tile, w), lambda b, t, qt, kt: (b, qt[t], 0)),
                  pl.BlockSpec((1, tile, w), lambda b, t, qt, kt: (b, kt[t], 0)),
                  pl.BlockSpec((1, w, tile), lambda b, t, qt, kt: (b, 0, kt[t])),
                  pl.BlockSpec((heads, 1, tile, tile), bias_idx),
                  pl.BlockSpec(dl.shape, lambda b, t, qt, kt: (0, 0)),
                  pl.BlockSpec((LANES, 1), lambda b, t, qt, kt: (0, 0))],
        out_specs=pl.BlockSpec((1, tile, w), lambda b, t, qt, kt: (b, qt[t], 0)),
        scratch_shapes=[pltpu.VMEM((heads, 1, 2 * tile), F32),
                        pltpu.VMEM((heads, 1, 2 * tile), F32),
                        pltpu.VMEM((heads, LANES, 2 * tile), F32)],
    )
    return pl.pallas_call(
        kern,
        out_shape=jax.ShapeDtypeStruct((bsz, seq, w), BF16),
        grid_spec=grid_spec,
        compiler_params=_cparams("parallel", "arbitrary"),
        name="prompt_attention",
    )(q_tab, k_tab, qn, knb, vt, bias, dl, g.reshape(LANES, 1))


def _decode_attn_kernel(pt_ref, q_ref, kn_ref, vn_ref, *refs, heads, dec_seq, page, ppb, dk, lam_init):
    del pt_ref
    kc_refs, vc_refs = refs[:ppb], refs[ppb:2 * ppb]
    bias_ref, bnew_ref, dl_ref, g_ref, o_ref, wq_sc, wqb_sc, m_sc, l_sc, acc_sc = refs[2 * ppb:]
    p_idx = pl.program_id(1)
    rows = 2 * dec_seq * heads
    grp = 2 * dec_seq
    width = heads * LANES

    @pl.when(p_idx == 0)
    def _init():
        q = q_ref[0].astype(F32)
        lane = lax.broadcasted_iota(jnp.int32, (dec_seq, LANES), 1)
        pieces = []
        for h in range(heads):
            qh = q[:, LANES * h:LANES * (h + 1)]
            pieces += [jnp.where(lane < dk, qh, 0.0), jnp.where(lane >= dk, qh, 0.0)]
        wq = jnp.concatenate(pieces, axis=0)
        wq_sc[...] = wq.astype(BF16)
        c = lax.broadcasted_iota(jnp.int32, (rows, width), 1)
        r = lax.broadcasted_iota(jnp.int32, (rows, width), 0)
        own = _div_pow2(r, grp) == _div_pow2(c, LANES)
        wqb_sc[...] = jnp.where(own, jnp.concatenate([wq] * heads, axis=1), 0.0).astype(BF16)
        m_sc[...] = jnp.full(m_sc.shape, -jnp.inf, F32)
        l_sc[...] = jnp.zeros(l_sc.shape, F32)
        acc_sc[...] = jnp.zeros(acc_sc.shape, F32)

    def head_rows(ref, h):
        return ref[pl.ds(h, page, stride=heads), :].astype(BF16)

    s = jnp.concatenate(
        [jnp.concatenate([_dot_nt(wq_sc[grp * h:grp * (h + 1), :], head_rows(kc, h)) for h in range(heads)],
                         axis=0) for kc in kc_refs], axis=1) + bias_ref[0]
    p, alpha, m_new, l_new = _flash_update(s, m_sc[...], l_sc[...])
    m_sc[...] = m_new
    l_sc[...] = l_new
    pb = p.astype(BF16)
    for h in range(heads):
        rs = slice(grp * h, grp * (h + 1))
        acc = alpha[rs] * acc_sc[rs, :]
        for j, vc in enumerate(vc_refs):
            acc += _dot(pb[rs, page * j:page * (j + 1)], head_rows(vc, h))
        acc_sc[rs, :] = acc

    @pl.when(p_idx == pl.num_programs(1) - 1)
    def _finish():
        kn = _pad_rows(kn_ref[0].astype(F32), page).astype(BF16)
        vn = _pad_rows(vn_ref[0].astype(F32), page)
        s = _dot_nt(wqb_sc[...], kn) + bnew_ref[...]
        p, alpha, _, l_fin = _flash_update(s, m_sc[...], l_sc[...])
        pb = p.astype(BF16)
        lam = _diff_lambda(dl_ref[...], lam_init)
        g = g_ref[...]
        for h in range(heads):
            rs = slice(grp * h, grp * (h + 1))
            vh = vn[:, LANES * h:LANES * (h + 1)].astype(BF16)
            acc = alpha[rs] * acc_sc[rs, :] + _dot(pb[rs], vh)
            l = l_fin[rs]
            o = acc[:dec_seq] / l[:dec_seq] - lam * (acc[dec_seq:] / l[dec_seq:])
            o_ref[0, :, LANES * h:LANES * (h + 1)] = _subln(o, g, lam_init)


def _decode_attention(layer, qn, knb, vb, cache_k, cache_v, page_table, bias_pg, bias_new, dl, g, lam_init, dk):
    bsz, dec_seq, w = qn.shape
    heads = w // LANES
    page = cache_k.shape[2] // heads
    ppb = bias_pg.shape[-1] // page
    n_steps = page_table.shape[1] // ppb
    assert page_table.shape[1] % ppb == 0
    rows = 2 * dec_seq * heads
    kern = functools.partial(_decode_attn_kernel, heads=heads, dec_seq=dec_seq, page=page, ppb=ppb, dk=dk,
                             lam_init=lam_init)
    new_spec = pl.BlockSpec((1, dec_seq, w), lambda b, p, pt: (b, 0, 0))
    cache_specs = [pl.BlockSpec((None, None, page * heads, LANES),
                                lambda b, p, pt, j=j: (layer, pt[b, p * ppb + j], 0, 0)) for j in range(ppb)]
    grid_spec = pltpu.PrefetchScalarGridSpec(
        num_scalar_prefetch=1,
        grid=(bsz, n_steps),
        in_specs=[new_spec, new_spec, new_spec, *cache_specs, *cache_specs,
                  pl.BlockSpec((1, rows, ppb * page),
                               lambda b, p, pt: (jnp.where(p == n_steps - 1, 1, 0), 0, 0)),
                  pl.BlockSpec((rows, page), lambda b, p, pt: (0, 0)),
                  pl.BlockSpec(dl.shape, lambda b, p, pt: (0, 0)),
                  pl.BlockSpec((1, LANES), lambda b, p, pt: (0, 0))],
        out_specs=pl.BlockSpec((1, dec_seq, w), lambda b, p, pt: (b, 0, 0)),
        scratch_shapes=[pltpu.VMEM((rows, LANES), BF16),
                        pltpu.VMEM((rows, w), BF16),
                        pltpu.VMEM((rows, 1), F32),
                        pltpu.VMEM((rows, 1), F32),
                        pltpu.VMEM((rows, LANES), F32)],
    )
    return pl.pallas_call(
        kern,
        out_shape=jax.ShapeDtypeStruct((bsz, dec_seq, w), F32),
        grid_spec=grid_spec,
        compiler_params=_cparams("parallel", "arbitrary"),
        name="decode_attention",
    )(page_table, qn, knb, vb, *([cache_k] * ppb), *([cache_v] * ppb), bias_pg, bias_new, dl,
      g.reshape(1, LANES))


GATE_I = 16
GATE_F = 20


def _mlstm_kernel(q_ref, k_ref, v_ref, og_ref, zs_ref, gt_ref, gbl_ref, gbc_ref, g_ref,
                  c0_ref, n0_ref, m0_ref,
                  o_ref, cf_ref, nf_ref, mf_ref, c_sc, n_sc, m_sc,
                  *, heads, chunk, valid, scale):
    c_idx = pl.program_id(1)

    @pl.when(c_idx == 0)
    def _init():
        for h in range(heads):
            c_sc[h] = c0_ref[0, h].T
        n_sc[...] = n0_ref[0]
        m_sc[...] = m0_ref[0]

    rows_in = q_ref.shape[0]
    row = lax.broadcasted_iota(jnp.int32, (chunk, 1), 0)
    s_i = lax.broadcasted_iota(jnp.int32, (chunk, chunk), 0)
    t_i = lax.broadcasted_iota(jnp.int32, (chunk, chunk), 1)
    causal = s_i <= t_i
    incl = causal.astype(BF16)
    incl_t = (s_i >= t_i).astype(BF16)

    gcol = _pad_rows(zs_ref[...], chunk) + gbl_ref[...]
    lf_mat = _log_sigmoid(gcol)
    if valid < chunk:
        lf_mat = jnp.where(row < valid, lf_mat, 0.0)
    hi, lo = _split_bf16(lf_mat)
    b_mat = _dot(incl_t, hi) + _dot(incl_t, lo)

    grow = gt_ref[0] + gbc_ref[...]
    col = lax.broadcasted_iota(jnp.int32, grow.shape, 1)
    grow_id = lax.broadcasted_iota(jnp.int32, grow.shape, 0)
    lf_rows = jnp.where(grow_id >= heads, _log_sigmoid(grow), 0.0)
    if valid < chunk:
        lf_rows = jnp.where(col < valid, lf_rows, 0.0)
    hi, lo = _split_bf16(lf_rows)
    b_rows = _dot(hi, incl) + _dot(lo, incl)

    g = g_ref[...]
    first_row = lax.broadcasted_iota(jnp.int32, (16, 1), 0) == 0
    for h in range(heads):
        sl = slice(LANES * h, LANES * (h + 1))
        li_col = gcol[:, GATE_I + h:GATE_I + h + 1]
        if valid < chunk:
            li_col = jnp.where(row < valid, li_col, NEG)
        u_col = li_col - b_mat[:, GATE_F + h:GATE_F + h + 1]
        b_row = b_rows[heads + h:heads + h + 1, :]
        dt = jnp.where(causal, b_row + u_col, NEG)
        m_prev = m_sc[h:h + 1, 0:1]
        inter = b_row + m_prev
        mt = jnp.maximum(inter, jnp.max(dt, axis=0, keepdims=True))
        wi = jnp.exp(inter - mt)
        q = _pad_rows(q_ref[:, sl], chunk)
        ks = _pad_rows(k_ref[:, sl], chunk) * scale
        vtb = _pad_rows(v_ref[:, sl], chunk).T.astype(BF16)
        qb = q.astype(BF16)
        pt = _dot_nt(ks.astype(BF16), qb) * jnp.exp(dt - mt)
        ct_prev = c_sc[h]
        n_prev = n_sc[h:h + 1, :]
        num = wi * _dot_nt(ct_prev.astype(BF16), qb) + _dot(vtb, pt.astype(BF16))
        n_rows = jnp.where(first_row, jnp.broadcast_to(n_prev, (16, LANES)), 0.0)
        qn = _dot_nt(n_rows.astype(BF16), qb)[0:1]
        den = wi * qn + jnp.sum(pt, axis=0, keepdims=True)
        ht = num / jnp.maximum(jnp.abs(den), jnp.exp(-mt))
        og = _sigmoid(og_ref[:, sl])
        y = og * ht.T[:rows_in]
        ms = jnp.mean(y * y, axis=-1, keepdims=True)
        o_ref[:, sl] = (y * lax.rsqrt(ms + EPS) * g).astype(o_ref.dtype)

        m_new = mt[:, chunk - 1:chunk]
        b_last = b_row[:, chunk - 1:chunk]
        a = jnp.exp(b_last + m_prev - m_new)
        kw = ks * jnp.exp(u_col + (b_last - m_new))
        c_sc[h] = a * ct_prev + _dot(vtb, kw.astype(BF16))
        n_sc[h:h + 1, :] = a * n_prev + jnp.sum(kw, axis=0, keepdims=True)
        m_sc[h:h + 1, :] = jnp.broadcast_to(m_new, (1, LANES))

    @pl.when(c_idx == pl.num_programs(1) - 1)
    def _finish():
        for h in range(heads):
            cf_ref[0, h] = c_sc[h].T
        nf_ref[0] = n_sc[...]
        mf_ref[0] = m_sc[...]


def _mlstm(z, zs, gt, gate_b, g, c0, n0, m0, seq, chunk, col0):
    bsz, heads = c0.shape[0], c0.shape[1]
    w = heads * LANES
    rows_in = min(seq, chunk)
    nc = seq // rows_in
    valid = rows_in
    gt_w = gt.shape[-1] // nc
    kern = functools.partial(_mlstm_kernel, heads=heads, chunk=chunk, valid=valid, scale=LANES ** -0.5)
    gbl = jnp.zeros((1, LANES), F32)
    gbl = gbl.at[0, GATE_I:GATE_I + heads].set(gate_b[0]).at[0, GATE_F:GATE_F + heads].set(gate_b[1])
    gbc = gate_b.reshape(2 * heads, 1)
    m0b = jnp.broadcast_to(m0[:, :, None], (bsz, heads, LANES))

    def zspec(blk):
        return pl.BlockSpec((rows_in, w), lambda b, c: (b * nc + c, col0 + blk))

    state = lambda shape: pl.BlockSpec((1,) + shape, lambda b, c: (b,) + (0,) * len(shape))
    return pl.pallas_call(
        kern,
        out_shape=(jax.ShapeDtypeStruct((bsz * seq, w), _mixer_dtype(rows_in)),
                   jax.ShapeDtypeStruct((bsz, heads, LANES, LANES), F32),
                   jax.ShapeDtypeStruct((bsz, heads, LANES), F32),
                   jax.ShapeDtypeStruct((bsz, heads, LANES), F32)),
        grid=(bsz, nc),
        in_specs=[zspec(0), zspec(1), zspec(2), zspec(3),
                  pl.BlockSpec((rows_in, LANES), lambda b, c: (b * nc + c, 0)),
                  pl.BlockSpec((1, 2 * heads, gt_w), lambda b, c: (b, 0, c)),
                  pl.BlockSpec((1, LANES), lambda b, c: (0, 0)),
                  pl.BlockSpec((2 * heads, 1), lambda b, c: (0, 0)),
                  pl.BlockSpec((1, LANES), lambda b, c: (0, 0)),
                  state((heads, LANES, LANES)), state((heads, LANES)), state((heads, LANES))],
        out_specs=(pl.BlockSpec((rows_in, w), lambda b, c: (b * nc + c, 0)),
                   state((heads, LANES, LANES)), state((heads, LANES)), state((heads, LANES))),
        scratch_shapes=[pltpu.VMEM((heads, LANES, LANES), F32),
                        pltpu.VMEM((heads, LANES), F32),
                        pltpu.VMEM((heads, LANES), F32)],
        compiler_params=_cparams("parallel", "arbitrary"),
        name="mlstm",
    )(z, z, z, z, zs, gt, gbl, gbc, g.reshape(1, LANES), c0, n0, m0b)


def _gla_levels(chunk):
    n, out = chunk, []
    while n >= 2:
        out.append(n)
        n //= 2
    return out


def _gla_weights(chunk):
    t = np.arange(chunk)[:, None]
    s = np.arange(chunk)[None, :]
    blocks = [(s <= t).astype(np.float32), (s > t).astype(np.float32)]
    for n in _gla_levels(chunk):
        mid = (t // n) * n + n // 2 - 1
        blocks.append(((s > mid) & (s <= t)).astype(np.float32) - ((s > t) & (s <= mid)).astype(np.float32))
    return jnp.asarray(np.concatenate(blocks, axis=0), BF16)


def _gla_kernel(q_ref, k_ref, v_ref, gg_ref, zs_ref, wa_ref, ba_ref, ws_ref, g_ref, s0_ref,
                o_ref, sf_ref, s_sc, *, heads, chunk, valid, dk, scale):
    c_idx = pl.program_id(1)
    per = LANES // dk
    assert per == 2 and heads % per == 0

    @pl.when(c_idx == 0)
    def _init():
        for h in range(heads):
            s_sc[h // per, dk * (h % per):dk * (h % per + 1), :] = s0_ref[0, h]

    rows_in = q_ref.shape[0]
    row = lax.broadcasted_iota(jnp.int32, (chunk, 1), 0)
    t_i = lax.broadcasted_iota(jnp.int32, (per * chunk, chunk), 0) & (chunk - 1)
    s_i = lax.broadcasted_iota(jnp.int32, (per * chunk, chunk), 1)
    first_head = lax.broadcasted_iota(jnp.int32, (chunk, LANES), 1) < dk

    zs = _pad_rows(zs_ref[...], chunk)
    la = _log_sigmoid(_dot(zs.astype(BF16), wa_ref[...]) + ba_ref[...]) * (1.0 / GLA_TAU)
    if valid < chunk:
        la = jnp.where(row < valid, la, 0.0)
    hi, lo = _split_bf16(la)
    wst = ws_ref[...]
    e_all = _dot(wst, hi) + _dot(wst, lo)
    ones = jnp.ones((chunk, LANES), BF16)
    levels = _gla_levels(chunk)
    g = g_ref[...]
    second_half = [(row & (n - 1)) >= n // 2 for n in levels]
    same_node = [_div_pow2(t_i, n) == _div_pow2(s_i, n) for n in levels]

    def split_heads(x):
        return [jnp.where(first_head, x, 0.0), jnp.where(first_head, 0.0, x)]

    for j in range(heads // per):
        psl = slice(LANES * j, LANES * (j + 1))
        q = _pad_rows(q_ref[:, psl], chunk) * scale
        k = _pad_rows(k_ref[:, psl], chunk)
        bc = e_all[0:chunk, psl]
        rem = e_all[chunk:2 * chunk, psl]
        a_pair = jnp.zeros((per * chunk, chunk), F32)
        for li in range(len(levels)):
            e = e_all[(2 + li) * chunk:(3 + li) * chunk, psl]
            qt = jnp.where(second_half[li], q * jnp.exp(jnp.minimum(e, 0.0)), 0.0)
            kt = jnp.where(second_half[li], 0.0, k * jnp.exp(jnp.minimum(-e, 0.0)))
            qs = jnp.concatenate(split_heads(qt), axis=0).astype(BF16)
            a_pair = a_pair + jnp.where(same_node[li], _dot_nt(qs, kt.astype(BF16)), 0.0)
        s_prev = s_sc[j]
        s_prev_b = s_prev.astype(BF16)
        qd = split_heads(q * jnp.exp(bc))
        qk = split_heads(q * k)
        kd = split_heads(k * jnp.exp(rem))
        hi_p, lo_p = _split_bf16(la[:, psl])
        tot = _dot_tn(hi_p, ones) + _dot_tn(lo_p, ones)
        s_new = jnp.exp(tot) * s_prev
        for hh in range(per):
            h = per * j + hh
            vsl = slice(LANES * h, LANES * (h + 1))
            vb = _pad_rows(v_ref[:, vsl], chunk).astype(BF16)
            a_h = a_pair[chunk * hh:chunk * (hh + 1)]
            o = _dot(qd[hh].astype(BF16), s_prev_b) + _dot(a_h.astype(BF16), vb)
            o = o + jnp.sum(qk[hh], axis=-1, keepdims=True) * vb.astype(F32)
            o = o[:rows_in]
            ms = jnp.mean(o * o, axis=-1, keepdims=True)
            gate = gg_ref[:, vsl]
            o_ref[:, vsl] = (o * lax.rsqrt(ms + EPS) * g * (gate * _sigmoid(gate))).astype(o_ref.dtype)
            s_new = s_new + _dot_tn(kd[hh].astype(BF16), vb)
        s_sc[j] = s_new

    @pl.when(c_idx == pl.num_programs(1) - 1)
    def _finish():
        for h in range(heads):
            sf_ref[0, h] = s_sc[h // per, dk * (h % per):dk * (h % per + 1), :]


def _gla(z, zs, w_alpha, b_alpha, g, s0, seq, chunk, qcol, kcol, vcol, gcol):
    bsz, heads, dk, dv = s0.shape
    rows_in = min(seq, chunk)
    nc = seq // rows_in
    kw = heads * dk
    vw = heads * dv
    wa = jnp.zeros((LANES, kw), F32).at[:w_alpha.shape[0]].set(w_alpha).astype(BF16)
    wst = _gla_weights(chunk)
    kern = functools.partial(_gla_kernel, heads=heads, chunk=chunk, valid=rows_in, dk=dk, scale=dk ** -0.5)
    const = lambda shape: pl.BlockSpec(shape, lambda b, c: (0,) * len(shape))
    return pl.pallas_call(
        kern,
        out_shape=(jax.ShapeDtypeStruct((bsz * seq, vw), _mixer_dtype(rows_in)),
                   jax.ShapeDtypeStruct((bsz, heads, dk, dv), F32)),
        grid=(bsz, nc),
        in_specs=[pl.BlockSpec((rows_in, kw), lambda b, c: (b * nc + c, qcol)),
                  pl.BlockSpec((rows_in, kw), lambda b, c: (b * nc + c, kcol)),
                  pl.BlockSpec((rows_in, vw), lambda b, c: (b * nc + c, vcol)),
                  pl.BlockSpec((rows_in, vw), lambda b, c: (b * nc + c, gcol)),
                  pl.BlockSpec((rows_in, LANES), lambda b, c: (b * nc + c, 0)),
                  const((LANES, kw)), const((1, kw)), const(wst.shape), const((1, LANES)),
                  pl.BlockSpec((1, heads, dk, dv), lambda b, c: (b, 0, 0, 0))],
        out_specs=(pl.BlockSpec((rows_in, vw), lambda b, c: (b * nc + c, 0)),
                   pl.BlockSpec((1, heads, dk, dv), lambda b, c: (b, 0, 0, 0))),
        scratch_shapes=[pltpu.VMEM((kw // LANES, LANES, dv), F32)],
        compiler_params=_cparams("parallel", "arbitrary"),
        name="gla",
    )(z, z, z, z, zs, wa, b_alpha.reshape(1, kw), wst, g.reshape(1, LANES), s0)


def _conv_gate(ug, uv, cwg, cwv, cbg, cbv, prev):
    def conv(u, cw, cb, which):
        u1, u2 = prev(u, which)
        return cb + cw[0:1] * u2 + cw[1:2] * u1 + cw[2:3] * u
    cg = conv(ug, cwg, cbg, 0)
    cv = conv(uv, cwv, cbv, 1)
    return cg * _sigmoid(cg) * cv


FFN_SUB = 256


def _ffn_up_prompt_kernel(h_ref, wg_ref, wv_ref, cwg_ref, cwv_ref, cbg_ref, cbv_ref,
                          act_ref, tg_ref, tv_ref, u_sc, w_sc, *, tiles_per_seq):
    i = pl.program_id(1)
    first = (i % tiles_per_seq) == 0
    hb = h_ref[...]
    tm = hb.shape[0]

    @pl.when(i == 0)
    def _cast_weights():
        w_sc[0] = wg_ref[...].astype(BF16)
        w_sc[1] = wv_ref[...].astype(BF16)

    @pl.when(first)
    def _reset():
        u_sc[:, 0:8, :] = jnp.zeros((2, 8, u_sc.shape[2]), F32)

    def conv_half(which, u, cw_ref, cb_ref, t_ref, cs):
        cw = cw_ref[:, cs]
        u_sc[which, 8:8 + tm, cs] = u
        u1 = u_sc[which, 7:7 + tm, cs]
        u2 = u_sc[which, 6:6 + tm, cs]
        conv = cb_ref[:, cs] + cw[0:1] * u2 + cw[1:2] * u1 + cw[2:3] * u
        u_sc[which, 0:8, cs] = u[tm - 8:tm]
        t_ref[0, :, cs] = u[tm - 8:tm]
        return conv

    subs = [slice(c0, c0 + FFN_SUB) for c0 in range(0, wg_ref.shape[1], FFN_SUB)]
    for cs in subs:
        cg = conv_half(0, _dot(hb, w_sc[0, :, cs]), cwg_ref, cbg_ref, tg_ref, cs)
        cv = conv_half(1, _dot(hb, w_sc[1, :, cs]), cwv_ref, cbv_ref, tv_ref, cs)
        act_ref[:, cs] = (cg * _sigmoid(cg) * cv).astype(BF16)


def _ffn_up_prompt(h2, w_up, layer, conv_w, conv_b, bsz, seq):
    m, d = h2.shape
    dff = w_up.shape[2] // 2
    tm, tn = 1024, 512
    assert seq % tm == 0 and dff % tn == 0 and tn % FFN_SUB == 0
    nj = dff // tn
    tps = seq // tm
    kern = functools.partial(_ffn_up_prompt_kernel, tiles_per_seq=tps)
    wspec = lambda off: pl.BlockSpec((None, d, tn), lambda j, i: (layer, 0, j + off))
    cspec = lambda r, off: pl.BlockSpec((r, tn), lambda j, i: (0, j + off))
    tail = pl.BlockSpec((1, 8, tn), lambda j, i: (i // tps, 0, j))
    cb = conv_b.reshape(1, -1)
    return pl.pallas_call(
        kern,
        out_shape=(jax.ShapeDtypeStruct((m, dff), BF16),
                   jax.ShapeDtypeStruct((bsz, 8, dff), F32),
                   jax.ShapeDtypeStruct((bsz, 8, dff), F32)),
        grid=(nj, m // tm),
        in_specs=[pl.BlockSpec((tm, d), lambda j, i: (i, 0)),
                  wspec(0), wspec(nj), cspec(3, 0), cspec(3, nj), cspec(1, 0), cspec(1, nj)],
        out_specs=(pl.BlockSpec((tm, tn), lambda j, i: (i, j)), tail, tail),
        scratch_shapes=[pltpu.VMEM((2, tm + 8, tn), F32), pltpu.VMEM((2, d, tn), BF16)],
        compiler_params=_cparams("parallel", "arbitrary"),
        name="ffn_up_prompt",
    )(h2, w_up, w_up, conv_w, conv_w, cb, cb)


def _ffn_up_sample_kernel(h_ref, wg_ref, wv_ref, cwg_ref, cwv_ref, cbg_ref, cbv_ref,
                          p1g_ref, p1v_ref, p2g_ref, p2v_ref, act_ref, ug_ref, uv_ref, *, seq):
    hb = h_ref[...]
    ug = _dot(hb, wg_ref[...].astype(BF16))
    uv = _dot(hb, wv_ref[...].astype(BF16))
    tm = ug.shape[0]
    pos = _mod_pow2(lax.broadcasted_iota(jnp.int32, (tm, 1), 0), seq)
    p1 = (p1g_ref, p1v_ref)
    p2 = (p2g_ref, p2v_ref)

    def prev(u, which):
        u1 = jnp.where(pos >= 1, pltpu.roll(u, 1, 0), p1[which][...])
        u2 = jnp.where(pos >= 2, pltpu.roll(u, 2, 0), p2[which][...])
        return u1, u2

    act_ref[...] = _conv_gate(ug, uv, cwg_ref[...], cwv_ref[...], cbg_ref[...], cbv_ref[...], prev).astype(BF16)
    ug_ref[...] = ug
    uv_ref[...] = uv


def _ffn_up_sample(h2, w_up, layer, conv_w, conv_b, conv_state, seq):
    m, d = h2.shape
    dff = w_up.shape[2] // 2
    tn = 512
    assert dff % tn == 0
    nj = dff // tn
    bsz = m // seq
    p1 = jnp.zeros((bsz, seq, 2 * dff), F32).at[:, 0].set(conv_state[:, 1]).reshape(m, 2 * dff)
    p2 = jnp.zeros((bsz, seq, 2 * dff), F32).at[:, 0].set(conv_state[:, 0]).at[:, 1].set(conv_state[:, 1])
    p2 = p2.reshape(m, 2 * dff)
    kern = functools.partial(_ffn_up_sample_kernel, seq=seq)
    wspec = lambda off: pl.BlockSpec((None, d, tn), lambda j: (layer, 0, j + off))
    cspec = lambda r, off: pl.BlockSpec((r, tn), lambda j: (0, j + off))
    cb = conv_b.reshape(1, -1)
    ospec = pl.BlockSpec((m, tn), lambda j: (0, j))
    return pl.pallas_call(
        kern,
        out_shape=(jax.ShapeDtypeStruct((m, dff), BF16),
                   jax.ShapeDtypeStruct((m, dff), F32),
                   jax.ShapeDtypeStruct((m, dff), F32)),
        grid=(nj,),
        in_specs=[pl.BlockSpec((m, d), lambda j: (0, 0)),
                  wspec(0), wspec(nj), cspec(3, 0), cspec(3, nj), cspec(1, 0), cspec(1, nj),
                  cspec(m, 0), cspec(m, nj), cspec(m, 0), cspec(m, nj)],
        out_specs=(ospec, ospec, ospec),
        compiler_params=_cparams("parallel"),
        name="ffn_up_sample",
    )(h2, w_up, w_up, conv_w, conv_w, cb, cb, p1, p1, p2, p2)


def _layer(x, l, w, dims, attn_fn, mstate, gstate, conv_state, bsz, seq):
    (heads_a, dk_a, heads_b, heads_c, dk_c) = dims
    wa = heads_a * LANES
    wb = heads_b * LANES
    prompt = conv_state is None
    z, zs = _proj_in(x, w["norm_mix_g"][l], w["w_main"], w["w_small"], l)
    qn, kn, knb, vb, vr = _qknorm(z, w["q_norm_g"][l], w["k_norm_g"][l], heads_a, dk_a, seq,
                                  v_transposed=prompt)
    lam_init = 0.8 - 0.6 * math.exp(-0.3 * l)
    oa = attn_fn(l, qn, knb, vb, lam_init)

    chunk_b = SCAN_CHUNK if seq % SCAN_CHUNK == 0 else SAMPLE_CHUNK
    gates = zs[:, GATE_I:GATE_I + 2 * heads_b].reshape(bsz, seq, 2 * heads_b)
    gt = jnp.swapaxes(gates, 1, 2)
    if seq < chunk_b:
        gt = jnp.pad(gt, ((0, 0), (0, 0), (0, chunk_b - seq)))
    ob, c_f, n_f, m_f = _mlstm(z, zs, gt, w["mlstm_gate_b"][l], w["mlstm_norm_g"][l],
                               mstate[0], mstate[1], mstate[2], seq, chunk_b, (3 * wa) // wb)

    chunk_c = GLA_CHUNK if seq % GLA_CHUNK == 0 else SAMPLE_CHUNK
    kw = heads_c * dk_c
    vw = heads_c * LANES
    c0 = 3 * wa + 4 * wb
    oc, s_f = _gla(z, zs, w["gla_w_alpha"][l], w["gla_b_alpha"][l], w["gla_norm_g"][l], gstate,
                   seq, chunk_c, c0 // kw, c0 // kw + 1, (c0 + 2 * kw) // vw, (c0 + 2 * kw) // vw + 1)

    x1, h2 = _proj_out(oa, ob, oc, w["w_out"], l, x, w["norm_ffn_g"][l])
    if prompt:
        act, tg, tv = _ffn_up_prompt(h2, w["w_up"], l, w["ffn_conv_w"][l], w["ffn_conv_b"][l], bsz, seq)
        conv_rows = jnp.concatenate([tg[:, 6:8], tv[:, 6:8]], axis=-1)
    else:
        act, ug, uv = _ffn_up_sample(h2, w["w_up"], l, w["ffn_conv_w"][l], w["ffn_conv_b"][l], conv_state, seq)
        u = jnp.concatenate([ug, uv], axis=-1).reshape(bsz, seq, -1)
        conv_rows = u[:, seq - 2:]
    x2 = _proj_down(act, w["w_down"], l, x1)

    k_rows = kn.reshape(bsz, seq, heads_a, LANES)
    v_rows = vr.reshape(bsz, seq, heads_a, LANES)
    return x2, (k_rows, v_rows, c_f, n_f, m_f[:, :, 0], s_f, conv_rows)


def kernel(x_prompt, x_sample, cache_k, cache_v, page_table, state_mlstm_C, state_mlstm_n, state_mlstm_m, state_gla_S, state_ffn_conv, norm_mix_g, w_in, q_norm_g, k_norm_g, diff_lambda, diff_subln_g, rel_bias, mlstm_gate_b, mlstm_norm_g, gla_w_alpha, gla_b_alpha, gla_norm_g, w_out, norm_ffn_g, ffn_w_up, ffn_conv_w, ffn_conv_b, ffn_w_down):
    depth = w_in.shape[0]
    bp, sp, d_model = x_prompt.shape
    bs, ss, _ = x_sample.shape
    heads_a, dv_a = cache_v.shape[3], cache_v.shape[4]
    dk_a = cache_k.shape[4] // 2
    heads_b, dk_b, dv_b = state_mlstm_C.shape[2:]
    heads_c, dk_c, dv_c = state_gla_S.shape[2:]
    rank = gla_w_alpha.shape[1]
    page = cache_k.shape[2]
    assert dv_a == LANES and 2 * dk_a == LANES and dk_b == LANES and dv_b == LANES and dv_c == LANES
    assert page >= MAX_DISTANCE and ATTN_TILE >= MAX_DISTANCE and rank <= GATE_I
    wa, wb = heads_a * LANES, heads_b * LANES
    n_main = 3 * wa + 4 * wb + 2 * heads_c * dk_c + 2 * heads_c * dv_c
    gate0 = 3 * wa + 4 * wb
    c0 = gate0 + 2 * heads_b
    assert w_in.shape[2] == n_main + 2 * heads_b + rank

    w_main = jnp.concatenate([w_in[:, :, :gate0], w_in[:, :, c0:c0 + n_main - gate0]], axis=-1).astype(BF16)
    w_small = jnp.zeros((depth, d_model, LANES), F32)
    w_small = w_small.at[:, :, :rank].set(w_in[:, :, n_main + 2 * heads_b:])
    w_small = w_small.at[:, :, GATE_I:GATE_I + 2 * heads_b].set(w_in[:, :, gate0:c0]).astype(BF16)
    w = dict(norm_mix_g=norm_mix_g, w_main=w_main, w_small=w_small, q_norm_g=q_norm_g, k_norm_g=k_norm_g,
             mlstm_gate_b=mlstm_gate_b, mlstm_norm_g=mlstm_norm_g, gla_w_alpha=gla_w_alpha,
             gla_b_alpha=gla_b_alpha, gla_norm_g=gla_norm_g, w_out=w_out.astype(BF16),
             norm_ffn_g=norm_ffn_g, w_up=ffn_w_up, ffn_conv_w=ffn_conv_w,
             ffn_conv_b=ffn_conv_b, w_down=ffn_w_down.astype(BF16))
    dims = (heads_a, dk_a, heads_b, heads_c, dk_c)

    bias_p = _prompt_bias(rel_bias, ATTN_TILE)
    bias_pg, bias_new = _decode_bias(rel_bias, page, DECODE_PAGES_PER_STEP, ss)
    kc = cache_k.reshape(depth, cache_k.shape[1], page * heads_a, LANES)
    vc = cache_v.reshape(depth, cache_v.shape[1], page * heads_a, LANES)

    def prompt_attn(l, qn, knb, vt, lam_init):
        shp = (bp, sp, wa)
        o = _prompt_attention(qn.reshape(shp), knb.reshape(shp), vt, bias_p,
                              diff_lambda[l], diff_subln_g[l], lam_init, dk_a)
        return o.reshape(bp * sp, wa)

    def sample_attn(l, qn, knb, vb, lam_init):
        shp = (bs, ss, wa)
        o = _decode_attention(l, qn.reshape(shp), knb.reshape(shp), vb.reshape(shp), kc, vc, page_table,
                              bias_pg, bias_new, diff_lambda[l], diff_subln_g[l], lam_init, dk_a)
        return o.reshape(bs * ss, wa)

    xp = x_prompt.reshape(bp * sp, d_model)
    xs = x_sample.reshape(bs * ss, d_model)
    zero_m = (jnp.zeros((bp, heads_b, dk_b, dv_b), F32), jnp.zeros((bp, heads_b, dk_b), F32),
              jnp.zeros((bp, heads_b), F32))
    zero_g = jnp.zeros((bp, heads_c, dk_c, dv_c), F32)
    rows_p, rows_s = [], []
    for l in range(depth):
        xp, rp = _layer(xp, l, w, dims, prompt_attn, zero_m, zero_g, None, bp, sp)
        rows_p.append(rp)
        xs, rs = _layer(xs, l, w, dims, sample_attn,
                        (state_mlstm_C[l], state_mlstm_n[l], state_mlstm_m[l]), state_gla_S[l],
                        state_ffn_conv[l], bs, ss)
        rows_s.append(rs)

    def field(rows, i):
        return jnp.stack([r[i] for r in rows], axis=0)

    return (xp.reshape(bp, sp, d_model), xs.reshape(bs, ss, d_model),
            *[field(rows_p, i) for i in range(7)], *[field(rows_s, i) for i in range(7)])
```

```python
import functools
import math

import numpy as np
import jax
import jax.numpy as jnp
from jax import lax
from jax.experimental import pallas as pl
from jax.experimental.pallas import tpu as pltpu

F32 = jnp.float32
BF16 = jnp.bfloat16

LANES = 128
VMEM_LIMIT = 52 * 1024 * 1024
EPS = 1e-6
NEG = -1e30
LOG2E = math.log2(math.e)
GLA_TAU = 16.0
MAX_DISTANCE = 128
ATTN_TILE = 256
SCAN_CHUNK = 256
GLA_CHUNK = 128
SAMPLE_CHUNK = 128
DECODE_PAGES_PER_STEP = 8


def _cparams(*sem):
    return pltpu.CompilerParams(dimension_semantics=sem, vmem_limit_bytes=VMEM_LIMIT)


def _split_bf16(x):
    hi = x.astype(BF16)
    lo = (x - hi.astype(F32)).astype(BF16)
    return hi, lo


def _dot(a, b):
    return jnp.dot(a, b, preferred_element_type=F32)


def _dot_nt(a, b):
    return lax.dot_general(a, b, (((1,), (1,)), ((), ())), preferred_element_type=F32)


def _dot_tn(a, b):
    return lax.dot_general(a, b, (((0,), (0,)), ((), ())), preferred_element_type=F32)


def _log_sigmoid(x):
    return jnp.minimum(x, 0.0) - jnp.log(1.0 + jnp.exp(-jnp.abs(x)))


def _sigmoid(x):
    return 1.0 / (1.0 + jnp.exp(-x))


def _div_pow2(x, n):
    assert n & (n - 1) == 0
    return lax.shift_right_logical(x, n.bit_length() - 1)


def _mod_pow2(x, n):
    assert n & (n - 1) == 0
    return x & (n - 1)


def _mixer_dtype(block_rows):
    return BF16 if block_rows % 16 == 0 else F32


def _pad_rows(x, rows):
    if x.shape[0] == rows:
        return x
    return jnp.concatenate([x, jnp.zeros((rows - x.shape[0], x.shape[1]), x.dtype)], axis=0)


def _proj_down_kernel(a_ref, w_ref, r_ref, o_ref, w_sc):
    @pl.when(pl.program_id(1) == 0)
    def _cast_weights():
        w_sc[...] = w_ref[...].astype(BF16)

    o_ref[...] = r_ref[...] + _dot(a_ref[...], w_sc[...])


def _proj_down(a, w, layer, res):
    m, k = a.shape
    n = w.shape[2]
    tm = min(m, 512)
    tn = 512
    assert m % tm == 0 and n % tn == 0
    return pl.pallas_call(
        _proj_down_kernel,
        out_shape=jax.ShapeDtypeStruct((m, n), F32),
        grid=(n // tn, m // tm),
        in_specs=[pl.BlockSpec((tm, k), lambda j, i: (i, 0)),
                  pl.BlockSpec((None, k, tn), lambda j, i: (layer, 0, j)),
                  pl.BlockSpec((tm, tn), lambda j, i: (i, j))],
        out_specs=pl.BlockSpec((tm, tn), lambda j, i: (i, j)),
        scratch_shapes=[pltpu.VMEM((k, tn), BF16)],
        compiler_params=_cparams("parallel", "arbitrary"),
        name="proj_down",
    )(a, w, res)


def _rms_rows(x, g):
    ms = jnp.mean(x * x, axis=-1, keepdims=True)
    return x * lax.rsqrt(ms + EPS) * g


def _proj_in_kernel(x_ref, g_ref, wa_ref, wc_ref, ws_ref, z_ref, zs_ref, h_sc, *, n_lead):
    j = pl.program_id(1)

    @pl.when(j == 0)
    def _norm():
        h_sc[...] = _rms_rows(x_ref[...], g_ref[...]).astype(BF16)
        zs_ref[...] = _dot(h_sc[...], ws_ref[...])

    @pl.when(j < n_lead)
    def _lead():
        z_ref[...] = _dot(h_sc[...], wa_ref[...].astype(BF16))

    @pl.when(j >= n_lead)
    def _tail():
        z_ref[...] = _dot(h_sc[...], wc_ref[...])


def _proj_in(x, g, w_in, n_lead_cols, w_tail, w_small, layer):
    m, d = x.shape
    tm = min(m, 1024)
    tn = 512
    n = n_lead_cols + w_tail.shape[2]
    ns = w_small.shape[2]
    assert m % tm == 0 and n_lead_cols % tn == 0 and w_tail.shape[2] % tn == 0
    n_lead = n_lead_cols // tn
    return pl.pallas_call(
        functools.partial(_proj_in_kernel, n_lead=n_lead),
        out_shape=(jax.ShapeDtypeStruct((m, n), F32), jax.ShapeDtypeStruct((m, ns), F32)),
        grid=(m // tm, n // tn),
        in_specs=[pl.BlockSpec((tm, d), lambda i, j: (i, 0)),
                  pl.BlockSpec((1, d), lambda i, j: (0, 0)),
                  pl.BlockSpec((None, d, tn), lambda i, j: (layer, 0, jnp.minimum(j, n_lead - 1))),
                  pl.BlockSpec((None, d, tn), lambda i, j: (layer, 0, jnp.maximum(j - n_lead, 0))),
                  pl.BlockSpec((None, d, ns), lambda i, j: (layer, 0, 0))],
        out_specs=(pl.BlockSpec((tm, tn), lambda i, j: (i, j)),
                   pl.BlockSpec((tm, ns), lambda i, j: (i, 0))),
        scratch_shapes=[pltpu.VMEM((tm, d), BF16)],
        compiler_params=_cparams("parallel", "arbitrary"),
        name="proj_in",
    )(x, g.reshape(1, d), w_in, w_tail, w_small)


def _proj_out_kernel(oa_ref, ob_ref, oc_ref, w_ref, r_ref, g_ref, x_ref, h_ref):
    ka, kb = oa_ref.shape[1], ob_ref.shape[1]
    acc = _dot(oa_ref[...].astype(BF16), w_ref[0:ka, :])
    acc += _dot(ob_ref[...].astype(BF16), w_ref[ka:ka + kb, :])
    acc += _dot(oc_ref[...].astype(BF16), w_ref[ka + kb:, :])
    x = r_ref[...] + acc
    x_ref[...] = x
    h_ref[...] = _rms_rows(x, g_ref[...]).astype(BF16)


def _proj_out(oa, ob, oc, w, layer, res, g):
    m = oa.shape[0]
    k, n = w.shape[1], w.shape[2]
    tm = min(m, 256)
    assert m % tm == 0 and oa.shape[1] + ob.shape[1] + oc.shape[1] == k
    rows = lambda width: pl.BlockSpec((tm, width), lambda i: (i, 0))
    return pl.pallas_call(
        _proj_out_kernel,
        out_shape=(jax.ShapeDtypeStruct((m, n), F32), jax.ShapeDtypeStruct((m, n), BF16)),
        grid=(m // tm,),
        in_specs=[rows(oa.shape[1]), rows(ob.shape[1]), rows(oc.shape[1]),
                  pl.BlockSpec((None, k, n), lambda i: (layer, 0, 0)),
                  rows(n),
                  pl.BlockSpec((1, n), lambda i: (0, 0))],
        out_specs=(rows(n), rows(n)),
        compiler_params=_cparams("parallel"),
        name="proj_out",
    )(oa, ob, oc, w, res, g.reshape(1, n))


def _qknorm_kernel(aq_ref, ak_ref, av_ref, qg_ref, kg_ref, bd_ref,
                   qn_ref, kn_ref, knb_ref, vb_ref, vr_ref, *, heads, inv_dk, scale, v_transposed):
    bd = bd_ref[...]
    qg = qg_ref[...]
    kg = kg_ref[...]

    def norm(x, g):
        hi, lo = _split_bf16(x * x)
        ss = _dot(hi, bd) + _dot(lo, bd)
        return x * lax.rsqrt(ss * inv_dk + EPS) * g

    for h in range(heads):
        sl = slice(LANES * h, LANES * (h + 1))
        qn_ref[:, sl] = (norm(aq_ref[:, sl], qg) * scale).astype(BF16)
        kn = norm(ak_ref[:, sl], kg)
        kn_ref[:, sl] = kn
        knb_ref[:, sl] = kn.astype(BF16)
        if v_transposed:
            vb_ref[0, sl, :] = av_ref[:, sl].T.astype(BF16)
    if not v_transposed:
        vb_ref[...] = av_ref[...].astype(BF16)
    vr_ref[...] = av_ref[...]


def _qknorm(z, qg, kg, heads, dk, seq, v_transposed):
    m = z.shape[0]
    w = heads * LANES
    tm = min(m, 256)
    assert m % tm == 0
    lane = np.arange(LANES)
    bd = jnp.asarray((lane[:, None] // dk) == (lane[None, :] // dk), BF16)
    reps = LANES // dk
    kern = functools.partial(_qknorm_kernel, heads=heads, inv_dk=1.0 / dk, scale=dk ** -0.5 * LOG2E,
                             v_transposed=v_transposed)
    rows = pl.BlockSpec((tm, w), lambda i: (i, 0))
    if v_transposed:
        tps = seq // tm
        v_shape = jax.ShapeDtypeStruct((m // seq, w, seq), BF16)
        v_spec = pl.BlockSpec((1, w, tm), lambda i: (i // tps, 0, i % tps))
    else:
        v_shape = jax.ShapeDtypeStruct((m, w), BF16)
        v_spec = rows
    return pl.pallas_call(
        kern,
        out_shape=(jax.ShapeDtypeStruct((m, w), BF16), jax.ShapeDtypeStruct((m, w), F32),
                   jax.ShapeDtypeStruct((m, w), BF16), v_shape, jax.ShapeDtypeStruct((m, w), F32)),
        grid=(m // tm,),
        in_specs=[pl.BlockSpec((tm, w), lambda i: (i, 0)),
                  pl.BlockSpec((tm, w), lambda i: (i, 1)),
                  pl.BlockSpec((tm, w), lambda i: (i, 2)),
                  pl.BlockSpec((1, LANES), lambda i: (0, 0)),
                  pl.BlockSpec((1, LANES), lambda i: (0, 0)),
                  pl.BlockSpec((LANES, LANES), lambda i: (0, 0))],
        out_specs=(rows, rows, rows, v_spec, rows),
        compiler_params=_cparams("parallel"),
        name="qknorm",
    )(z, z, z, jnp.tile(qg, reps).reshape(1, LANES), jnp.tile(kg, reps).reshape(1, LANES), bd)


def _bucket(n, n_buckets):
    max_exact = n_buckets // 2
    nf = jnp.maximum(n, max_exact).astype(F32)
    large = max_exact + (jnp.log(nf / max_exact) / math.log(MAX_DISTANCE / max_exact)
                         * (n_buckets - max_exact)).astype(jnp.int32)
    large = jnp.minimum(large, n_buckets - 1)
    return jnp.where(n < max_exact, jnp.maximum(n, 0), large)


def _prompt_bias_kernel(rb_ref, o_ref, *, tile, n_buckets):
    h = pl.program_id(0)
    j = lax.broadcasted_iota(jnp.int32, (tile, tile), 0)
    i = lax.broadcasted_iota(jnp.int32, (tile, tile), 1)
    far = jnp.full((tile, tile), MAX_DISTANCE, jnp.int32)
    for t, n in enumerate((i - j, tile + i - j, far)):
        b = _bucket(n, n_buckets)
        val = jnp.full((tile, tile), rb_ref[0, h], F32)
        for k in range(1, n_buckets):
            val = jnp.where(b == k, rb_ref[k, h], val)
        o_ref[0, t] = jnp.where(n >= 0, val * LOG2E, NEG)


def _prompt_bias(rel_bias, tile):
    n_buckets, heads = rel_bias.shape
    kern = functools.partial(_prompt_bias_kernel, tile=tile, n_buckets=n_buckets)
    return pl.pallas_call(
        kern,
        out_shape=jax.ShapeDtypeStruct((heads, 3, tile, tile), F32),
        grid=(heads,),
        in_specs=[pl.BlockSpec(memory_space=pltpu.SMEM)],
        out_specs=pl.BlockSpec((1, 3, tile, tile), lambda h: (h, 0, 0, 0)),
        compiler_params=_cparams("parallel"),
        name="prompt_bias",
    )(rel_bias)


def _decode_bias_kernel(rb_ref, pg_ref, new_ref, *, page, ppb, dec_seq, heads, n_buckets):
    grp = 2 * dec_seq

    def lookup(n, h):
        b = _bucket(n, n_buckets)
        val = jnp.full(n.shape, rb_ref[0, h], F32)
        for k in range(1, n_buckets):
            val = jnp.where(b == k, rb_ref[k, h], val)
        return jnp.where(n >= 0, val * LOG2E, NEG)

    qi = _mod_pow2(lax.broadcasted_iota(jnp.int32, (grp, page), 0), dec_seq)
    j = lax.broadcasted_iota(jnp.int32, (grp, page), 1)
    far = jnp.full((grp, page), MAX_DISTANCE, jnp.int32)
    for h in range(heads):
        rs = slice(grp * h, grp * (h + 1))
        far_b = lookup(far, h)
        for c in range(ppb):
            pg_ref[0, rs, page * c:page * (c + 1)] = far_b
            pg_ref[1, rs, page * c:page * (c + 1)] = far_b if c < ppb - 1 else lookup(page + qi - j, h)
        new_ref[rs, :] = lookup(jnp.where(j < dec_seq, qi - j, -1), h)


def _decode_bias(rel_bias, page, ppb, dec_seq):
    n_buckets, heads = rel_bias.shape
    rows = 2 * dec_seq * heads
    kern = functools.partial(_decode_bias_kernel, page=page, ppb=ppb, dec_seq=dec_seq, heads=heads,
                             n_buckets=n_buckets)
    return pl.pallas_call(
        kern,
        out_shape=(jax.ShapeDtypeStruct((2, rows, ppb * page), F32),
                   jax.ShapeDtypeStruct((rows, page), F32)),
        in_specs=[pl.BlockSpec(memory_space=pltpu.SMEM)],
        name="decode_bias",
    )(rel_bias)


def _diff_lambda(dl, lam_init):
    s1 = jnp.sum(dl[0:1] * dl[1:2], axis=-1, keepdims=True)
    s2 = jnp.sum(dl[2:3] * dl[3:4], axis=-1, keepdims=True)
    return jnp.exp(s1) - jnp.exp(s2) + lam_init


def _subln(o, g, lam_init):
    ms = jnp.mean(o * o, axis=-1, keepdims=True)
    return o * lax.rsqrt(ms + EPS) * g * (1.0 - lam_init)


def _flash_update(s, m_prev, l_prev):
    m_new = jnp.maximum(m_prev, jnp.max(s, axis=-1, keepdims=True))
    alpha = jnp.exp2(m_prev - m_new)
    p = jnp.exp2(s - m_new)
    l_new = alpha * l_prev + jnp.sum(p, axis=-1, keepdims=True)
    return p, alpha, m_new, l_new


def _attn_kernel(qt_ref, kt_ref, q_ref, k_ref, vt_ref, bias_ref, dl_ref, g_ref, o_ref, m_sc, l_sc, acc_sc,
                 *, heads, tile, dk, lam_init):
    qi = qt_ref[pl.program_id(1)]
    ki = kt_ref[pl.program_id(1)]
    n_sub = tile // LANES

    @pl.when(ki == 0)
    def _init():
        m_sc[...] = jnp.full(m_sc.shape, -jnp.inf, F32)
        l_sc[...] = jnp.zeros(l_sc.shape, F32)
        acc_sc[...] = jnp.zeros(acc_sc.shape, F32)

    lane = lax.broadcasted_iota(jnp.int32, (LANES, LANES), 1)
    for h in range(heads):
        sl = slice(LANES * h, LANES * (h + 1))
        kh = k_ref[0, :, sl]
        vht = vt_ref[0, sl, :]
        for c in range(2 * n_sub):
            qrows = slice(LANES * (c % n_sub), LANES * (c % n_sub + 1))
            cs = slice(LANES * c, LANES * (c + 1))
            qc = q_ref[0, qrows, sl]
            own_map = (lane < dk) if c < n_sub else (lane >= dk)
            qc = jnp.where(own_map, qc, jnp.zeros_like(qc))
            s = _dot_nt(kh, qc) + bias_ref[h, 0, :, qrows]
            m_prev = m_sc[h, :, cs]
            m_new = jnp.maximum(m_prev, jnp.max(s, axis=0, keepdims=True))
            alpha = jnp.exp2(m_prev - m_new)
            p = jnp.exp2(s - m_new)
            l_sc[h, :, cs] = alpha * l_sc[h, :, cs] + jnp.sum(p, axis=0, keepdims=True)
            acc_sc[h, :, cs] = alpha * acc_sc[h, :, cs] + _dot(vht, p.astype(BF16))
            m_sc[h, :, cs] = m_new

    @pl.when(ki == qi)
    def _finish():
        lam = _diff_lambda(dl_ref[...], lam_init)
        g_col = g_ref[...]
        for h in range(heads):
            acc = acc_sc[h]
            l = l_sc[h]
            ot = acc[:, :tile] / l[:, :tile] - lam * (acc[:, tile:] / l[:, tile:])
            ms = jnp.mean(ot * ot, axis=0, keepdims=True)
            ot = ot * lax.rsqrt(ms + EPS) * g_col * (1.0 - lam_init)
            o_ref[0, :, LANES * h:LANES * (h + 1)] = ot.T.astype(o_ref.dtype)


def _prompt_attention(qn, knb, vt, bias, dl, g, lam_init, dk):
    bsz, seq, w = qn.shape
    heads = w // LANES
    tile = bias.shape[-1]
    nq = seq // tile
    assert seq % tile == 0 and tile % LANES == 0
    kern = functools.partial(_attn_kernel, heads=heads, tile=tile, dk=dk, lam_init=lam_init)
    pairs = [(qi, ki) for qi in range(nq) for ki in range(qi + 1)]
    q_tab = jnp.asarray([p[0] for p in pairs], jnp.int32)
    k_tab = jnp.asarray([p[1] for p in pairs], jnp.int32)

    def bias_idx(b, t, qt, kt):
        return (0, jnp.where(kt[t] == qt[t], 0, jnp.where(kt[t] == qt[t] - 1, 1, 2)), 0, 0)

    grid_spec = pltpu.PrefetchScalarGridSpec(
        num_scalar_prefetch=2,
        grid=(bsz, len(pairs)),
        in_specs=[pl.BlockSpec((1, tile, w), lambda b, t, qt, kt: (b, qt[t], 0)),
                  pl.BlockSpec((1, tile, w), lambda b, t, qt, kt: (b, kt[t], 0)),
                  pl.BlockSpec((1, w, tile), lambda b, t, qt, kt: (b, 0, kt[t])),
                  pl.BlockSpec((heads, 1, tile, tile), bias_idx),
                  pl.BlockSpec(dl.shape, lambda b, t, qt, kt: (0, 0)),
                  pl.BlockSpec((LANES, 1), lambda b, t, qt, kt: (0, 0))],
        out_specs=pl.BlockSpec((1, tile, w), lambda b, t, qt, kt: (b, qt[t], 0)),
        scratch_shapes=[pltpu.VMEM((heads, 1, 2 * tile), F32),
                        pltpu.VMEM((heads, 1, 2 * tile), F32),
                        pltpu.VMEM((heads, LANES, 2 * tile), F32)],
    )
    return pl.pallas_call(
        kern,
        out_shape=jax.ShapeDtypeStruct((bsz, seq, w), BF16),
        grid_spec=grid_spec,
        compiler_params=_cparams("parallel", "arbitrary"),
        name="prompt_attention",
    )(q_tab, k_tab, qn, knb, vt, bias, dl, g.reshape(LANES, 1))


def _decode_attn_kernel(pt_ref, q_ref, kn_ref, vn_ref, *refs, heads, dec_seq, page, ppb, dk, lam_init):
    del pt_ref
    kc_refs, vc_refs = refs[:ppb], refs[ppb:2 * ppb]
    bias_ref, bnew_ref, dl_ref, g_ref, o_ref, wq_sc, wqb_sc, m_sc, l_sc, acc_sc = refs[2 * ppb:]
    p_idx = pl.program_id(1)
    rows = 2 * dec_seq * heads
    grp = 2 * dec_seq
    width = heads * LANES

    @pl.when(p_idx == 0)
    def _init():
        q = q_ref[0].astype(F32)
        lane = lax.broadcasted_iota(jnp.int32, (dec_seq, LANES), 1)
        pieces = []
        for h in range(heads):
            qh = q[:, LANES * h:LANES * (h + 1)]
            pieces += [jnp.where(lane < dk, qh, 0.0), jnp.where(lane >= dk, qh, 0.0)]
        wq = jnp.concatenate(pieces, axis=0)
        wq_sc[...] = wq.astype(BF16)
        c = lax.broadcasted_iota(jnp.int32, (rows, width), 1)
        r = lax.broadcasted_iota(jnp.int32, (rows, width), 0)
        own = _div_pow2(r, grp) == _div_pow2(c, LANES)
        wqb_sc[...] = jnp.where(own, jnp.concatenate([wq] * heads, axis=1), 0.0).astype(BF16)
        m_sc[...] = jnp.full(m_sc.shape, -jnp.inf, F32)
        l_sc[...] = jnp.zeros(l_sc.shape, F32)
        acc_sc[...] = jnp.zeros(acc_sc.shape, F32)

    def head_rows(ref, h):
        return ref[pl.ds(h, page, stride=heads), :].astype(BF16)

    s = jnp.concatenate(
        [jnp.concatenate([_dot_nt(wq_sc[grp * h:grp * (h + 1), :], head_rows(kc, h)) for h in range(heads)],
                         axis=0) for kc in kc_refs], axis=1) + bias_ref[0]
    p, alpha, m_new, l_new = _flash_update(s, m_sc[...], l_sc[...])
    m_sc[...] = m_new
    l_sc[...] = l_new
    pb = p.astype(BF16)
    for h in range(heads):
        rs = slice(grp * h, grp * (h + 1))
        acc = alpha[rs] * acc_sc[rs, :]
        for j, vc in enumerate(vc_refs):
            acc += _dot(pb[rs, page * j:page * (j + 1)], head_rows(vc, h))
        acc_sc[rs, :] = acc

    @pl.when(p_idx == pl.num_programs(1) - 1)
    def _finish():
        kn = _pad_rows(kn_ref[0].astype(F32), page).astype(BF16)
        vn = _pad_rows(vn_ref[0].astype(F32), page)
        s = _dot_nt(wqb_sc[...], kn) + bnew_ref[...]
        p, alpha, _, l_fin = _flash_update(s, m_sc[...], l_sc[...])
        pb = p.astype(BF16)
        lam = _diff_lambda(dl_ref[...], lam_init)
        g = g_ref[...]
        for h in range(heads):
            rs = slice(grp * h, grp * (h + 1))
            vh = vn[:, LANES * h:LANES * (h + 1)].astype(BF16)
            acc = alpha[rs] * acc_sc[rs, :] + _dot(pb[rs], vh)
            l = l_fin[rs]
            o = acc[:dec_seq] / l[:dec_seq] - lam * (acc[dec_seq:] / l[dec_seq:])
            o_ref[0, :, LANES * h:LANES * (h + 1)] = _subln(o, g, lam_init)


def _decode_attention(layer, qn, knb, vb, cache_k, cache_v, page_table, bias_pg, bias_new, dl, g, lam_init, dk):
    bsz, dec_seq, w = qn.shape
    heads = w // LANES
    page = cache_k.shape[2] // heads
    ppb = bias_pg.shape[-1] // page
    n_steps = page_table.shape[1] // ppb
    assert page_table.shape[1] % ppb == 0
    rows = 2 * dec_seq * heads
    kern = functools.partial(_decode_attn_kernel, heads=heads, dec_seq=dec_seq, page=page, ppb=ppb, dk=dk,
                             lam_init=lam_init)
    new_spec = pl.BlockSpec((1, dec_seq, w), lambda b, p, pt: (b, 0, 0))
    cache_specs = [pl.BlockSpec((None, None, page * heads, LANES),
                                lambda b, p, pt, j=j: (layer, pt[b, p * ppb + j], 0, 0)) for j in range(ppb)]
    grid_spec = pltpu.PrefetchScalarGridSpec(
        num_scalar_prefetch=1,
        grid=(bsz, n_steps),
        in_specs=[new_spec, new_spec, new_spec, *cache_specs, *cache_specs,
                  pl.BlockSpec((1, rows, ppb * page),
                               lambda b, p, pt: (jnp.where(p == n_steps - 1, 1, 0), 0, 0)),
                  pl.BlockSpec((rows, page), lambda b, p, pt: (0, 0)),
                  pl.BlockSpec(dl.shape, lambda b, p, pt: (0, 0)),
                  pl.BlockSpec((1, LANES), lambda b, p, pt: (0, 0))],
        out_specs=pl.BlockSpec((1, dec_seq, w), lambda b, p, pt: (b, 0, 0)),
        scratch_shapes=[pltpu.VMEM((rows, LANES), BF16),
                        pltpu.VMEM((rows, w), BF16),
                        pltpu.VMEM((rows, 1), F32),
                        pltpu.VMEM((rows, 1), F32),
                        pltpu.VMEM((rows, LANES), F32)],
    )
    return pl.pallas_call(
        kern,
        out_shape=jax.ShapeDtypeStruct((bsz, dec_seq, w), F32),
        grid_spec=grid_spec,
        compiler_params=_cparams("parallel", "arbitrary"),
        name="decode_attention",
    )(page_table, qn, knb, vb, *([cache_k] * ppb), *([cache_v] * ppb), bias_pg, bias_new, dl,
      g.reshape(1, LANES))


GATE_I = 16
GATE_F = 20


def _mlstm_kernel(q_ref, k_ref, v_ref, og_ref, zs_ref, gt_ref, gbl_ref, gbc_ref, g_ref,
                  c0_ref, n0_ref, m0_ref,
                  o_ref, cf_ref, nf_ref, mf_ref, c_sc, n_sc, m_sc,
                  *, heads, chunk, valid, scale):
    c_idx = pl.program_id(1)

    @pl.when(c_idx == 0)
    def _init():
        for h in range(heads):
            c_sc[h] = c0_ref[0, h].T
        n_sc[...] = n0_ref[0]
        m_sc[...] = m0_ref[0]

    rows_in = q_ref.shape[0]
    row = lax.broadcasted_iota(jnp.int32, (chunk, 1), 0)
    s_i = lax.broadcasted_iota(jnp.int32, (chunk, chunk), 0)
    t_i = lax.broadcasted_iota(jnp.int32, (chunk, chunk), 1)
    causal = s_i <= t_i
    incl = causal.astype(BF16)
    incl_t = (s_i >= t_i).astype(BF16)

    gcol = _pad_rows(zs_ref[...], chunk) + gbl_ref[...]
    lf_mat = _log_sigmoid(gcol)
    if valid < chunk:
        lf_mat = jnp.where(row < valid, lf_mat, 0.0)
    hi, lo = _split_bf16(lf_mat)
    b_mat = _dot(incl_t, hi) + _dot(incl_t, lo)

    grow = gt_ref[0] + gbc_ref[...]
    col = lax.broadcasted_iota(jnp.int32, grow.shape, 1)
    grow_id = lax.broadcasted_iota(jnp.int32, grow.shape, 0)
    lf_rows = jnp.where(grow_id >= heads, _log_sigmoid(grow), 0.0)
    if valid < chunk:
        lf_rows = jnp.where(col < valid, lf_rows, 0.0)
    hi, lo = _split_bf16(lf_rows)
    b_rows = _dot(hi, incl) + _dot(lo, incl)

    g = g_ref[...]
    first_row = lax.broadcasted_iota(jnp.int32, (16, 1), 0) == 0
    for h in range(heads):
        sl = slice(LANES * h, LANES * (h + 1))
        li_col = gcol[:, GATE_I + h:GATE_I + h + 1]
        if valid < chunk:
            li_col = jnp.where(row < valid, li_col, NEG)
        u_col = li_col - b_mat[:, GATE_F + h:GATE_F + h + 1]
        b_row = b_rows[heads + h:heads + h + 1, :]
        dt = jnp.where(causal, b_row + u_col, NEG)
        m_prev = m_sc[h:h + 1, 0:1]
        inter = b_row + m_prev
        mt = jnp.maximum(inter, jnp.max(dt, axis=0, keepdims=True))
        wi = jnp.exp(inter - mt)
        q = _pad_rows(q_ref[:, sl], chunk)
        ks = _pad_rows(k_ref[:, sl], chunk) * scale
        vtb = _pad_rows(v_ref[:, sl], chunk).T.astype(BF16)
        qb = q.astype(BF16)
        pt = _dot_nt(ks.astype(BF16), qb) * jnp.exp(dt - mt)
        ct_prev = c_sc[h]
        n_prev = n_sc[h:h + 1, :]
        num = wi * _dot_nt(ct_prev.astype(BF16), qb) + _dot(vtb, pt.astype(BF16))
        n_rows = jnp.where(first_row, jnp.broadcast_to(n_prev, (16, LANES)), 0.0)
        qn = _dot_nt(n_rows.astype(BF16), qb)[0:1]
        den = wi * qn + jnp.sum(pt, axis=0, keepdims=True)
        ht = num / jnp.maximum(jnp.abs(den), jnp.exp(-mt))
        og = _sigmoid(og_ref[:, sl])
        y = og * ht.T[:rows_in]
        ms = jnp.mean(y * y, axis=-1, keepdims=True)
        o_ref[:, sl] = (y * lax.rsqrt(ms + EPS) * g).astype(o_ref.dtype)

        m_new = mt[:, chunk - 1:chunk]
        b_last = b_row[:, chunk - 1:chunk]
        a = jnp.exp(b_last + m_prev - m_new)
        kw = ks * jnp.exp(u_col + (b_last - m_new))
        c_sc[h] = a * ct_prev + _dot(vtb, kw.astype(BF16))
        n_sc[h:h + 1, :] = a * n_prev + jnp.sum(kw, axis=0, keepdims=True)
        m_sc[h:h + 1, :] = jnp.broadcast_to(m_new, (1, LANES))

    @pl.when(c_idx == pl.num_programs(1) - 1)
    def _finish():
        for h in range(heads):
            cf_ref[0, h] = c_sc[h].T
        nf_ref[0] = n_sc[...]
        mf_ref[0] = m_sc[...]


def _mlstm(z, zs, gt, gate_b, g, c0, n0, m0, seq, chunk, col0):
    bsz, heads = c0.shape[0], c0.shape[1]
    w = heads * LANES
    rows_in = min(seq, chunk)
    nc = seq // rows_in
    valid = rows_in
    gt_w = gt.shape[-1] // nc
    kern = functools.partial(_mlstm_kernel, heads=heads, chunk=chunk, valid=valid, scale=LANES ** -0.5)
    gbl = jnp.zeros((1, LANES), F32)
    gbl = gbl.at[0, GATE_I:GATE_I + heads].set(gate_b[0]).at[0, GATE_F:GATE_F + heads].set(gate_b[1])
    gbc = gate_b.reshape(2 * heads, 1)
    m0b = jnp.broadcast_to(m0[:, :, None], (bsz, heads, LANES))

    def zspec(blk):
        return pl.BlockSpec((rows_in, w), lambda b, c: (b * nc + c, col0 + blk))

    state = lambda shape: pl.BlockSpec((1,) + shape, lambda b, c: (b,) + (0,) * len(shape))
    return pl.pallas_call(
        kern,
        out_shape=(jax.ShapeDtypeStruct((bsz * seq, w), _mixer_dtype(rows_in)),
                   jax.ShapeDtypeStruct((bsz, heads, LANES, LANES), F32),
                   jax.ShapeDtypeStruct((bsz, heads, LANES), F32),
                   jax.ShapeDtypeStruct((bsz, heads, LANES), F32)),
        grid=(bsz, nc),
        in_specs=[zspec(0), zspec(1), zspec(2), zspec(3),
                  pl.BlockSpec((rows_in, LANES), lambda b, c: (b * nc + c, 0)),
                  pl.BlockSpec((1, 2 * heads, gt_w), lambda b, c: (b, 0, c)),
                  pl.BlockSpec((1, LANES), lambda b, c: (0, 0)),
                  pl.BlockSpec((2 * heads, 1), lambda b, c: (0, 0)),
                  pl.BlockSpec((1, LANES), lambda b, c: (0, 0)),
                  state((heads, LANES, LANES)), state((heads, LANES)), state((heads, LANES))],
        out_specs=(pl.BlockSpec((rows_in, w), lambda b, c: (b * nc + c, 0)),
                   state((heads, LANES, LANES)), state((heads, LANES)), state((heads, LANES))),
        scratch_shapes=[pltpu.VMEM((heads, LANES, LANES), F32),
                        pltpu.VMEM((heads, LANES), F32),
                        pltpu.VMEM((heads, LANES), F32)],
        compiler_params=_cparams("parallel", "arbitrary"),
        name="mlstm",
    )(z, z, z, z, zs, gt, gbl, gbc, g.reshape(1, LANES), c0, n0, m0b)


def _gla_levels(chunk):
    n, out = chunk, []
    while n >= 2:
        out.append(n)
        n //= 2
    return out


def _gla_weights(chunk):
    t = np.arange(chunk)[:, None]
    s = np.arange(chunk)[None, :]
    blocks = [(s <= t).astype(np.float32), (s > t).astype(np.float32)]
    for n in _gla_levels(chunk):
        mid = (t // n) * n + n // 2 - 1
        blocks.append(((s > mid) & (s <= t)).astype(np.float32) - ((s > t) & (s <= mid)).astype(np.float32))
    return jnp.asarray(np.concatenate(blocks, axis=0), BF16)


def _gla_kernel(q_ref, k_ref, v_ref, gg_ref, zs_ref, wa_ref, ba_ref, ws_ref, g_ref, s0_ref,
                o_ref, sf_ref, s_sc, *, heads, chunk, valid, dk, scale):
    c_idx = pl.program_id(1)
    per = LANES // dk
    assert per == 2 and heads % per == 0

    @pl.when(c_idx == 0)
    def _init():
        for h in range(heads):
            s_sc[h // per, dk * (h % per):dk * (h % per + 1), :] = s0_ref[0, h]

    rows_in = q_ref.shape[0]
    row = lax.broadcasted_iota(jnp.int32, (chunk, 1), 0)
    t_i = lax.broadcasted_iota(jnp.int32, (per * chunk, chunk), 0) & (chunk - 1)
    s_i = lax.broadcasted_iota(jnp.int32, (per * chunk, chunk), 1)
    first_head = lax.broadcasted_iota(jnp.int32, (chunk, LANES), 1) < dk

    zs = _pad_rows(zs_ref[...], chunk)
    la = _log_sigmoid(_dot(zs.astype(BF16), wa_ref[...]) + ba_ref[...]) * (1.0 / GLA_TAU)
    if valid < chunk:
        la = jnp.where(row < valid, la, 0.0)
    hi, lo = _split_bf16(la)
    wst = ws_ref[...]
    e_all = _dot(wst, hi) + _dot(wst, lo)
    ones = jnp.ones((chunk, LANES), BF16)
    levels = _gla_levels(chunk)
    g = g_ref[...]
    second_half = [(row & (n - 1)) >= n // 2 for n in levels]
    same_node = [_div_pow2(t_i, n) == _div_pow2(s_i, n) for n in levels]

    def split_heads(x):
        return [jnp.where(first_head, x, 0.0), jnp.where(first_head, 0.0, x)]

    for j in range(heads // per):
        psl = slice(LANES * j, LANES * (j + 1))
        q = _pad_rows(q_ref[:, psl], chunk) * scale
        k = _pad_rows(k_ref[:, psl], chunk)
        bc = e_all[0:chunk, psl]
        rem = e_all[chunk:2 * chunk, psl]
        a_pair = jnp.zeros((per * chunk, chunk), F32)
        for li in range(len(levels)):
            e = e_all[(2 + li) * chunk:(3 + li) * chunk, psl]
            qt = jnp.where(second_half[li], q * jnp.exp(jnp.minimum(e, 0.0)), 0.0)
            kt = jnp.where(second_half[li], 0.0, k * jnp.exp(jnp.minimum(-e, 0.0)))
            qs = jnp.concatenate(split_heads(qt), axis=0).astype(BF16)
            a_pair = a_pair + jnp.where(same_node[li], _dot_nt(qs, kt.astype(BF16)), 0.0)
        s_prev = s_sc[j]
        s_prev_b = s_prev.astype(BF16)
        qd = split_heads(q * jnp.exp(bc))
        qk = split_heads(q * k)
        kd = split_heads(k * jnp.exp(rem))
        hi_p, lo_p = _split_bf16(la[:, psl])
        tot = _dot_tn(hi_p, ones) + _dot_tn(lo_p, ones)
        s_new = jnp.exp(tot) * s_prev
        for hh in range(per):
            h = per * j + hh
            vsl = slice(LANES * h, LANES * (h + 1))
            vb = _pad_rows(v_ref[:, vsl], chunk).astype(BF16)
            a_h = a_pair[chunk * hh:chunk * (hh + 1)]
            o = _dot(qd[hh].astype(BF16), s_prev_b) + _dot(a_h.astype(BF16), vb)
            o = o + jnp.sum(qk[hh], axis=-1, keepdims=True) * vb.astype(F32)
            o = o[:rows_in]
            ms = jnp.mean(o * o, axis=-1, keepdims=True)
            gate = gg_ref[:, vsl]
            o_ref[:, vsl] = (o * lax.rsqrt(ms + EPS) * g * (gate * _sigmoid(gate))).astype(o_ref.dtype)
            s_new = s_new + _dot_tn(kd[hh].astype(BF16), vb)
        s_sc[j] = s_new

    @pl.when(c_idx == pl.num_programs(1) - 1)
    def _finish():
        for h in range(heads):
            sf_ref[0, h] = s_sc[h // per, dk * (h % per):dk * (h % per + 1), :]


def _gla(z, zs, w_alpha, b_alpha, g, s0, seq, chunk, qcol, kcol, vcol, gcol):
    bsz, heads, dk, dv = s0.shape
    rows_in = min(seq, chunk)
    nc = seq // rows_in
    kw = heads * dk
    vw = heads * dv
    wa = jnp.zeros((LANES, kw), F32).at[:w_alpha.shape[0]].set(w_alpha).astype(BF16)
    wst = _gla_weights(chunk)
    kern = functools.partial(_gla_kernel, heads=heads, chunk=chunk, valid=rows_in, dk=dk, scale=dk ** -0.5)
    const = lambda shape: pl.BlockSpec(shape, lambda b, c: (0,) * len(shape))
    return pl.pallas_call(
        kern,
        out_shape=(jax.ShapeDtypeStruct((bsz * seq, vw), _mixer_dtype(rows_in)),
                   jax.ShapeDtypeStruct((bsz, heads, dk, dv), F32)),
        grid=(bsz, nc),
        in_specs=[pl.BlockSpec((rows_in, kw), lambda b, c: (b * nc + c, qcol)),
                  pl.BlockSpec((rows_in, kw), lambda b, c: (b * nc + c, kcol)),
                  pl.BlockSpec((rows_in, vw), lambda b, c: (b * nc + c, vcol)),
                  pl.BlockSpec((rows_in, vw), lambda b, c: (b * nc + c, gcol)),
                  pl.BlockSpec((rows_in, LANES), lambda b, c: (b * nc + c, 0)),
                  const((LANES, kw)), const((1, kw)), const(wst.shape), const((1, LANES)),
                  pl.BlockSpec((1, heads, dk, dv), lambda b, c: (b, 0, 0, 0))],
        out_specs=(pl.BlockSpec((rows_in, vw), lambda b, c: (b * nc + c, 0)),
                   pl.BlockSpec((1, heads, dk, dv), lambda b, c: (b, 0, 0, 0))),
        scratch_shapes=[pltpu.VMEM((kw // LANES, LANES, dv), F32)],
        compiler_params=_cparams("parallel", "arbitrary"),
        name="gla",
    )(z, z, z, z, zs, wa, b_alpha.reshape(1, kw), wst, g.reshape(1, LANES), s0)


def _conv_gate(ug, uv, cwg, cwv, cbg, cbv, prev):
    def conv(u, cw, cb, which):
        u1, u2 = prev(u, which)
        return cb + cw[0:1] * u2 + cw[1:2] * u1 + cw[2:3] * u
    cg = conv(ug, cwg, cbg, 0)
    cv = conv(uv, cwv, cbv, 1)
    return cg * _sigmoid(cg) * cv


FFN_SUB = 256


def _ffn_up_prompt_kernel(h_ref, wg_ref, wv_ref, cwg_ref, cwv_ref, cbg_ref, cbv_ref,
                          act_ref, tg_ref, tv_ref, u_sc, w_sc, *, tiles_per_seq):
    i = pl.program_id(1)
    first = (i % tiles_per_seq) == 0
    hb = h_ref[...]
    tm = hb.shape[0]

    @pl.when(i == 0)
    def _cast_weights():
        w_sc[0] = wg_ref[...].astype(BF16)
        w_sc[1] = wv_ref[...].astype(BF16)

    @pl.when(first)
    def _reset():
        u_sc[:, 0:8, :] = jnp.zeros((2, 8, u_sc.shape[2]), F32)

    def conv_half(which, u, cw_ref, cb_ref, t_ref, cs):
        cw = cw_ref[:, cs]
        u_sc[which, 8:8 + tm, cs] = u
        u1 = u_sc[which, 7:7 + tm, cs]
        u2 = u_sc[which, 6:6 + tm, cs]
        conv = cb_ref[:, cs] + cw[0:1] * u2 + cw[1:2] * u1 + cw[2:3] * u
        u_sc[which, 0:8, cs] = u[tm - 8:tm]
        t_ref[0, :, cs] = u[tm - 8:tm]
        return conv

    subs = [slice(c0, c0 + FFN_SUB) for c0 in range(0, wg_ref.shape[1], FFN_SUB)]
    for cs in subs:
        cg = conv_half(0, _dot(hb, w_sc[0, :, cs]), cwg_ref, cbg_ref, tg_ref, cs)
        cv = conv_half(1, _dot(hb, w_sc[1, :, cs]), cwv_ref, cbv_ref, tv_ref, cs)
        act_ref[:, cs] = (cg * _sigmoid(cg) * cv).astype(BF16)


def _ffn_up_prompt(h2, w_up, layer, conv_w, conv_b, bsz, seq):
    m, d = h2.shape
    dff = w_up.shape[2] // 2
    tm, tn = 1024, 512
    assert seq % tm == 0 and dff % tn == 0 and tn % FFN_SUB == 0
    nj = dff // tn
    tps = seq // tm
    kern = functools.partial(_ffn_up_prompt_kernel, tiles_per_seq=tps)
    wspec = lambda off: pl.BlockSpec((None, d, tn), lambda j, i: (layer, 0, j + off))
    cspec = lambda r, off: pl.BlockSpec((r, tn), lambda j, i: (0, j + off))
    tail = pl.BlockSpec((1, 8, tn), lambda j, i: (i // tps, 0, j))
    cb = conv_b.reshape(1, -1)
    return pl.pallas_call(
        kern,
        out_shape=(jax.ShapeDtypeStruct((m, dff), BF16),
                   jax.ShapeDtypeStruct((bsz, 8, dff), F32),
                   jax.ShapeDtypeStruct((bsz, 8, dff), F32)),
        grid=(nj, m // tm),
        in_specs=[pl.BlockSpec((tm, d), lambda j, i: (i, 0)),
                  wspec(0), wspec(nj), cspec(3, 0), cspec(3, nj), cspec(1, 0), cspec(1, nj)],
        out_specs=(pl.BlockSpec((tm, tn), lambda j, i: (i, j)), tail, tail),
        scratch_shapes=[pltpu.VMEM((2, tm + 8, tn), F32), pltpu.VMEM((2, d, tn), BF16)],
        compiler_params=_cparams("parallel", "arbitrary"),
        name="ffn_up_prompt",
    )(h2, w_up, w_up, conv_w, conv_w, cb, cb)


def _ffn_up_sample_kernel(h_ref, wg_ref, wv_ref, cwg_ref, cwv_ref, cbg_ref, cbv_ref,
                          stg_ref, stv_ref, act_ref, ug_ref, uv_ref, *, seq):
    hb = h_ref[...]
    ug = _dot(hb, wg_ref[...].astype(BF16))
    uv = _dot(hb, wv_ref[...].astype(BF16))
    tm = ug.shape[0]
    row = lax.broadcasted_iota(jnp.int32, (tm, 1), 0)
    states = (stg_ref[...], stv_ref[...])

    def prev(u, which):
        st = states[which]
        u1 = pltpu.roll(u, 1, 0)
        u2 = pltpu.roll(u, 2, 0)
        for b in range(tm // seq):
            older, newer = st[2 * b:2 * b + 1], st[2 * b + 1:2 * b + 2]
            u1 = jnp.where(row == seq * b, newer, u1)
            u2 = jnp.where(row == seq * b, older, jnp.where(row == seq * b + 1, newer, u2))
        return u1, u2

    act_ref[...] = _conv_gate(ug, uv, cwg_ref[...], cwv_ref[...], cbg_ref[...], cbv_ref[...], prev).astype(BF16)
    ug_ref[...] = ug
    uv_ref[...] = uv


def _ffn_up_sample(h2, w_up, layer, conv_w, conv_b, conv_state, seq):
    m, d = h2.shape
    dff = w_up.shape[2] // 2
    tn = 512
    assert dff % tn == 0
    nj = dff // tn
    bsz = m // seq
    assert seq >= 2
    st = conv_state.reshape(2 * bsz, 2 * dff)
    kern = functools.partial(_ffn_up_sample_kernel, seq=seq)
    wspec = lambda off: pl.BlockSpec((None, d, tn), lambda j: (layer, 0, j + off))
    cspec = lambda r, off: pl.BlockSpec((r, tn), lambda j: (0, j + off))
    cb = conv_b.reshape(1, -1)
    ospec = pl.BlockSpec((m, tn), lambda j: (0, j))
    return pl.pallas_call(
        kern,
        out_shape=(jax.ShapeDtypeStruct((m, dff), BF16),
                   jax.ShapeDtypeStruct((m, dff), F32),
                   jax.ShapeDtypeStruct((m, dff), F32)),
        grid=(nj,),
        in_specs=[pl.BlockSpec((m, d), lambda j: (0, 0)),
                  wspec(0), wspec(nj), cspec(3, 0), cspec(3, nj), cspec(1, 0), cspec(1, nj),
                  cspec(2 * bsz, 0), cspec(2 * bsz, nj)],
        out_specs=(ospec, ospec, ospec),
        compiler_params=_cparams("parallel"),
        name="ffn_up_sample",
    )(h2, w_up, w_up, conv_w, conv_w, cb, cb, st, st)


def _layer(x, l, w, dims, attn_fn, mstate, gstate, conv_state, bsz, seq):
    (heads_a, dk_a, heads_b, heads_c, dk_c) = dims
    wa = heads_a * LANES
    wb = heads_b * LANES
    prompt = conv_state is None
    z, zs = _proj_in(x, w["norm_mix_g"][l], w["w_in"], 3 * wa + 4 * wb, w["w_tail"], w["w_small"], l)
    qn, kn, knb, vb, vr = _qknorm(z, w["q_norm_g"][l], w["k_norm_g"][l], heads_a, dk_a, seq,
                                  v_transposed=prompt)
    lam_init = 0.8 - 0.6 * math.exp(-0.3 * l)
    oa = attn_fn(l, qn, knb, vb, lam_init)

    chunk_b = SCAN_CHUNK if seq % SCAN_CHUNK == 0 else SAMPLE_CHUNK
    gates = zs[:, GATE_I:GATE_I + 2 * heads_b].reshape(bsz, seq, 2 * heads_b)
    gt = jnp.swapaxes(gates, 1, 2)
    if seq < chunk_b:
        gt = jnp.pad(gt, ((0, 0), (0, 0), (0, chunk_b - seq)))
    ob, c_f, n_f, m_f = _mlstm(z, zs, gt, w["mlstm_gate_b"][l], w["mlstm_norm_g"][l],
                               mstate[0], mstate[1], mstate[2], seq, chunk_b, (3 * wa) // wb)

    chunk_c = GLA_CHUNK if seq % GLA_CHUNK == 0 else SAMPLE_CHUNK
    kw = heads_c * dk_c
    vw = heads_c * LANES
    c0 = 3 * wa + 4 * wb
    oc, s_f = _gla(z, zs, w["gla_w_alpha"][l], w["gla_b_alpha"][l], w["gla_norm_g"][l], gstate,
                   seq, chunk_c, c0 // kw, c0 // kw + 1, (c0 + 2 * kw) // vw, (c0 + 2 * kw) // vw + 1)

    x1, h2 = _proj_out(oa, ob, oc, w["w_out"], l, x, w["norm_ffn_g"][l])
    if prompt:
        act, tg, tv = _ffn_up_prompt(h2, w["w_up"], l, w["ffn_conv_w"][l], w["ffn_conv_b"][l], bsz, seq)
        conv_rows = jnp.concatenate([tg[:, 6:8], tv[:, 6:8]], axis=-1)
    else:
        act, ug, uv = _ffn_up_sample(h2, w["w_up"], l, w["ffn_conv_w"][l], w["ffn_conv_b"][l], conv_state, seq)
        u = jnp.concatenate([ug, uv], axis=-1).reshape(bsz, seq, -1)
        conv_rows = u[:, seq - 2:]
    x2 = _proj_down(act, w["w_down"], l, x1)

    k_rows = kn.reshape(bsz, seq, heads_a, LANES)
    v_rows = vr.reshape(bsz, seq, heads_a, LANES)
    return x2, (k_rows, v_rows, c_f, n_f, m_f[:, :, 0], s_f, conv_rows)


def kernel(x_prompt, x_sample, cache_k, cache_v, page_table, state_mlstm_C, state_mlstm_n, state_mlstm_m, state_gla_S, state_ffn_conv, norm_mix_g, w_in, q_norm_g, k_norm_g, diff_lambda, diff_subln_g, rel_bias, mlstm_gate_b, mlstm_norm_g, gla_w_alpha, gla_b_alpha, gla_norm_g, w_out, norm_ffn_g, ffn_w_up, ffn_conv_w, ffn_conv_b, ffn_w_down):
    depth = w_in.shape[0]
    bp, sp, d_model = x_prompt.shape
    bs, ss, _ = x_sample.shape
    heads_a, dv_a = cache_v.shape[3], cache_v.shape[4]
    dk_a = cache_k.shape[4] // 2
    heads_b, dk_b, dv_b = state_mlstm_C.shape[2:]
    heads_c, dk_c, dv_c = state_gla_S.shape[2:]
    rank = gla_w_alpha.shape[1]
    page = cache_k.shape[2]
    assert dv_a == LANES and 2 * dk_a == LANES and dk_b == LANES and dv_b == LANES and dv_c == LANES
    assert page >= MAX_DISTANCE and ATTN_TILE >= MAX_DISTANCE and rank <= GATE_I
    wa, wb = heads_a * LANES, heads_b * LANES
    n_main = 3 * wa + 4 * wb + 2 * heads_c * dk_c + 2 * heads_c * dv_c
    gate0 = 3 * wa + 4 * wb
    c0 = gate0 + 2 * heads_b
    assert w_in.shape[2] == n_main + 2 * heads_b + rank

    w_tail = w_in[:, :, c0:c0 + n_main - gate0].astype(BF16)
    w_small = jnp.zeros((depth, d_model, LANES), F32)
    w_small = w_small.at[:, :, :rank].set(w_in[:, :, n_main + 2 * heads_b:])
    w_small = w_small.at[:, :, GATE_I:GATE_I + 2 * heads_b].set(w_in[:, :, gate0:c0]).astype(BF16)
    w = dict(norm_mix_g=norm_mix_g, w_in=w_in, w_tail=w_tail, w_small=w_small, q_norm_g=q_norm_g, k_norm_g=k_norm_g,
             mlstm_gate_b=mlstm_gate_b, mlstm_norm_g=mlstm_norm_g, gla_w_alpha=gla_w_alpha,
             gla_b_alpha=gla_b_alpha, gla_norm_g=gla_norm_g, w_out=w_out.astype(BF16),
             norm_ffn_g=norm_ffn_g, w_up=ffn_w_up, ffn_conv_w=ffn_conv_w,
             ffn_conv_b=ffn_conv_b, w_down=ffn_w_down)
    dims = (heads_a, dk_a, heads_b, heads_c, dk_c)

    bias_p = _prompt_bias(rel_bias, ATTN_TILE)
    bias_pg, bias_new = _decode_bias(rel_bias, page, DECODE_PAGES_PER_STEP, ss)
    kc = cache_k.reshape(depth, cache_k.shape[1], page * heads_a, LANES)
    vc = cache_v.reshape(depth, cache_v.shape[1], page * heads_a, LANES)

    def prompt_attn(l, qn, knb, vt, lam_init):
        shp = (bp, sp, wa)
        o = _prompt_attention(qn.reshape(shp), knb.reshape(shp), vt, bias_p,
                              diff_lambda[l], diff_subln_g[l], lam_init, dk_a)
        return o.reshape(bp * sp, wa)

    def sample_attn(l, qn, knb, vb, lam_init):
        shp = (bs, ss, wa)
        o = _decode_attention(l, qn.reshape(shp), knb.reshape(shp), vb.reshape(shp), kc, vc, page_table,
                              bias_pg, bias_new, diff_lambda[l], diff_subln_g[l], lam_init, dk_a)
        return o.reshape(bs * ss, wa)

    xp = x_prompt.reshape(bp * sp, d_model)
    xs = x_sample.reshape(bs * ss, d_model)
    zero_m = (jnp.zeros((bp, heads_b, dk_b, dv_b), F32), jnp.zeros((bp, heads_b, dk_b), F32),
              jnp.zeros((bp, heads_b), F32))
    zero_g = jnp.zeros((bp, heads_c, dk_c, dv_c), F32)
    rows_p, rows_s = [], []
    for l in range(depth):
        xp, rp = _layer(xp, l, w, dims, prompt_attn, zero_m, zero_g, None, bp, sp)
        rows_p.append(rp)
        xs, rs = _layer(xs, l, w, dims, sample_attn,
                        (state_mlstm_C[l], state_mlstm_n[l], state_mlstm_m[l]), state_gla_S[l],
                        state_ffn_conv[l], bs, ss)
        rows_s.append(rs)

    def field(rows, i):
        return jnp.stack([r[i] for r in rows], axis=0)

    return (xp.reshape(bp, sp, d_model), xs.reshape(bs, ss, d_model),
            *[field(rows_p, i) for i in range(7)], *[field(rows_s, i) for i in range(7)])
```

```python
import functools
import math

import numpy as np
import jax
import jax.numpy as jnp
from jax import lax
from jax.experimental import pallas as pl
from jax.experimental.pallas import tpu as pltpu

F32 = jnp.float32
BF16 = jnp.bfloat16

LANES = 128
VMEM_LIMIT = 52 * 1024 * 1024
EPS = 1e-6
NEG = -1e30
LOG2E = math.log2(math.e)
GLA_TAU = 16.0
MAX_DISTANCE = 128
ATTN_TILE = 256
SCAN_CHUNK = 256
GLA_CHUNK = 128
SAMPLE_CHUNK = 128
DECODE_PAGES_PER_STEP = 16


def _cparams(*sem):
    return pltpu.CompilerParams(dimension_semantics=sem, vmem_limit_bytes=VMEM_LIMIT)


def _split_bf16(x):
    hi = x.astype(BF16)
    lo = (x - hi.astype(F32)).astype(BF16)
    return hi, lo


def _dot(a, b):
    return jnp.dot(a, b, preferred_element_type=F32)


def _dot_nt(a, b):
    return lax.dot_general(a, b, (((1,), (1,)), ((), ())), preferred_element_type=F32)


def _dot_tn(a, b):
    return lax.dot_general(a, b, (((0,), (0,)), ((), ())), preferred_element_type=F32)


def _log_sigmoid(x):
    return jnp.minimum(x, 0.0) - jnp.log(1.0 + jnp.exp(-jnp.abs(x)))


def _sigmoid(x):
    return 1.0 / (1.0 + jnp.exp(-x))


def _div_pow2(x, n):
    assert n & (n - 1) == 0
    return lax.shift_right_logical(x, n.bit_length() - 1)


def _mod_pow2(x, n):
    assert n & (n - 1) == 0
    return x & (n - 1)


def _mixer_dtype(block_rows):
    return BF16 if block_rows % 16 == 0 else F32


def _pad_rows(x, rows):
    if x.shape[0] == rows:
        return x
    return jnp.concatenate([x, jnp.zeros((rows - x.shape[0], x.shape[1]), x.dtype)], axis=0)


def _proj_down_kernel(a_ref, w_ref, r_ref, o_ref, w_sc):
    @pl.when(pl.program_id(1) == 0)
    def _cast_weights():
        w_sc[...] = w_ref[...].astype(BF16)

    o_ref[...] = r_ref[...] + _dot(a_ref[...], w_sc[...])


def _proj_down(a, w, layer, res):
    m, k = a.shape
    n = w.shape[2]
    tm = min(m, 512)
    tn = 512
    assert m % tm == 0 and n % tn == 0
    return pl.pallas_call(
        _proj_down_kernel,
        out_shape=jax.ShapeDtypeStruct((m, n), F32),
        grid=(n // tn, m // tm),
        in_specs=[pl.BlockSpec((tm, k), lambda j, i: (i, 0)),
                  pl.BlockSpec((None, k, tn), lambda j, i: (layer, 0, j)),
                  pl.BlockSpec((tm, tn), lambda j, i: (i, j))],
        out_specs=pl.BlockSpec((tm, tn), lambda j, i: (i, j)),
        scratch_shapes=[pltpu.VMEM((k, tn), BF16)],
        compiler_params=_cparams("parallel", "arbitrary"),
        name="proj_down",
    )(a, w, res)


def _rms_rows(x, g):
    ms = jnp.mean(x * x, axis=-1, keepdims=True)
    return x * lax.rsqrt(ms + EPS) * g


def _proj_in_kernel(x_ref, g_ref, w_ref, ws_ref, z_ref, zs_ref, h_sc):
    @pl.when(pl.program_id(1) == 0)
    def _norm():
        h_sc[...] = _rms_rows(x_ref[...], g_ref[...]).astype(BF16)
        zs_ref[...] = _dot(h_sc[...], ws_ref[...])

    z_ref[...] = _dot(h_sc[...], w_ref[...])


def _proj_in(x, g, w_main, w_small, layer):
    m, d = x.shape
    n = w_main.shape[2]
    ns = w_small.shape[2]
    tm = min(m, 1024)
    tn = 512
    assert m % tm == 0 and n % tn == 0
    return pl.pallas_call(
        _proj_in_kernel,
        out_shape=(jax.ShapeDtypeStruct((m, n), F32), jax.ShapeDtypeStruct((m, ns), F32)),
        grid=(m // tm, n // tn),
        in_specs=[pl.BlockSpec((tm, d), lambda i, j: (i, 0)),
                  pl.BlockSpec((1, d), lambda i, j: (0, 0)),
                  pl.BlockSpec((None, d, tn), lambda i, j: (layer, 0, j)),
                  pl.BlockSpec((None, d, ns), lambda i, j: (layer, 0, 0))],
        out_specs=(pl.BlockSpec((tm, tn), lambda i, j: (i, j)),
                   pl.BlockSpec((tm, ns), lambda i, j: (i, 0))),
        scratch_shapes=[pltpu.VMEM((tm, d), BF16)],
        compiler_params=_cparams("parallel", "arbitrary"),
        name="proj_in",
    )(x, g.reshape(1, d), w_main, w_small)


def _proj_out_kernel(oa_ref, ob_ref, oc_ref, w_ref, r_ref, g_ref, x_ref, h_ref):
    ka, kb = oa_ref.shape[1], ob_ref.shape[1]
    acc = _dot(oa_ref[...].astype(BF16), w_ref[0:ka, :])
    acc += _dot(ob_ref[...].astype(BF16), w_ref[ka:ka + kb, :])
    acc += _dot(oc_ref[...].astype(BF16), w_ref[ka + kb:, :])
    x = r_ref[...] + acc
    x_ref[...] = x
    h_ref[...] = _rms_rows(x, g_ref[...]).astype(BF16)


def _proj_out(oa, ob, oc, w, layer, res, g):
    m = oa.shape[0]
    k, n = w.shape[1], w.shape[2]
    tm = min(m, 256)
    assert m % tm == 0 and oa.shape[1] + ob.shape[1] + oc.shape[1] == k
    rows = lambda width: pl.BlockSpec((tm, width), lambda i: (i, 0))
    return pl.pallas_call(
        _proj_out_kernel,
        out_shape=(jax.ShapeDtypeStruct((m, n), F32), jax.ShapeDtypeStruct((m, n), BF16)),
        grid=(m // tm,),
        in_specs=[rows(oa.shape[1]), rows(ob.shape[1]), rows(oc.shape[1]),
                  pl.BlockSpec((None, k, n), lambda i: (layer, 0, 0)),
                  rows(n),
                  pl.BlockSpec((1, n), lambda i: (0, 0))],
        out_specs=(rows(n), rows(n)),
        compiler_params=_cparams("parallel"),
        name="proj_out",
    )(oa, ob, oc, w, res, g.reshape(1, n))


def _qknorm_kernel(aq_ref, ak_ref, av_ref, qg_ref, kg_ref, bd_ref,
                   qn_ref, kn_ref, knb_ref, vb_ref, vr_ref, *, heads, inv_dk, scale, v_transposed):
    bd = bd_ref[...]
    qg = qg_ref[...]
    kg = kg_ref[...]

    def norm(x, g):
        hi, lo = _split_bf16(x * x)
        ss = _dot(hi, bd) + _dot(lo, bd)
        return x * lax.rsqrt(ss * inv_dk + EPS) * g

    for h in range(heads):
        sl = slice(LANES * h, LANES * (h + 1))
        qn_ref[:, sl] = (norm(aq_ref[:, sl], qg) * scale).astype(BF16)
        kn = norm(ak_ref[:, sl], kg)
        kn_ref[:, sl] = kn
        knb_ref[:, sl] = kn.astype(BF16)
        if v_transposed:
            vb_ref[0, sl, :] = av_ref[:, sl].T.astype(BF16)
    if not v_transposed:
        vb_ref[...] = av_ref[...].astype(BF16)
    vr_ref[...] = av_ref[...]


def _qknorm(z, qg, kg, heads, dk, seq, v_transposed):
    m = z.shape[0]
    w = heads * LANES
    tm = min(m, 256)
    assert m % tm == 0
    lane = np.arange(LANES)
    bd = jnp.asarray((lane[:, None] // dk) == (lane[None, :] // dk), BF16)
    reps = LANES // dk
    kern = functools.partial(_qknorm_kernel, heads=heads, inv_dk=1.0 / dk, scale=dk ** -0.5 * LOG2E,
                             v_transposed=v_transposed)
    rows = pl.BlockSpec((tm, w), lambda i: (i, 0))
    if v_transposed:
        tps = seq // tm
        v_shape = jax.ShapeDtypeStruct((m // seq, w, seq), BF16)
        v_spec = pl.BlockSpec((1, w, tm), lambda i: (i // tps, 0, i % tps))
    else:
        v_shape = jax.ShapeDtypeStruct((m, w), BF16)
        v_spec = rows
    return pl.pallas_call(
        kern,
        out_shape=(jax.ShapeDtypeStruct((m, w), BF16), jax.ShapeDtypeStruct((m, w), F32),
                   jax.ShapeDtypeStruct((m, w), BF16), v_shape, jax.ShapeDtypeStruct((m, w), F32)),
        grid=(m // tm,),
        in_specs=[pl.BlockSpec((tm, w), lambda i: (i, 0)),
                  pl.BlockSpec((tm, w), lambda i: (i, 1)),
                  pl.BlockSpec((tm, w), lambda i: (i, 2)),
                  pl.BlockSpec((1, LANES), lambda i: (0, 0)),
                  pl.BlockSpec((1, LANES), lambda i: (0, 0)),
                  pl.BlockSpec((LANES, LANES), lambda i: (0, 0))],
        out_specs=(rows, rows, rows, v_spec, rows),
        compiler_params=_cparams("parallel"),
        name="qknorm",
    )(z, z, z, jnp.tile(qg, reps).reshape(1, LANES), jnp.tile(kg, reps).reshape(1, LANES), bd)


def _bucket(n, n_buckets):
    max_exact = n_buckets // 2
    nf = jnp.maximum(n, max_exact).astype(F32)
    large = max_exact + (jnp.log(nf / max_exact) / math.log(MAX_DISTANCE / max_exact)
                         * (n_buckets - max_exact)).astype(jnp.int32)
    large = jnp.minimum(large, n_buckets - 1)
    return jnp.where(n < max_exact, jnp.maximum(n, 0), large)


def _prompt_bias_kernel(rb_ref, o_ref, *, tile, n_buckets):
    h = pl.program_id(0)
    j = lax.broadcasted_iota(jnp.int32, (tile, tile), 0)
    i = lax.broadcasted_iota(jnp.int32, (tile, tile), 1)
    far = jnp.full((tile, tile), MAX_DISTANCE, jnp.int32)
    for t, n in enumerate((i - j, tile + i - j, far)):
        b = _bucket(n, n_buckets)
        val = jnp.full((tile, tile), rb_ref[0, h], F32)
        for k in range(1, n_buckets):
            val = jnp.where(b == k, rb_ref[k, h], val)
        o_ref[0, t] = jnp.where(n >= 0, val * LOG2E, NEG)


def _prompt_bias(rel_bias, tile):
    n_buckets, heads = rel_bias.shape
    kern = functools.partial(_prompt_bias_kernel, tile=tile, n_buckets=n_buckets)
    return pl.pallas_call(
        kern,
        out_shape=jax.ShapeDtypeStruct((heads, 3, tile, tile), F32),
        grid=(heads,),
        in_specs=[pl.BlockSpec(memory_space=pltpu.SMEM)],
        out_specs=pl.BlockSpec((1, 3, tile, tile), lambda h: (h, 0, 0, 0)),
        compiler_params=_cparams("parallel"),
        name="prompt_bias",
    )(rel_bias)


def _decode_bias_kernel(rb_ref, pg_ref, new_ref, *, page, ppb, dec_seq, heads, n_buckets):
    grp = 2 * dec_seq

    def lookup(n, h):
        b = _bucket(n, n_buckets)
        val = jnp.full(n.shape, rb_ref[0, h], F32)
        for k in range(1, n_buckets):
            val = jnp.where(b == k, rb_ref[k, h], val)
        return jnp.where(n >= 0, val * LOG2E, NEG)

    qi = _mod_pow2(lax.broadcasted_iota(jnp.int32, (grp, page), 0), dec_seq)
    j = lax.broadcasted_iota(jnp.int32, (grp, page), 1)
    far = jnp.full((grp, page), MAX_DISTANCE, jnp.int32)
    for h in range(heads):
        rs = slice(grp * h, grp * (h + 1))
        far_b = lookup(far, h)
        for c in range(ppb):
            pg_ref[0, rs, page * c:page * (c + 1)] = far_b
            pg_ref[1, rs, page * c:page * (c + 1)] = far_b if c < ppb - 1 else lookup(page + qi - j, h)
        new_ref[rs, :] = lookup(jnp.where(j < dec_seq, qi - j, -1), h)


def _decode_bias(rel_bias, page, ppb, dec_seq):
    n_buckets, heads = rel_bias.shape
    rows = 2 * dec_seq * heads
    kern = functools.partial(_decode_bias_kernel, page=page, ppb=ppb, dec_seq=dec_seq, heads=heads,
                             n_buckets=n_buckets)
    return pl.pallas_call(
        kern,
        out_shape=(jax.ShapeDtypeStruct((2, rows, ppb * page), F32),
                   jax.ShapeDtypeStruct((rows, page), F32)),
        in_specs=[pl.BlockSpec(memory_space=pltpu.SMEM)],
        name="decode_bias",
    )(rel_bias)


def _diff_lambda(dl, lam_init):
    s1 = jnp.sum(dl[0:1] * dl[1:2], axis=-1, keepdims=True)
    s2 = jnp.sum(dl[2:3] * dl[3:4], axis=-1, keepdims=True)
    return jnp.exp(s1) - jnp.exp(s2) + lam_init


def _subln(o, g, lam_init):
    ms = jnp.mean(o * o, axis=-1, keepdims=True)
    return o * lax.rsqrt(ms + EPS) * g * (1.0 - lam_init)


def _flash_update(s, m_prev, l_prev):
    m_new = jnp.maximum(m_prev, jnp.max(s, axis=-1, keepdims=True))
    alpha = jnp.exp2(m_prev - m_new)
    p = jnp.exp2(s - m_new)
    l_new = alpha * l_prev + jnp.sum(p, axis=-1, keepdims=True)
    return p, alpha, m_new, l_new


def _attn_kernel(qt_ref, kt_ref, q_ref, k_ref, vt_ref, bias_ref, dl_ref, g_ref, o_ref, m_sc, l_sc, acc_sc,
                 *, heads, tile, dk, lam_init):
    qi = qt_ref[pl.program_id(1)]
    ki = kt_ref[pl.program_id(1)]
    n_sub = tile // LANES

    @pl.when(ki == 0)
    def _init():
        m_sc[...] = jnp.full(m_sc.shape, -jnp.inf, F32)
        l_sc[...] = jnp.zeros(l_sc.shape, F32)
        acc_sc[...] = jnp.zeros(acc_sc.shape, F32)

    lane = lax.broadcasted_iota(jnp.int32, (LANES, LANES), 1)
    for h in range(heads):
        sl = slice(LANES * h, LANES * (h + 1))
        kh = k_ref[0, :, sl]
        vht = vt_ref[0, sl, :]
        for c in range(2 * n_sub):
            qrows = slice(LANES * (c % n_sub), LANES * (c % n_sub + 1))
            cs = slice(LANES * c, LANES * (c + 1))
            qc = q_ref[0, qrows, sl]
            own_map = (lane < dk) if c < n_sub else (lane >= dk)
            qc = jnp.where(own_map, qc, jnp.zeros_like(qc))
            s = _dot_nt(kh, qc) + bias_ref[h, 0, :, qrows]
            m_prev = m_sc[h, :, cs]
            m_new = jnp.maximum(m_prev, jnp.max(s, axis=0, keepdims=True))
            alpha = jnp.exp2(m_prev - m_new)
            p = jnp.exp2(s - m_new)
            l_sc[h, :, cs] = alpha * l_sc[h, :, cs] + jnp.sum(p, axis=0, keepdims=True)
            acc_sc[h, :, cs] = alpha * acc_sc[h, :, cs] + _dot(vht, p.astype(BF16))
            m_sc[h, :, cs] = m_new

    @pl.when(ki == qi)
    def _finish():
        lam = _diff_lambda(dl_ref[...], lam_init)
        g_col = g_ref[...]
        for h in range(heads):
            acc = acc_sc[h]
            l = l_sc[h]
            ot = acc[:, :tile] / l[:, :tile] - lam * (acc[:, tile:] / l[:, tile:])
            ms = jnp.mean(ot * ot, axis=0, keepdims=True)
            ot = ot * lax.rsqrt(ms + EPS) * g_col * (1.0 - lam_init)
            o_ref[0, :, LANES * h:LANES * (h + 1)] = ot.T.astype(o_ref.dtype)


def _prompt_attention(qn, knb, vt, bias, dl, g, lam_init, dk):
    bsz, seq, w = qn.shape
    heads = w // LANES
    tile = bias.shape[-1]
    nq = seq // tile
    assert seq % tile == 0 and tile % LANES == 0
    kern = functools.partial(_attn_kernel, heads=heads, tile=tile, dk=dk, lam_init=lam_init)
    pairs = [(qi, ki) for qi in range(nq) for ki in range(qi + 1)]
    q_tab = jnp.asarray([p[0] for p in pairs], jnp.int32)
    k_tab = jnp.asarray([p[1] for p in pairs], jnp.int32)

    def bias_idx(b, t, qt, kt):
        return (0, jnp.where(kt[t] == qt[t], 0, jnp.where(kt[t] == qt[t] - 1, 1, 2)), 0, 0)

    grid_spec = pltpu.PrefetchScalarGridSpec(
        num_scalar_prefetch=2,
        grid=(bsz, len(pairs)),
        in_specs=[pl.BlockSpec((1, tile, w), lambda b, t, qt, kt: (b, qt[t], 0)),
                  pl.BlockSpec((1, tile, w), lambda b, t, qt, kt: (b, kt[t], 0)),
                  pl.BlockSpec((1, w, tile), lambda b, t, qt, kt: (b, 0, kt[t])),
                  pl.BlockSpec((heads, 1, tile, tile), bias_idx),
                  pl.BlockSpec(dl.shape, lambda b, t, qt, kt: (0, 0)),
                  pl.BlockSpec((LANES, 1), lambda b, t, qt, kt: (0, 0))],
        out_specs=pl.BlockSpec((1, tile, w), lambda b, t, qt, kt: (b, qt[t], 0)),
        scratch_shapes=[pltpu.VMEM((heads, 1, 2 * tile), F32),
                        pltpu.VMEM((heads, 1, 2 * tile), F32),
                        pltpu.VMEM((heads, LANES, 2 * tile), F32)],
    )
    return pl.pallas_call(
        kern,
        out_shape=jax.ShapeDtypeStruct((bsz, seq, w), BF16),
        grid_spec=grid_spec,
        compiler_params=_cparams("parallel", "arbitrary"),
        name="prompt_attention",
    )(q_tab, k_tab, qn, knb, vt, bias, dl, g.reshape(LANES, 1))


def _decode_attn_kernel(pt_ref, q_ref, kn_ref, vn_ref, *refs, heads, dec_seq, page, ppb, dk, lam_init):
    del pt_ref
    kc_refs, vc_refs = refs[:ppb], refs[ppb:2 * ppb]
    bias_ref, bnew_ref, dl_ref, g_ref, o_ref, wq_sc, wqb_sc, m_sc, l_sc, acc_sc = refs[2 * ppb:]
    p_idx = pl.program_id(1)
    rows = 2 * dec_seq * heads
    grp = 2 * dec_seq
    width = heads * LANES

    @pl.when(p_idx == 0)
    def _init():
        q = q_ref[0].astype(F32)
        lane = lax.broadcasted_iota(jnp.int32, (dec_seq, LANES), 1)
        pieces = []
        for h in range(heads):
            qh = q[:, LANES * h:LANES * (h + 1)]
            pieces += [jnp.where(lane < dk, qh, 0.0), jnp.where(lane >= dk, qh, 0.0)]
        wq = jnp.concatenate(pieces, axis=0)
        wq_sc[...] = wq.astype(BF16)
        c = lax.broadcasted_iota(jnp.int32, (rows, width), 1)
        r = lax.broadcasted_iota(jnp.int32, (rows, width), 0)
        own = _div_pow2(r, grp) == _div_pow2(c, LANES)
        wqb_sc[...] = jnp.where(own, jnp.concatenate([wq] * heads, axis=1), 0.0).astype(BF16)
        m_sc[...] = jnp.full(m_sc.shape, -jnp.inf, F32)
        l_sc[...] = jnp.zeros(l_sc.shape, F32)
        acc_sc[...] = jnp.zeros(acc_sc.shape, F32)

    def head_rows(ref, h):
        return ref[pl.ds(h, page, stride=heads), :].astype(BF16)

    s = jnp.concatenate(
        [jnp.concatenate([_dot_nt(wq_sc[grp * h:grp * (h + 1), :], head_rows(kc, h)) for h in range(heads)],
                         axis=0) for kc in kc_refs], axis=1) + bias_ref[0]
    p, alpha, m_new, l_new = _flash_update(s, m_sc[...], l_sc[...])
    m_sc[...] = m_new
    l_sc[...] = l_new
    pb = p.astype(BF16)
    for h in range(heads):
        rs = slice(grp * h, grp * (h + 1))
        acc = alpha[rs] * acc_sc[rs, :]
        for j, vc in enumerate(vc_refs):
            acc += _dot(pb[rs, page * j:page * (j + 1)], head_rows(vc, h))
        acc_sc[rs, :] = acc

    @pl.when(p_idx == pl.num_programs(1) - 1)
    def _finish():
        kn = _pad_rows(kn_ref[0].astype(F32), page).astype(BF16)
        vn = _pad_rows(vn_ref[0].astype(F32), page)
        s = _dot_nt(wqb_sc[...], kn) + bnew_ref[...]
        p, alpha, _, l_fin = _flash_update(s, m_sc[...], l_sc[...])
        pb = p.astype(BF16)
        lam = _diff_lambda(dl_ref[...], lam_init)
        g = g_ref[...]
        for h in range(heads):
            rs = slice(grp * h, grp * (h + 1))
            vh = vn[:, LANES * h:LANES * (h + 1)].astype(BF16)
            acc = alpha[rs] * acc_sc[rs, :] + _dot(pb[rs], vh)
            l = l_fin[rs]
            o = acc[:dec_seq] / l[:dec_seq] - lam * (acc[dec_seq:] / l[dec_seq:])
            o_ref[0, :, LANES * h:LANES * (h + 1)] = _subln(o, g, lam_init)


def _decode_attention(layer, qn, knb, vb, cache_k, cache_v, page_table, bias_pg, bias_new, dl, g, lam_init, dk):
    bsz, dec_seq, w = qn.shape
    heads = w // LANES
    page = cache_k.shape[2] // heads
    ppb = bias_pg.shape[-1] // page
    n_steps = page_table.shape[1] // ppb
    assert page_table.shape[1] % ppb == 0
    rows = 2 * dec_seq * heads
    kern = functools.partial(_decode_attn_kernel, heads=heads, dec_seq=dec_seq, page=page, ppb=ppb, dk=dk,
                             lam_init=lam_init)
    new_spec = pl.BlockSpec((1, dec_seq, w), lambda b, p, pt: (b, 0, 0))
    cache_specs = [pl.BlockSpec((None, None, page * heads, LANES),
                                lambda b, p, pt, j=j: (layer, pt[b, p * ppb + j], 0, 0)) for j in range(ppb)]
    grid_spec = pltpu.PrefetchScalarGridSpec(
        num_scalar_prefetch=1,
        grid=(bsz, n_steps),
        in_specs=[new_spec, new_spec, new_spec, *cache_specs, *cache_specs,
                  pl.BlockSpec((1, rows, ppb * page),
                               lambda b, p, pt: (jnp.where(p == n_steps - 1, 1, 0), 0, 0)),
                  pl.BlockSpec((rows, page), lambda b, p, pt: (0, 0)),
                  pl.BlockSpec(dl.shape, lambda b, p, pt: (0, 0)),
                  pl.BlockSpec((1, LANES), lambda b, p, pt: (0, 0))],
        out_specs=pl.BlockSpec((1, dec_seq, w), lambda b, p, pt: (b, 0, 0)),
        scratch_shapes=[pltpu.VMEM((rows, LANES), BF16),
                        pltpu.VMEM((rows, w), BF16),
                        pltpu.VMEM((rows, 1), F32),
                        pltpu.VMEM((rows, 1), F32),
                        pltpu.VMEM((rows, LANES), F32)],
    )
    return pl.pallas_call(
        kern,
        out_shape=jax.ShapeDtypeStruct((bsz, dec_seq, w), F32),
        grid_spec=grid_spec,
        compiler_params=_cparams("parallel", "arbitrary"),
        name="decode_attention",
    )(page_table, qn, knb, vb, *([cache_k] * ppb), *([cache_v] * ppb), bias_pg, bias_new, dl,
      g.reshape(1, LANES))


GATE_I = 16
GATE_F = 20


def _mlstm_kernel(q_ref, k_ref, v_ref, og_ref, zs_ref, gt_ref, gbl_ref, gbc_ref, g_ref,
                  c0_ref, n0_ref, m0_ref,
                  o_ref, cf_ref, nf_ref, mf_ref, c_sc, n_sc, m_sc,
                  *, heads, chunk, valid, scale):
    c_idx = pl.program_id(1)

    @pl.when(c_idx == 0)
    def _init():
        for h in range(heads):
            c_sc[h] = c0_ref[0, h].T
        n_sc[...] = n0_ref[0]
        m_sc[...] = m0_ref[0]

    rows_in = q_ref.shape[0]
    row = lax.broadcasted_iota(jnp.int32, (chunk, 1), 0)
    s_i = lax.broadcasted_iota(jnp.int32, (chunk, chunk), 0)
    t_i = lax.broadcasted_iota(jnp.int32, (chunk, chunk), 1)
    causal = s_i <= t_i
    incl = causal.astype(BF16)
    incl_t = (s_i >= t_i).astype(BF16)

    gcol = _pad_rows(zs_ref[...], chunk) + gbl_ref[...]
    lf_mat = _log_sigmoid(gcol)
    if valid < chunk:
        lf_mat = jnp.where(row < valid, lf_mat, 0.0)
    hi, lo = _split_bf16(lf_mat)
    b_mat = _dot(incl_t, hi) + _dot(incl_t, lo)

    grow = gt_ref[0] + gbc_ref[...]
    col = lax.broadcasted_iota(jnp.int32, grow.shape, 1)
    grow_id = lax.broadcasted_iota(jnp.int32, grow.shape, 0)
    lf_rows = jnp.where(grow_id >= heads, _log_sigmoid(grow), 0.0)
    if valid < chunk:
        lf_rows = jnp.where(col < valid, lf_rows, 0.0)
    hi, lo = _split_bf16(lf_rows)
    b_rows = _dot(hi, incl) + _dot(lo, incl)

    g = g_ref[...]
    first_row = lax.broadcasted_iota(jnp.int32, (16, 1), 0) == 0
    for h in range(heads):
        sl = slice(LANES * h, LANES * (h + 1))
        li_col = gcol[:, GATE_I + h:GATE_I + h + 1]
        if valid < chunk:
            li_col = jnp.where(row < valid, li_col, NEG)
        u_col = li_col - b_mat[:, GATE_F + h:GATE_F + h + 1]
        b_row = b_rows[heads + h:heads + h + 1, :]
        dt = jnp.where(causal, b_row + u_col, NEG)
        m_prev = m_sc[h:h + 1, 0:1]
        inter = b_row + m_prev
        mt = jnp.maximum(inter, jnp.max(dt, axis=0, keepdims=True))
        wi = jnp.exp(inter - mt)
        q = _pad_rows(q_ref[:, sl], chunk)
        ks = _pad_rows(k_ref[:, sl], chunk) * scale
        vtb = _pad_rows(v_ref[:, sl], chunk).T.astype(BF16)
        qb = q.astype(BF16)
        pt = _dot_nt(ks.astype(BF16), qb) * jnp.exp(dt - mt)
        ct_prev = c_sc[h]
        n_prev = n_sc[h:h + 1, :]
        num = wi * _dot_nt(ct_prev.astype(BF16), qb) + _dot(vtb, pt.astype(BF16))
        n_rows = jnp.where(first_row, jnp.broadcast_to(n_prev, (16, LANES)), 0.0)
        qn = _dot_nt(n_rows.astype(BF16), qb)[0:1]
        den = wi * qn + jnp.sum(pt, axis=0, keepdims=True)
        ht = num / jnp.maximum(jnp.abs(den), jnp.exp(-mt))
        og = _sigmoid(og_ref[:, sl])
        y = og * ht.T[:rows_in]
        ms = jnp.mean(y * y, axis=-1, keepdims=True)
        o_ref[:, sl] = (y * lax.rsqrt(ms + EPS) * g).astype(o_ref.dtype)

        m_new = mt[:, chunk - 1:chunk]
        b_last = b_row[:, chunk - 1:chunk]
        a = jnp.exp(b_last + m_prev - m_new)
        kw = ks * jnp.exp(u_col + (b_last - m_new))
        c_sc[h] = a * ct_prev + _dot(vtb, kw.astype(BF16))
        n_sc[h:h + 1, :] = a * n_prev + jnp.sum(kw, axis=0, keepdims=True)
        m_sc[h:h + 1, :] = jnp.broadcast_to(m_new, (1, LANES))

    @pl.when(c_idx == pl.num_programs(1) - 1)
    def _finish():
        for h in range(heads):
            cf_ref[0, h] = c_sc[h].T
        nf_ref[0] = n_sc[...]
        mf_ref[0] = m_sc[...]


def _mlstm(z, zs, gt, gate_b, g, c0, n0, m0, seq, chunk, col0):
    bsz, heads = c0.shape[0], c0.shape[1]
    w = heads * LANES
    rows_in = min(seq, chunk)
    nc = seq // rows_in
    valid = rows_in
    gt_w = gt.shape[-1] // nc
    kern = functools.partial(_mlstm_kernel, heads=heads, chunk=chunk, valid=valid, scale=LANES ** -0.5)
    gbl = jnp.zeros((1, LANES), F32)
    gbl = gbl.at[0, GATE_I:GATE_I + heads].set(gate_b[0]).at[0, GATE_F:GATE_F + heads].set(gate_b[1])
    gbc = gate_b.reshape(2 * heads, 1)
    m0b = jnp.broadcast_to(m0[:, :, None], (bsz, heads, LANES))

    def zspec(blk):
        return pl.BlockSpec((rows_in, w), lambda b, c: (b * nc + c, col0 + blk))

    state = lambda shape: pl.BlockSpec((1,) + shape, lambda b, c: (b,) + (0,) * len(shape))
    return pl.pallas_call(
        kern,
        out_shape=(jax.ShapeDtypeStruct((bsz * seq, w), _mixer_dtype(rows_in)),
                   jax.ShapeDtypeStruct((bsz, heads, LANES, LANES), F32),
                   jax.ShapeDtypeStruct((bsz, heads, LANES), F32),
                   jax.ShapeDtypeStruct((bsz, heads, LANES), F32)),
        grid=(bsz, nc),
        in_specs=[zspec(0), zspec(1), zspec(2), zspec(3),
                  pl.BlockSpec((rows_in, LANES), lambda b, c: (b * nc + c, 0)),
                  pl.BlockSpec((1, 2 * heads, gt_w), lambda b, c: (b, 0, c)),
                  pl.BlockSpec((1, LANES), lambda b, c: (0, 0)),
                  pl.BlockSpec((2 * heads, 1), lambda b, c: (0, 0)),
                  pl.BlockSpec((1, LANES), lambda b, c: (0, 0)),
                  state((heads, LANES, LANES)), state((heads, LANES)), state((heads, LANES))],
        out_specs=(pl.BlockSpec((rows_in, w), lambda b, c: (b * nc + c, 0)),
                   state((heads, LANES, LANES)), state((heads, LANES)), state((heads, LANES))),
        scratch_shapes=[pltpu.VMEM((heads, LANES, LANES), F32),
                        pltpu.VMEM((heads, LANES), F32),
                        pltpu.VMEM((heads, LANES), F32)],
        compiler_params=_cparams("parallel", "arbitrary"),
        name="mlstm",
    )(z, z, z, z, zs, gt, gbl, gbc, g.reshape(1, LANES), c0, n0, m0b)


def _gla_levels(chunk):
    n, out = chunk, []
    while n >= 2:
        out.append(n)
        n //= 2
    return out


def _gla_weights(chunk):
    t = np.arange(chunk)[:, None]
    s = np.arange(chunk)[None, :]
    blocks = [(s <= t).astype(np.float32), (s > t).astype(np.float32)]
    for n in _gla_levels(chunk):
        mid = (t // n) * n + n // 2 - 1
        blocks.append(((s > mid) & (s <= t)).astype(np.float32) - ((s > t) & (s <= mid)).astype(np.float32))
    return jnp.asarray(np.concatenate(blocks, axis=0), BF16)


def _gla_kernel(q_ref, k_ref, v_ref, gg_ref, zs_ref, wa_ref, ba_ref, ws_ref, g_ref, s0_ref,
                o_ref, sf_ref, s_sc, *, heads, chunk, valid, dk, scale):
    c_idx = pl.program_id(1)
    per = LANES // dk
    assert per == 2 and heads % per == 0

    @pl.when(c_idx == 0)
    def _init():
        for h in range(heads):
            s_sc[h // per, dk * (h % per):dk * (h % per + 1), :] = s0_ref[0, h]

    rows_in = q_ref.shape[0]
    row = lax.broadcasted_iota(jnp.int32, (chunk, 1), 0)
    t_i = lax.broadcasted_iota(jnp.int32, (per * chunk, chunk), 0) & (chunk - 1)
    s_i = lax.broadcasted_iota(jnp.int32, (per * chunk, chunk), 1)
    first_head = lax.broadcasted_iota(jnp.int32, (chunk, LANES), 1) < dk

    zs = _pad_rows(zs_ref[...], chunk)
    la = _log_sigmoid(_dot(zs.astype(BF16), wa_ref[...]) + ba_ref[...]) * (1.0 / GLA_TAU)
    if valid < chunk:
        la = jnp.where(row < valid, la, 0.0)
    hi, lo = _split_bf16(la)
    wst = ws_ref[...]
    e_all = _dot(wst, hi) + _dot(wst, lo)
    ones = jnp.ones((chunk, LANES), BF16)
    levels = _gla_levels(chunk)
    g = g_ref[...]
    second_half = [(row & (n - 1)) >= n // 2 for n in levels]
    same_node = [_div_pow2(t_i, n) == _div_pow2(s_i, n) for n in levels]

    def split_heads(x):
        return [jnp.where(first_head, x, 0.0), jnp.where(first_head, 0.0, x)]

    for j in range(heads // per):
        psl = slice(LANES * j, LANES * (j + 1))
        q = _pad_rows(q_ref[:, psl], chunk) * scale
        k = _pad_rows(k_ref[:, psl], chunk)
        bc = e_all[0:chunk, psl]
        rem = e_all[chunk:2 * chunk, psl]
        a_pair = jnp.zeros((per * chunk, chunk), F32)
        for li in range(len(levels)):
            e = e_all[(2 + li) * chunk:(3 + li) * chunk, psl]
            qt = jnp.where(second_half[li], q * jnp.exp(jnp.minimum(e, 0.0)), 0.0)
            kt = jnp.where(second_half[li], 0.0, k * jnp.exp(jnp.minimum(-e, 0.0)))
            qs = jnp.concatenate(split_heads(qt), axis=0).astype(BF16)
            a_pair = a_pair + jnp.where(same_node[li], _dot_nt(qs, kt.astype(BF16)), 0.0)
        s_prev = s_sc[j]
        s_prev_b = s_prev.astype(BF16)
        qd = split_heads(q * jnp.exp(bc))
        qk = split_heads(q * k)
        kd = split_heads(k * jnp.exp(rem))
        hi_p, lo_p = _split_bf16(la[:, psl])
        tot = _dot_tn(hi_p, ones) + _dot_tn(lo_p, ones)
        s_new = jnp.exp(tot) * s_prev
        for hh in range(per):
            h = per * j + hh
            vsl = slice(LANES * h, LANES * (h + 1))
            vb = _pad_rows(v_ref[:, vsl], chunk).astype(BF16)
            a_h = a_pair[chunk * hh:chunk * (hh + 1)]
            o = _dot(qd[hh].astype(BF16), s_prev_b) + _dot(a_h.astype(BF16), vb)
            o = o + jnp.sum(qk[hh], axis=-1, keepdims=True) * vb.astype(F32)
            o = o[:rows_in]
            ms = jnp.mean(o * o, axis=-1, keepdims=True)
            gate = gg_ref[:, vsl]
            o_ref[:, vsl] = (o * lax.rsqrt(ms + EPS) * g * (gate * _sigmoid(gate))).astype(o_ref.dtype)
            s_new = s_new + _dot_tn(kd[hh].astype(BF16), vb)
        s_sc[j] = s_new

    @pl.when(c_idx == pl.num_programs(1) - 1)
    def _finish():
        for h in range(heads):
            sf_ref[0, h] = s_sc[h // per, dk * (h % per):dk * (h % per + 1), :]


def _gla(z, zs, w_alpha, b_alpha, g, s0, seq, chunk, qcol, kcol, vcol, gcol):
    bsz, heads, dk, dv = s0.shape
    rows_in = min(seq, chunk)
    nc = seq // rows_in
    kw = heads * dk
    vw = heads * dv
    wa = jnp.zeros((LANES, kw), F32).at[:w_alpha.shape[0]].set(w_alpha).astype(BF16)
    wst = _gla_weights(chunk)
    kern = functools.partial(_gla_kernel, heads=heads, chunk=chunk, valid=rows_in, dk=dk, scale=dk ** -0.5)
    const = lambda shape: pl.BlockSpec(shape, lambda b, c: (0,) * len(shape))
    return pl.pallas_call(
        kern,
        out_shape=(jax.ShapeDtypeStruct((bsz * seq, vw), _mixer_dtype(rows_in)),
                   jax.ShapeDtypeStruct((bsz, heads, dk, dv), F32)),
        grid=(bsz, nc),
        in_specs=[pl.BlockSpec((rows_in, kw), lambda b, c: (b * nc + c, qcol)),
                  pl.BlockSpec((rows_in, kw), lambda b, c: (b * nc + c, kcol)),
                  pl.BlockSpec((rows_in, vw), lambda b, c: (b * nc + c, vcol)),
                  pl.BlockSpec((rows_in, vw), lambda b, c: (b * nc + c, gcol)),
                  pl.BlockSpec((rows_in, LANES), lambda b, c: (b * nc + c, 0)),
                  const((LANES, kw)), const((1, kw)), const(wst.shape), const((1, LANES)),
                  pl.BlockSpec((1, heads, dk, dv), lambda b, c: (b, 0, 0, 0))],
        out_specs=(pl.BlockSpec((rows_in, vw), lambda b, c: (b * nc + c, 0)),
                   pl.BlockSpec((1, heads, dk, dv), lambda b, c: (b, 0, 0, 0))),
        scratch_shapes=[pltpu.VMEM((kw // LANES, LANES, dv), F32)],
        compiler_params=_cparams("parallel", "arbitrary"),
        name="gla",
    )(z, z, z, z, zs, wa, b_alpha.reshape(1, kw), wst, g.reshape(1, LANES), s0)


def _conv_gate(ug, uv, cwg, cwv, cbg, cbv, prev):
    def conv(u, cw, cb, which):
        u1, u2 = prev(u, which)
        return cb + cw[0:1] * u2 + cw[1:2] * u1 + cw[2:3] * u
    cg = conv(ug, cwg, cbg, 0)
    cv = conv(uv, cwv, cbv, 1)
    return cg * _sigmoid(cg) * cv


FFN_SUB = 256


def _ffn_up_prompt_kernel(h_ref, wg_ref, wv_ref, cwg_ref, cwv_ref, cbg_ref, cbv_ref,
                          act_ref, tg_ref, tv_ref, u_sc, w_sc, *, tiles_per_seq):
    i = pl.program_id(1)
    first = (i % tiles_per_seq) == 0
    hb = h_ref[...]
    tm = hb.shape[0]

    @pl.when(i == 0)
    def _cast_weights():
        w_sc[0] = wg_ref[...].astype(BF16)
        w_sc[1] = wv_ref[...].astype(BF16)

    @pl.when(first)
    def _reset():
        u_sc[:, 0:8, :] = jnp.zeros((2, 8, u_sc.shape[2]), F32)

    def conv_half(which, u, cw_ref, cb_ref, t_ref, cs):
        cw = cw_ref[:, cs]
        u_sc[which, 8:8 + tm, cs] = u
        u1 = u_sc[which, 7:7 + tm, cs]
        u2 = u_sc[which, 6:6 + tm, cs]
        conv = cb_ref[:, cs] + cw[0:1] * u2 + cw[1:2] * u1 + cw[2:3] * u
        u_sc[which, 0:8, cs] = u[tm - 8:tm]
        t_ref[0, :, cs] = u[tm - 8:tm]
        return conv

    subs = [slice(c0, c0 + FFN_SUB) for c0 in range(0, wg_ref.shape[1], FFN_SUB)]
    for cs in subs:
        cg = conv_half(0, _dot(hb, w_sc[0, :, cs]), cwg_ref, cbg_ref, tg_ref, cs)
        cv = conv_half(1, _dot(hb, w_sc[1, :, cs]), cwv_ref, cbv_ref, tv_ref, cs)
        act_ref[:, cs] = (cg * _sigmoid(cg) * cv).astype(BF16)


def _ffn_up_prompt(h2, w_up, layer, conv_w, conv_b, bsz, seq):
    m, d = h2.shape
    dff = w_up.shape[2] // 2
    tm, tn = 1024, 512
    assert seq % tm == 0 and dff % tn == 0 and tn % FFN_SUB == 0
    nj = dff // tn
    tps = seq // tm
    kern = functools.partial(_ffn_up_prompt_kernel, tiles_per_seq=tps)
    wspec = lambda off: pl.BlockSpec((None, d, tn), lambda j, i: (layer, 0, j + off))
    cspec = lambda r, off: pl.BlockSpec((r, tn), lambda j, i: (0, j + off))
    tail = pl.BlockSpec((1, 8, tn), lambda j, i: (i // tps, 0, j))
    cb = conv_b.reshape(1, -1)
    return pl.pallas_call(
        kern,
        out_shape=(jax.ShapeDtypeStruct((m, dff), BF16),
                   jax.ShapeDtypeStruct((bsz, 8, dff), F32),
                   jax.ShapeDtypeStruct((bsz, 8, dff), F32)),
        grid=(nj, m // tm),
        in_specs=[pl.BlockSpec((tm, d), lambda j, i: (i, 0)),
                  wspec(0), wspec(nj), cspec(3, 0), cspec(3, nj), cspec(1, 0), cspec(1, nj)],
        out_specs=(pl.BlockSpec((tm, tn), lambda j, i: (i, j)), tail, tail),
        scratch_shapes=[pltpu.VMEM((2, tm + 8, tn), F32), pltpu.VMEM((2, d, tn), BF16)],
        compiler_params=_cparams("parallel", "arbitrary"),
        name="ffn_up_prompt",
    )(h2, w_up, w_up, conv_w, conv_w, cb, cb)


def _ffn_up_sample_kernel(h_ref, wg_ref, wv_ref, cwg_ref, cwv_ref, cbg_ref, cbv_ref,
                          stg_ref, stv_ref, act_ref, ug_ref, uv_ref, *, seq):
    hb = h_ref[...]
    ug = _dot(hb, wg_ref[...].astype(BF16))
    uv = _dot(hb, wv_ref[...].astype(BF16))
    tm = ug.shape[0]
    row = lax.broadcasted_iota(jnp.int32, (tm, 1), 0)
    states = (stg_ref[...], stv_ref[...])

    def prev(u, which):
        st = states[which]
        u1 = pltpu.roll(u, 1, 0)
        u2 = pltpu.roll(u, 2, 0)
        for b in range(tm // seq):
            older, newer = st[2 * b:2 * b + 1], st[2 * b + 1:2 * b + 2]
            u1 = jnp.where(row == seq * b, newer, u1)
            u2 = jnp.where(row == seq * b, older, jnp.where(row == seq * b + 1, newer, u2))
        return u1, u2

    act_ref[...] = _conv_gate(ug, uv, cwg_ref[...], cwv_ref[...], cbg_ref[...], cbv_ref[...], prev).astype(BF16)
    ug_ref[...] = ug
    uv_ref[...] = uv


def _ffn_up_sample(h2, w_up, layer, conv_w, conv_b, conv_state, seq):
    m, d = h2.shape
    dff = w_up.shape[2] // 2
    tn = 512
    assert dff % tn == 0
    nj = dff // tn
    bsz = m // seq
    assert seq >= 2
    st = conv_state.reshape(2 * bsz, 2 * dff)
    kern = functools.partial(_ffn_up_sample_kernel, seq=seq)
    wspec = lambda off: pl.BlockSpec((None, d, tn), lambda j: (layer, 0, j + off))
    cspec = lambda r, off: pl.BlockSpec((r, tn), lambda j: (0, j + off))
    cb = conv_b.reshape(1, -1)
    ospec = pl.BlockSpec((m, tn), lambda j: (0, j))
    return pl.pallas_call(
        kern,
        out_shape=(jax.ShapeDtypeStruct((m, dff), BF16),
                   jax.ShapeDtypeStruct((m, dff), F32),
                   jax.ShapeDtypeStruct((m, dff), F32)),
        grid=(nj,),
        in_specs=[pl.BlockSpec((m, d), lambda j: (0, 0)),
                  wspec(0), wspec(nj), cspec(3, 0), cspec(3, nj), cspec(1, 0), cspec(1, nj),
                  cspec(2 * bsz, 0), cspec(2 * bsz, nj)],
        out_specs=(ospec, ospec, ospec),
        compiler_params=_cparams("parallel"),
        name="ffn_up_sample",
    )(h2, w_up, w_up, conv_w, conv_w, cb, cb, st, st)


def _layer(x, l, w, dims, attn_fn, mstate, gstate, conv_state, bsz, seq):
    (heads_a, dk_a, heads_b, heads_c, dk_c) = dims
    wa = heads_a * LANES
    wb = heads_b * LANES
    prompt = conv_state is None
    z, zs = _proj_in(x, w["norm_mix_g"][l], w["w_main"], w["w_small"], l)
    qn, kn, knb, vb, vr = _qknorm(z, w["q_norm_g"][l], w["k_norm_g"][l], heads_a, dk_a, seq,
                                  v_transposed=prompt)
    lam_init = 0.8 - 0.6 * math.exp(-0.3 * l)
    oa = attn_fn(l, qn, knb, vb, lam_init)

    chunk_b = SCAN_CHUNK if seq % SCAN_CHUNK == 0 else SAMPLE_CHUNK
    gates = zs[:, GATE_I:GATE_I + 2 * heads_b].reshape(bsz, seq, 2 * heads_b)
    gt = jnp.swapaxes(gates, 1, 2)
    if seq < chunk_b:
        gt = jnp.pad(gt, ((0, 0), (0, 0), (0, chunk_b - seq)))
    ob, c_f, n_f, m_f = _mlstm(z, zs, gt, w["mlstm_gate_b"][l], w["mlstm_norm_g"][l],
                               mstate[0], mstate[1], mstate[2], seq, chunk_b, (3 * wa) // wb)

    chunk_c = GLA_CHUNK if seq % GLA_CHUNK == 0 else SAMPLE_CHUNK
    kw = heads_c * dk_c
    vw = heads_c * LANES
    c0 = 3 * wa + 4 * wb
    oc, s_f = _gla(z, zs, w["gla_w_alpha"][l], w["gla_b_alpha"][l], w["gla_norm_g"][l], gstate,
                   seq, chunk_c, c0 // kw, c0 // kw + 1, (c0 + 2 * kw) // vw, (c0 + 2 * kw) // vw + 1)

    x1, h2 = _proj_out(oa, ob, oc, w["w_out"], l, x, w["norm_ffn_g"][l])
    if prompt:
        act, tg, tv = _ffn_up_prompt(h2, w["w_up"], l, w["ffn_conv_w"][l], w["ffn_conv_b"][l], bsz, seq)
        conv_rows = jnp.concatenate([tg[:, 6:8], tv[:, 6:8]], axis=-1)
    else:
        act, ug, uv = _ffn_up_sample(h2, w["w_up"], l, w["ffn_conv_w"][l], w["ffn_conv_b"][l], conv_state, seq)
        u = jnp.concatenate([ug, uv], axis=-1).reshape(bsz, seq, -1)
        conv_rows = u[:, seq - 2:]
    x2 = _proj_down(act, w["w_down"], l, x1)

    k_rows = kn.reshape(bsz, seq, heads_a, LANES)
    v_rows = vr.reshape(bsz, seq, heads_a, LANES)
    return x2, (k_rows, v_rows, c_f, n_f, m_f[:, :, 0], s_f, conv_rows)


def kernel(x_prompt, x_sample, cache_k, cache_v, page_table, state_mlstm_C, state_mlstm_n, state_mlstm_m, state_gla_S, state_ffn_conv, norm_mix_g, w_in, q_norm_g, k_norm_g, diff_lambda, diff_subln_g, rel_bias, mlstm_gate_b, mlstm_norm_g, gla_w_alpha, gla_b_alpha, gla_norm_g, w_out, norm_ffn_g, ffn_w_up, ffn_conv_w, ffn_conv_b, ffn_w_down):
    depth = w_in.shape[0]
    bp, sp, d_model = x_prompt.shape
    bs, ss, _ = x_sample.shape
    heads_a, dv_a = cache_v.shape[3], cache_v.shape[4]
    dk_a = cache_k.shape[4] // 2
    heads_b, dk_b, dv_b = state_mlstm_C.shape[2:]
    heads_c, dk_c, dv_c = state_gla_S.shape[2:]
    rank = gla_w_alpha.shape[1]
    page = cache_k.shape[2]
    assert dv_a == LANES and 2 * dk_a == LANES and dk_b == LANES and dv_b == LANES and dv_c == LANES
    assert page >= MAX_DISTANCE and ATTN_TILE >= MAX_DISTANCE and rank <= GATE_I
    wa, wb = heads_a * LANES, heads_b * LANES
    n_main = 3 * wa + 4 * wb + 2 * heads_c * dk_c + 2 * heads_c * dv_c
    gate0 = 3 * wa + 4 * wb
    c0 = gate0 + 2 * heads_b
    assert w_in.shape[2] == n_main + 2 * heads_b + rank

    w_main = jnp.concatenate([w_in[:, :, :gate0], w_in[:, :, c0:c0 + n_main - gate0]], axis=-1).astype(BF16)
    w_small = jnp.zeros((depth, d_model, LANES), F32)
    w_small = w_small.at[:, :, :rank].set(w_in[:, :, n_main + 2 * heads_b:])
    w_small = w_small.at[:, :, GATE_I:GATE_I + 2 * heads_b].set(w_in[:, :, gate0:c0]).astype(BF16)
    w = dict(norm_mix_g=norm_mix_g, w_main=w_main, w_small=w_small, q_norm_g=q_norm_g, k_norm_g=k_norm_g,
             mlstm_gate_b=mlstm_gate_b, mlstm_norm_g=mlstm_norm_g, gla_w_alpha=gla_w_alpha,
             gla_b_alpha=gla_b_alpha, gla_norm_g=gla_norm_g, w_out=w_out.astype(BF16),
             norm_ffn_g=norm_ffn_g, w_up=ffn_w_up, ffn_conv_w=ffn_conv_w,
             ffn_conv_b=ffn_conv_b, w_down=ffn_w_down)
    dims = (heads_a, dk_a, heads_b, heads_c, dk_c)

    bias_p = _prompt_bias(rel_bias, ATTN_TILE)
    bias_pg, bias_new = _decode_bias(rel_bias, page, DECODE_PAGES_PER_STEP, ss)
    kc = cache_k.reshape(depth, cache_k.shape[1], page * heads_a, LANES)
    vc = cache_v.reshape(depth, cache_v.shape[1], page * heads_a, LANES)

    def prompt_attn(l, qn, knb, vt, lam_init):
        shp = (bp, sp, wa)
        o = _prompt_attention(qn.reshape(shp), knb.reshape(shp), vt, bias_p,
                              diff_lambda[l], diff_subln_g[l], lam_init, dk_a)
        return o.reshape(bp * sp, wa)

    def sample_attn(l, qn, knb, vb, lam_init):
        shp = (bs, ss, wa)
        o = _decode_attention(l, qn.reshape(shp), knb.reshape(shp), vb.reshape(shp), kc, vc, page_table,
                              bias_pg, bias_new, diff_lambda[l], diff_subln_g[l], lam_init, dk_a)
        return o.reshape(bs * ss, wa)

    xp = x_prompt.reshape(bp * sp, d_model)
    xs = x_sample.reshape(bs * ss, d_model)
    zero_m = (jnp.zeros((bp, heads_b, dk_b, dv_b), F32), jnp.zeros((bp, heads_b, dk_b), F32),
              jnp.zeros((bp, heads_b), F32))
    zero_g = jnp.zeros((bp, heads_c, dk_c, dv_c), F32)
    rows_p, rows_s = [], []
    for l in range(depth):
        xp, rp = _layer(xp, l, w, dims, prompt_attn, zero_m, zero_g, None, bp, sp)
        rows_p.append(rp)
        xs, rs = _layer(xs, l, w, dims, sample_attn,
                        (state_mlstm_C[l], state_mlstm_n[l], state_mlstm_m[l]), state_gla_S[l],
                        state_ffn_conv[l], bs, ss)
        rows_s.append(rs)

    def field(rows, i):
        return jnp.stack([r[i] for r in rows], axis=0)

    return (xp.reshape(bp, sp, d_model), xs.reshape(bs, ss, d_model),
            *[field(rows_p, i) for i in range(7)], *[field(rows_s, i) for i in range(7)])
```

```python
import functools
import math

import numpy as np
import jax
import jax.numpy as jnp
from jax import lax
from jax.experimental import pallas as pl
from jax.experimental.pallas import tpu as pltpu

F32 = jnp.float32
BF16 = jnp.bfloat16

LANES = 128
VMEM_LIMIT = 52 * 1024 * 1024
EPS = 1e-6
NEG = -1e30
LOG2E = math.log2(math.e)
GLA_TAU = 16.0
MAX_DISTANCE = 128
ATTN_TILE = 256
SCAN_CHUNK = 256
GLA_CHUNK = 128
SAMPLE_CHUNK = 128
DECODE_PAGES_PER_STEP = 16


def _cparams(*sem):
    return pltpu.CompilerParams(dimension_semantics=sem, vmem_limit_bytes=VMEM_LIMIT)


def _split_bf16(x):
    hi = x.astype(BF16)
    lo = (x - hi.astype(F32)).astype(BF16)
    return hi, lo


def _dot(a, b):
    return jnp.dot(a, b, preferred_element_type=F32)


def _dot_nt(a, b):
    return lax.dot_general(a, b, (((1,), (1,)), ((), ())), preferred_element_type=F32)


def _dot_tn(a, b):
    return lax.dot_general(a, b, (((0,), (0,)), ((), ())), preferred_element_type=F32)


def _log_sigmoid(x):
    return jnp.minimum(x, 0.0) - jnp.log(1.0 + jnp.exp(-jnp.abs(x)))


def _sigmoid(x):
    return 1.0 / (1.0 + jnp.exp(-x))


def _div_pow2(x, n):
    assert n & (n - 1) == 0
    return lax.shift_right_logical(x, n.bit_length() - 1)


def _mod_pow2(x, n):
    assert n & (n - 1) == 0
    return x & (n - 1)


def _mixer_dtype(block_rows):
    return BF16 if block_rows % 16 == 0 else F32


def _pad_rows(x, rows):
    if x.shape[0] == rows:
        return x
    return jnp.concatenate([x, jnp.zeros((rows - x.shape[0], x.shape[1]), x.dtype)], axis=0)


def _proj_down_kernel(a_ref, w_ref, r_ref, o_ref, w_sc):
    @pl.when(pl.program_id(1) == 0)
    def _cast_weights():
        w_sc[...] = w_ref[...].astype(BF16)

    o_ref[...] = r_ref[...] + _dot(a_ref[...], w_sc[...])


def _proj_down(a, w, layer, res):
    m, k = a.shape
    n = w.shape[2]
    tm = min(m, 512)
    tn = 512
    assert m % tm == 0 and n % tn == 0
    return pl.pallas_call(
        _proj_down_kernel,
        out_shape=jax.ShapeDtypeStruct((m, n), F32),
        grid=(n // tn, m // tm),
        in_specs=[pl.BlockSpec((tm, k), lambda j, i: (i, 0)),
                  pl.BlockSpec((None, k, tn), lambda j, i: (layer, 0, j)),
                  pl.BlockSpec((tm, tn), lambda j, i: (i, j))],
        out_specs=pl.BlockSpec((tm, tn), lambda j, i: (i, j)),
        scratch_shapes=[pltpu.VMEM((k, tn), BF16)],
        compiler_params=_cparams("parallel", "arbitrary"),
        name="proj_down",
    )(a, w, res)


def _rms_rows(x, g):
    ms = jnp.mean(x * x, axis=-1, keepdims=True)
    return x * lax.rsqrt(ms + EPS) * g


def _proj_in_kernel(x_ref, g_ref, w_ref, ws_ref, z_ref, zs_ref, h_sc):
    @pl.when(pl.program_id(1) == 0)
    def _norm():
        h_sc[...] = _rms_rows(x_ref[...], g_ref[...]).astype(BF16)
        zs_ref[...] = _dot(h_sc[...], ws_ref[...])

    z_ref[...] = _dot(h_sc[...], w_ref[...])


def _proj_in(x, g, w_main, w_small, layer):
    m, d = x.shape
    n = w_main.shape[2]
    ns = w_small.shape[2]
    tm = min(m, 1024)
    tn = 512
    assert m % tm == 0 and n % tn == 0
    return pl.pallas_call(
        _proj_in_kernel,
        out_shape=(jax.ShapeDtypeStruct((m, n), F32), jax.ShapeDtypeStruct((m, ns), F32)),
        grid=(m // tm, n // tn),
        in_specs=[pl.BlockSpec((tm, d), lambda i, j: (i, 0)),
                  pl.BlockSpec((1, d), lambda i, j: (0, 0)),
                  pl.BlockSpec((None, d, tn), lambda i, j: (layer, 0, j)),
                  pl.BlockSpec((None, d, ns), lambda i, j: (layer, 0, 0))],
        out_specs=(pl.BlockSpec((tm, tn), lambda i, j: (i, j)),
                   pl.BlockSpec((tm, ns), lambda i, j: (i, 0))),
        scratch_shapes=[pltpu.VMEM((tm, d), BF16)],
        compiler_params=_cparams("parallel", "arbitrary"),
        name="proj_in",
    )(x, g.reshape(1, d), w_main, w_small)


def _proj_out_kernel(oa_ref, ob_ref, oc_ref, w_ref, r_ref, g_ref, x_ref, h_ref):
    ka, kb = oa_ref.shape[1], ob_ref.shape[1]
    acc = _dot(oa_ref[...].astype(BF16), w_ref[0:ka, :])
    acc += _dot(ob_ref[...].astype(BF16), w_ref[ka:ka + kb, :])
    acc += _dot(oc_ref[...].astype(BF16), w_ref[ka + kb:, :])
    x = r_ref[...] + acc
    x_ref[...] = x
    h_ref[...] = _rms_rows(x, g_ref[...]).astype(BF16)


def _proj_out(oa, ob, oc, w, layer, res, g):
    m = oa.shape[0]
    k, n = w.shape[1], w.shape[2]
    tm = min(m, 512)
    assert m % tm == 0 and oa.shape[1] + ob.shape[1] + oc.shape[1] == k
    rows = lambda width: pl.BlockSpec((tm, width), lambda i: (i, 0))
    return pl.pallas_call(
        _proj_out_kernel,
        out_shape=(jax.ShapeDtypeStruct((m, n), F32), jax.ShapeDtypeStruct((m, n), BF16)),
        grid=(m // tm,),
        in_specs=[rows(oa.shape[1]), rows(ob.shape[1]), rows(oc.shape[1]),
                  pl.BlockSpec((None, k, n), lambda i: (layer, 0, 0)),
                  rows(n),
                  pl.BlockSpec((1, n), lambda i: (0, 0))],
        out_specs=(rows(n), rows(n)),
        compiler_params=_cparams("parallel"),
        name="proj_out",
    )(oa, ob, oc, w, res, g.reshape(1, n))


def _qknorm_kernel(aq_ref, ak_ref, av_ref, qg_ref, kg_ref, bd_ref,
                   qn_ref, kn_ref, knb_ref, vb_ref, vr_ref, *, heads, inv_dk, scale, v_transposed):
    bd = bd_ref[...]
    qg = qg_ref[...]
    kg = kg_ref[...]

    def norm(x, g):
        hi, lo = _split_bf16(x * x)
        ss = _dot(hi, bd) + _dot(lo, bd)
        return x * lax.rsqrt(ss * inv_dk + EPS) * g

    for h in range(heads):
        sl = slice(LANES * h, LANES * (h + 1))
        qn_ref[:, sl] = (norm(aq_ref[:, sl], qg) * scale).astype(BF16)
        kn = norm(ak_ref[:, sl], kg)
        kn_ref[:, sl] = kn
        knb_ref[:, sl] = kn.astype(BF16)
        if v_transposed:
            vb_ref[0, sl, :] = av_ref[:, sl].T.astype(BF16)
    if not v_transposed:
        vb_ref[...] = av_ref[...].astype(BF16)
    vr_ref[...] = av_ref[...]


def _qknorm(z, qg, kg, heads, dk, seq, v_transposed):
    m = z.shape[0]
    w = heads * LANES
    tm = min(m, 512)
    assert m % tm == 0 and (seq % tm == 0 or not v_transposed)
    lane = np.arange(LANES)
    bd = jnp.asarray((lane[:, None] // dk) == (lane[None, :] // dk), BF16)
    reps = LANES // dk
    kern = functools.partial(_qknorm_kernel, heads=heads, inv_dk=1.0 / dk, scale=dk ** -0.5 * LOG2E,
                             v_transposed=v_transposed)
    rows = pl.BlockSpec((tm, w), lambda i: (i, 0))
    if v_transposed:
        tps = seq // tm
        v_shape = jax.ShapeDtypeStruct((m // seq, w, seq), BF16)
        v_spec = pl.BlockSpec((1, w, tm), lambda i: (i // tps, 0, i % tps))
    else:
        v_shape = jax.ShapeDtypeStruct((m, w), BF16)
        v_spec = rows
    return pl.pallas_call(
        kern,
        out_shape=(jax.ShapeDtypeStruct((m, w), BF16), jax.ShapeDtypeStruct((m, w), F32),
                   jax.ShapeDtypeStruct((m, w), BF16), v_shape, jax.ShapeDtypeStruct((m, w), F32)),
        grid=(m // tm,),
        in_specs=[pl.BlockSpec((tm, w), lambda i: (i, 0)),
                  pl.BlockSpec((tm, w), lambda i: (i, 1)),
                  pl.BlockSpec((tm, w), lambda i: (i, 2)),
                  pl.BlockSpec((1, LANES), lambda i: (0, 0)),
                  pl.BlockSpec((1, LANES), lambda i: (0, 0)),
                  pl.BlockSpec((LANES, LANES), lambda i: (0, 0))],
        out_specs=(rows, rows, rows, v_spec, rows),
        compiler_params=_cparams("parallel"),
        name="qknorm",
    )(z, z, z, jnp.tile(qg, reps).reshape(1, LANES), jnp.tile(kg, reps).reshape(1, LANES), bd)


def _bucket(n, n_buckets):
    max_exact = n_buckets // 2
    nf = jnp.maximum(n, max_exact).astype(F32)
    large = max_exact + (jnp.log(nf / max_exact) / math.log(MAX_DISTANCE / max_exact)
                         * (n_buckets - max_exact)).astype(jnp.int32)
    large = jnp.minimum(large, n_buckets - 1)
    return jnp.where(n < max_exact, jnp.maximum(n, 0), large)


def _prompt_bias_kernel(rb_ref, o_ref, *, tile, n_buckets):
    h = pl.program_id(0)
    j = lax.broadcasted_iota(jnp.int32, (tile, tile), 0)
    i = lax.broadcasted_iota(jnp.int32, (tile, tile), 1)
    far = jnp.full((tile, tile), MAX_DISTANCE, jnp.int32)
    for t, n in enumerate((i - j, tile + i - j, far)):
        b = _bucket(n, n_buckets)
        val = jnp.full((tile, tile), rb_ref[0, h], F32)
        for k in range(1, n_buckets):
            val = jnp.where(b == k, rb_ref[k, h], val)
        o_ref[0, t] = jnp.where(n >= 0, val * LOG2E, NEG)


def _prompt_bias(rel_bias, tile):
    n_buckets, heads = rel_bias.shape
    kern = functools.partial(_prompt_bias_kernel, tile=tile, n_buckets=n_buckets)
    return pl.pallas_call(
        kern,
        out_shape=jax.ShapeDtypeStruct((heads, 3, tile, tile), F32),
        grid=(heads,),
        in_specs=[pl.BlockSpec(memory_space=pltpu.SMEM)],
        out_specs=pl.BlockSpec((1, 3, tile, tile), lambda h: (h, 0, 0, 0)),
        compiler_params=_cparams("parallel"),
        name="prompt_bias",
    )(rel_bias)


def _decode_bias_kernel(rb_ref, pg_ref, new_ref, *, page, ppb, dec_seq, heads, n_buckets):
    grp = 2 * dec_seq

    def lookup(n, h):
        b = _bucket(n, n_buckets)
        val = jnp.full(n.shape, rb_ref[0, h], F32)
        for k in range(1, n_buckets):
            val = jnp.where(b == k, rb_ref[k, h], val)
        return jnp.where(n >= 0, val * LOG2E, NEG)

    qi = _mod_pow2(lax.broadcasted_iota(jnp.int32, (grp, page), 0), dec_seq)
    j = lax.broadcasted_iota(jnp.int32, (grp, page), 1)
    far = jnp.full((grp, page), MAX_DISTANCE, jnp.int32)
    for h in range(heads):
        rs = slice(grp * h, grp * (h + 1))
        far_b = lookup(far, h)
        for c in range(ppb):
            pg_ref[0, rs, page * c:page * (c + 1)] = far_b
            pg_ref[1, rs, page * c:page * (c + 1)] = far_b if c < ppb - 1 else lookup(page + qi - j, h)
        new_ref[rs, :] = lookup(jnp.where(j < dec_seq, qi - j, -1), h)


def _decode_bias(rel_bias, page, ppb, dec_seq):
    n_buckets, heads = rel_bias.shape
    rows = 2 * dec_seq * heads
    kern = functools.partial(_decode_bias_kernel, page=page, ppb=ppb, dec_seq=dec_seq, heads=heads,
                             n_buckets=n_buckets)
    return pl.pallas_call(
        kern,
        out_shape=(jax.ShapeDtypeStruct((2, rows, ppb * page), F32),
                   jax.ShapeDtypeStruct((rows, page), F32)),
        in_specs=[pl.BlockSpec(memory_space=pltpu.SMEM)],
        name="decode_bias",
    )(rel_bias)


def _diff_lambda(dl, lam_init):
    s1 = jnp.sum(dl[0:1] * dl[1:2], axis=-1, keepdims=True)
    s2 = jnp.sum(dl[2:3] * dl[3:4], axis=-1, keepdims=True)
    return jnp.exp(s1) - jnp.exp(s2) + lam_init


def _subln(o, g, lam_init):
    ms = jnp.mean(o * o, axis=-1, keepdims=True)
    return o * lax.rsqrt(ms + EPS) * g * (1.0 - lam_init)


def _flash_update(s, m_prev, l_prev):
    m_new = jnp.maximum(m_prev, jnp.max(s, axis=-1, keepdims=True))
    alpha = jnp.exp2(m_prev - m_new)
    p = jnp.exp2(s - m_new)
    l_new = alpha * l_prev + jnp.sum(p, axis=-1, keepdims=True)
    return p, alpha, m_new, l_new


def _attn_kernel(qt_ref, kt_ref, q_ref, k_ref, vt_ref, bias_ref, dl_ref, g_ref, o_ref, m_sc, l_sc, acc_sc, qm_sc,
                 *, heads, tile, dk, lam_init):
    qi = qt_ref[pl.program_id(1)]
    ki = kt_ref[pl.program_id(1)]
    n_sub = tile // LANES

    @pl.when(ki == 0)
    def _init():
        m_sc[...] = jnp.full(m_sc.shape, -jnp.inf, F32)
        l_sc[...] = jnp.zeros(l_sc.shape, F32)
        acc_sc[...] = jnp.zeros(acc_sc.shape, F32)
        first_map = _mod_pow2(lax.broadcasted_iota(jnp.int32, (LANES, heads * LANES), 1), LANES) < dk
        for c in range(2 * n_sub):
            qc = q_ref[0, LANES * (c % n_sub):LANES * (c % n_sub + 1), :]
            qm_sc[c] = jnp.where(first_map == (c < n_sub), qc, jnp.zeros_like(qc))

    for h in range(heads):
        sl = slice(LANES * h, LANES * (h + 1))
        kh = k_ref[0, :, sl]
        vht = vt_ref[0, sl, :]
        for c in range(2 * n_sub):
            qrows = slice(LANES * (c % n_sub), LANES * (c % n_sub + 1))
            cs = slice(LANES * c, LANES * (c + 1))
            s = _dot_nt(kh, qm_sc[c, :, sl]) + bias_ref[h, 0, :, qrows]
            m_prev = m_sc[h, :, cs]
            m_new = jnp.maximum(m_prev, jnp.max(s, axis=0, keepdims=True))
            alpha = jnp.exp2(m_prev - m_new)
            p = jnp.exp2(s - m_new)
            l_sc[h, :, cs] = alpha * l_sc[h, :, cs] + jnp.sum(p, axis=0, keepdims=True)
            acc_sc[h, :, cs] = alpha * acc_sc[h, :, cs] + _dot(vht, p.astype(BF16))
            m_sc[h, :, cs] = m_new

    @pl.when(ki == qi)
    def _finish():
        lam = _diff_lambda(dl_ref[...], lam_init)
        g_col = g_ref[...]
        for h in range(heads):
            acc = acc_sc[h]
            l = l_sc[h]
            ot = acc[:, :tile] / l[:, :tile] - lam * (acc[:, tile:] / l[:, tile:])
            ms = jnp.mean(ot * ot, axis=0, keepdims=True)
            ot = ot * lax.rsqrt(ms + EPS) * g_col * (1.0 - lam_init)
            o_ref[0, :, LANES * h:LANES * (h + 1)] = ot.T.astype(o_ref.dtype)


def _prompt_attention(qn, knb, vt, bias, dl, g, lam_init, dk):
    bsz, seq, w = qn.shape
    heads = w // LANES
    tile = bias.shape[-1]
    nq = seq // tile
    assert seq % tile == 0 and tile % LANES == 0
    kern = functools.partial(_attn_kernel, heads=heads, tile=tile, dk=dk, lam_init=lam_init)
    pairs = [(qi, ki) for qi in range(nq) for ki in range(qi + 1)]
    q_tab = jnp.asarray([p[0] for p in pairs], jnp.int32)
    k_tab = jnp.asarray([p[1] for p in pairs], jnp.int32)

    def bias_idx(b, t, qt, kt):
        return (0, jnp.where(kt[t] == qt[t], 0, jnp.where(kt[t] == qt[t] - 1, 1, 2)), 0, 0)

    grid_spec = pltpu.PrefetchScalarGridSpec(
        num_scalar_prefetch=2,
        grid=(bsz, len(pairs)),
        in_specs=[pl.BlockSpec((1, tile, w), lambda b, t, qt, kt: (b, qt[t], 0)),
                  pl.BlockSpec((1, tile, w), lambda b, t, qt, kt: (b, kt[t], 0)),
                  pl.BlockSpec((1, w, tile), lambda b, t, qt, kt: (b, 0, kt[t])),
                  pl.BlockSpec((heads, 1, tile, tile), bias_idx),
                  pl.BlockSpec(dl.shape, lambda b, t, qt, kt: (0, 0)),
                  pl.BlockSpec((LANES, 1), lambda b, t, qt, kt: (0, 0))],
        out_specs=pl.BlockSpec((1, tile, w), lambda b, t, qt, kt: (b, qt[t], 0)),
        scratch_shapes=[pltpu.VMEM((heads, 1, 2 * tile), F32),
                        pltpu.VMEM((heads, 1, 2 * tile), F32),
                        pltpu.VMEM((heads, LANES, 2 * tile), F32),
                        pltpu.VMEM((2 * tile // LANES, LANES, w), BF16)],
    )
    return pl.pallas_call(
        kern,
        out_shape=jax.ShapeDtypeStruct((bsz, seq, w), BF16),
        grid_spec=grid_spec,
        compiler_params=_cparams("parallel", "arbitrary"),
        name="prompt_attention",
    )(q_tab, k_tab, qn, knb, vt, bias, dl, g.reshape(LANES, 1))


def _decode_attn_kernel(pt_ref, q_ref, kn_ref, vn_ref, *refs, heads, dec_seq, page, ppb, dk, lam_init):
    del pt_ref
    kc_refs, vc_refs = refs[:ppb], refs[ppb:2 * ppb]
    bias_ref, bnew_ref, dl_ref, g_ref, o_ref, wq_sc, wqb_sc, m_sc, l_sc, acc_sc = refs[2 * ppb:]
    p_idx = pl.program_id(1)
    rows = 2 * dec_seq * heads
    grp = 2 * dec_seq
    width = heads * LANES

    @pl.when(p_idx == 0)
    def _init():
        q = q_ref[0].astype(F32)
        lane = lax.broadcasted_iota(jnp.int32, (dec_seq, LANES), 1)
        pieces = []
        for h in range(heads):
            qh = q[:, LANES * h:LANES * (h + 1)]
            pieces += [jnp.where(lane < dk, qh, 0.0), jnp.where(lane >= dk, qh, 0.0)]
        wq = jnp.concatenate(pieces, axis=0)
        wq_sc[...] = wq.astype(BF16)
        c = lax.broadcasted_iota(jnp.int32, (rows, width), 1)
        r = lax.broadcasted_iota(jnp.int32, (rows, width), 0)
        own = _div_pow2(r, grp) == _div_pow2(c, LANES)
        wqb_sc[...] = jnp.where(own, jnp.concatenate([wq] * heads, axis=1), 0.0).astype(BF16)
        m_sc[...] = jnp.full(m_sc.shape, -jnp.inf, F32)
        l_sc[...] = jnp.zeros(l_sc.shape, F32)
        acc_sc[...] = jnp.zeros(acc_sc.shape, F32)

    def head_rows(ref, h):
        return ref[pl.ds(h, page, stride=heads), :].astype(BF16)

    s = jnp.concatenate(
        [jnp.concatenate([_dot_nt(wq_sc[grp * h:grp * (h + 1), :], head_rows(kc, h)) for h in range(heads)],
                         axis=0) for kc in kc_refs], axis=1) + bias_ref[0]
    p, alpha, m_new, l_new = _flash_update(s, m_sc[...], l_sc[...])
    m_sc[...] = m_new
    l_sc[...] = l_new
    pb = p.astype(BF16)
    for h in range(heads):
        rs = slice(grp * h, grp * (h + 1))
        acc = alpha[rs] * acc_sc[rs, :]
        for j, vc in enumerate(vc_refs):
            acc += _dot(pb[rs, page * j:page * (j + 1)], head_rows(vc, h))
        acc_sc[rs, :] = acc

    @pl.when(p_idx == pl.num_programs(1) - 1)
    def _finish():
        kn = _pad_rows(kn_ref[0].astype(F32), page).astype(BF16)
        vn = _pad_rows(vn_ref[0].astype(F32), page)
        s = _dot_nt(wqb_sc[...], kn) + bnew_ref[...]
        p, alpha, _, l_fin = _flash_update(s, m_sc[...], l_sc[...])
        pb = p.astype(BF16)
        lam = _diff_lambda(dl_ref[...], lam_init)
        g = g_ref[...]
        for h in range(heads):
            rs = slice(grp * h, grp * (h + 1))
            vh = vn[:, LANES * h:LANES * (h + 1)].astype(BF16)
            acc = alpha[rs] * acc_sc[rs, :] + _dot(pb[rs], vh)
            l = l_fin[rs]
            o = acc[:dec_seq] / l[:dec_seq] - lam * (acc[dec_seq:] / l[dec_seq:])
            o_ref[0, :, LANES * h:LANES * (h + 1)] = _subln(o, g, lam_init)


def _decode_attention(layer, qn, knb, vb, cache_k, cache_v, page_table, bias_pg, bias_new, dl, g, lam_init, dk):
    bsz, dec_seq, w = qn.shape
    heads = w // LANES
    page = cache_k.shape[2] // heads
    ppb = bias_pg.shape[-1] // page
    n_steps = page_table.shape[1] // ppb
    assert page_table.shape[1] % ppb == 0
    rows = 2 * dec_seq * heads
    kern = functools.partial(_decode_attn_kernel, heads=heads, dec_seq=dec_seq, page=page, ppb=ppb, dk=dk,
                             lam_init=lam_init)
    new_spec = pl.BlockSpec((1, dec_seq, w), lambda b, p, pt: (b, 0, 0))
    cache_specs = [pl.BlockSpec((None, None, page * heads, LANES),
                                lambda b, p, pt, j=j: (layer, pt[b, p * ppb + j], 0, 0)) for j in range(ppb)]
    grid_spec = pltpu.PrefetchScalarGridSpec(
        num_scalar_prefetch=1,
        grid=(bsz, n_steps),
        in_specs=[new_spec, new_spec, new_spec, *cache_specs, *cache_specs,
                  pl.BlockSpec((1, rows, ppb * page),
                               lambda b, p, pt: (jnp.where(p == n_steps - 1, 1, 0), 0, 0)),
                  pl.BlockSpec((rows, page), lambda b, p, pt: (0, 0)),
                  pl.BlockSpec(dl.shape, lambda b, p, pt: (0, 0)),
                  pl.BlockSpec((1, LANES), lambda b, p, pt: (0, 0))],
        out_specs=pl.BlockSpec((1, dec_seq, w), lambda b, p, pt: (b, 0, 0)),
        scratch_shapes=[pltpu.VMEM((rows, LANES), BF16),
                        pltpu.VMEM((rows, w), BF16),
                        pltpu.VMEM((rows, 1), F32),
                        pltpu.VMEM((rows, 1), F32),
                        pltpu.VMEM((rows, LANES), F32)],
    )
    return pl.pallas_call(
        kern,
        out_shape=jax.ShapeDtypeStruct((bsz, dec_seq, w), F32),
        grid_spec=grid_spec,
        compiler_params=_cparams("parallel", "arbitrary"),
        name="decode_attention",
    )(page_table, qn, knb, vb, *([cache_k] * ppb), *([cache_v] * ppb), bias_pg, bias_new, dl,
      g.reshape(1, LANES))


GATE_I = 16
GATE_F = 20


def _mlstm_kernel(q_ref, k_ref, v_ref, og_ref, zs_ref, gt_ref, gbl_ref, gbc_ref, g_ref,
                  c0_ref, n0_ref, m0_ref,
                  o_ref, cf_ref, nf_ref, mf_ref, c_sc, n_sc, m_sc,
                  *, heads, chunk, valid, scale):
    c_idx = pl.program_id(1)

    @pl.when(c_idx == 0)
    def _init():
        for h in range(heads):
            c_sc[h] = c0_ref[0, h].T
        n_sc[...] = n0_ref[0]
        m_sc[...] = m0_ref[0]

    rows_in = q_ref.shape[0]
    row = lax.broadcasted_iota(jnp.int32, (chunk, 1), 0)
    s_i = lax.broadcasted_iota(jnp.int32, (chunk, chunk), 0)
    t_i = lax.broadcasted_iota(jnp.int32, (chunk, chunk), 1)
    causal = s_i <= t_i
    incl = causal.astype(BF16)
    incl_t = (s_i >= t_i).astype(BF16)

    gcol = _pad_rows(zs_ref[...], chunk) + gbl_ref[...]
    lf_mat = _log_sigmoid(gcol)
    if valid < chunk:
        lf_mat = jnp.where(row < valid, lf_mat, 0.0)
    hi, lo = _split_bf16(lf_mat)
    b_mat = _dot(incl_t, hi) + _dot(incl_t, lo)

    grow = gt_ref[0] + gbc_ref[...]
    col = lax.broadcasted_iota(jnp.int32, grow.shape, 1)
    grow_id = lax.broadcasted_iota(jnp.int32, grow.shape, 0)
    lf_rows = jnp.where(grow_id >= heads, _log_sigmoid(grow), 0.0)
    if valid < chunk:
        lf_rows = jnp.where(col < valid, lf_rows, 0.0)
    hi, lo = _split_bf16(lf_rows)
    b_rows = _dot(hi, incl) + _dot(lo, incl)

    g = g_ref[...]
    first_row = lax.broadcasted_iota(jnp.int32, (16, 1), 0) == 0
    for h in range(heads):
        sl = slice(LANES * h, LANES * (h + 1))
        li_col = gcol[:, GATE_I + h:GATE_I + h + 1]
        if valid < chunk:
            li_col = jnp.where(row < valid, li_col, NEG)
        u_col = li_col - b_mat[:, GATE_F + h:GATE_F + h + 1]
        b_row = b_rows[heads + h:heads + h + 1, :]
        dt = jnp.where(causal, b_row + u_col, NEG)
        m_prev = m_sc[h:h + 1, 0:1]
        inter = b_row + m_prev
        mt = jnp.maximum(inter, jnp.max(dt, axis=0, keepdims=True))
        wi = jnp.exp(inter - mt)
        q = _pad_rows(q_ref[:, sl], chunk)
        ks = _pad_rows(k_ref[:, sl], chunk) * scale
        vtb = _pad_rows(v_ref[:, sl], chunk).T.astype(BF16)
        qb = q.astype(BF16)
        pt = _dot_nt(ks.astype(BF16), qb) * jnp.exp(dt - mt)
        ct_prev = c_sc[h]
        n_prev = n_sc[h:h + 1, :]
        num = wi * _dot_nt(ct_prev.astype(BF16), qb) + _dot(vtb, pt.astype(BF16))
        n_rows = jnp.where(first_row, jnp.broadcast_to(n_prev, (16, LANES)), 0.0)
        qn = _dot_nt(n_rows.astype(BF16), qb)[0:1]
        den = wi * qn + jnp.sum(pt, axis=0, keepdims=True)
        ht = num / jnp.maximum(jnp.abs(den), jnp.exp(-mt))
        og = _sigmoid(og_ref[:, sl])
        y = og * ht.T[:rows_in]
        ms = jnp.mean(y * y, axis=-1, keepdims=True)
        o_ref[:, sl] = (y * lax.rsqrt(ms + EPS) * g).astype(o_ref.dtype)

        m_new = mt[:, chunk - 1:chunk]
        b_last = b_row[:, chunk - 1:chunk]
        a = jnp.exp(b_last + m_prev - m_new)
        kw = ks * jnp.exp(u_col + (b_last - m_new))
        c_sc[h] = a * ct_prev + _dot(vtb, kw.astype(BF16))
        n_sc[h:h + 1, :] = a * n_prev + jnp.sum(kw, axis=0, keepdims=True)
        m_sc[h:h + 1, :] = jnp.broadcast_to(m_new, (1, LANES))

    @pl.when(c_idx == pl.num_programs(1) - 1)
    def _finish():
        for h in range(heads):
            cf_ref[0, h] = c_sc[h].T
        nf_ref[0] = n_sc[...]
        mf_ref[0] = m_sc[...]


def _mlstm(z, zs, gt, gate_b, g, c0, n0, m0, seq, chunk, col0):
    bsz, heads = c0.shape[0], c0.shape[1]
    w = heads * LANES
    rows_in = min(seq, chunk)
    nc = seq // rows_in
    valid = rows_in
    gt_w = gt.shape[-1] // nc
    kern = functools.partial(_mlstm_kernel, heads=heads, chunk=chunk, valid=valid, scale=LANES ** -0.5)
    gbl = jnp.zeros((1, LANES), F32)
    gbl = gbl.at[0, GATE_I:GATE_I + heads].set(gate_b[0]).at[0, GATE_F:GATE_F + heads].set(gate_b[1])
    gbc = gate_b.reshape(2 * heads, 1)
    m0b = jnp.broadcast_to(m0[:, :, None], (bsz, heads, LANES))

    def zspec(blk):
        return pl.BlockSpec((rows_in, w), lambda b, c: (b * nc + c, col0 + blk))

    state = lambda shape: pl.BlockSpec((1,) + shape, lambda b, c: (b,) + (0,) * len(shape))
    return pl.pallas_call(
        kern,
        out_shape=(jax.ShapeDtypeStruct((bsz * seq, w), _mixer_dtype(rows_in)),
                   jax.ShapeDtypeStruct((bsz, heads, LANES, LANES), F32),
                   jax.ShapeDtypeStruct((bsz, heads, LANES), F32),
                   jax.ShapeDtypeStruct((bsz, heads, LANES), F32)),
        grid=(bsz, nc),
        in_specs=[zspec(0), zspec(1), zspec(2), zspec(3),
                  pl.BlockSpec((rows_in, LANES), lambda b, c: (b * nc + c, 0)),
                  pl.BlockSpec((1, 2 * heads, gt_w), lambda b, c: (b, 0, c)),
                  pl.BlockSpec((1, LANES), lambda b, c: (0, 0)),
                  pl.BlockSpec((2 * heads, 1), lambda b, c: (0, 0)),
                  pl.BlockSpec((1, LANES), lambda b, c: (0, 0)),
                  state((heads, LANES, LANES)), state((heads, LANES)), state((heads, LANES))],
        out_specs=(pl.BlockSpec((rows_in, w), lambda b, c: (b * nc + c, 0)),
                   state((heads, LANES, LANES)), state((heads, LANES)), state((heads, LANES))),
        scratch_shapes=[pltpu.VMEM((heads, LANES, LANES), F32),
                        pltpu.VMEM((heads, LANES), F32),
                        pltpu.VMEM((heads, LANES), F32)],
        compiler_params=_cparams("parallel", "arbitrary"),
        name="mlstm",
    )(z, z, z, z, zs, gt, gbl, gbc, g.reshape(1, LANES), c0, n0, m0b)


def _gla_levels(chunk):
    n, out = chunk, []
    while n >= 2:
        out.append(n)
        n //= 2
    return out


def _gla_weights(chunk):
    t = np.arange(chunk)[:, None]
    s = np.arange(chunk)[None, :]
    blocks = [(s <= t).astype(np.float32), (s > t).astype(np.float32)]
    for n in _gla_levels(chunk):
        mid = (t // n) * n + n // 2 - 1
        blocks.append(((s > mid) & (s <= t)).astype(np.float32) - ((s > t) & (s <= mid)).astype(np.float32))
    return jnp.asarray(np.concatenate(blocks, axis=0), BF16)


def _gla_kernel(q_ref, k_ref, v_ref, gg_ref, zs_ref, wa_ref, ba_ref, ws_ref, g_ref, s0_ref,
                o_ref, sf_ref, s_sc, *, heads, chunk, valid, dk, scale):
    c_idx = pl.program_id(1)
    per = LANES // dk
    assert per == 2 and heads % per == 0

    @pl.when(c_idx == 0)
    def _init():
        for h in range(heads):
            s_sc[h // per, dk * (h % per):dk * (h % per + 1), :] = s0_ref[0, h]

    rows_in = q_ref.shape[0]
    row = lax.broadcasted_iota(jnp.int32, (chunk, 1), 0)
    t_i = lax.broadcasted_iota(jnp.int32, (per * chunk, chunk), 0) & (chunk - 1)
    s_i = lax.broadcasted_iota(jnp.int32, (per * chunk, chunk), 1)
    first_head = lax.broadcasted_iota(jnp.int32, (chunk, LANES), 1) < dk

    zs = _pad_rows(zs_ref[...], chunk)
    la = _log_sigmoid(_dot(zs.astype(BF16), wa_ref[...]) + ba_ref[...]) * (1.0 / GLA_TAU)
    if valid < chunk:
        la = jnp.where(row < valid, la, 0.0)
    hi, lo = _split_bf16(la)
    wst = ws_ref[...]
    e_all = _dot(wst, hi) + _dot(wst, lo)
    ones = jnp.ones((chunk, LANES), BF16)
    levels = _gla_levels(chunk)
    g = g_ref[...]
    second_half = [(row & (n - 1)) >= n // 2 for n in levels]
    same_node = [_div_pow2(t_i, n) == _div_pow2(s_i, n) for n in levels]

    def split_heads(x):
        return [jnp.where(first_head, x, 0.0), jnp.where(first_head, 0.0, x)]

    for j in range(heads // per):
        psl = slice(LANES * j, LANES * (j + 1))
        q = _pad_rows(q_ref[:, psl], chunk) * scale
        k = _pad_rows(k_ref[:, psl], chunk)
        bc = e_all[0:chunk, psl]
        rem = e_all[chunk:2 * chunk, psl]
        a_pair = jnp.zeros((per * chunk, chunk), F32)
        for li in range(len(levels)):
            e = e_all[(2 + li) * chunk:(3 + li) * chunk, psl]
            qt = jnp.where(second_half[li], q * jnp.exp(jnp.minimum(e, 0.0)), 0.0)
            kt = jnp.where(second_half[li], 0.0, k * jnp.exp(jnp.minimum(-e, 0.0)))
            qs = jnp.concatenate(split_heads(qt), axis=0).astype(BF16)
            a_pair = a_pair + jnp.where(same_node[li], _dot_nt(qs, kt.astype(BF16)), 0.0)
        s_prev = s_sc[j]
        s_prev_b = s_prev.astype(BF16)
        qd = split_heads(q * jnp.exp(bc))
        qk = split_heads(q * k)
        kd = split_heads(k * jnp.exp(rem))
        hi_p, lo_p = _split_bf16(la[:, psl])
        tot = _dot_tn(hi_p, ones) + _dot_tn(lo_p, ones)
        s_new = jnp.exp(tot) * s_prev
        for hh in range(per):
            h = per * j + hh
            vsl = slice(LANES * h, LANES * (h + 1))
            vb = _pad_rows(v_ref[:, vsl], chunk).astype(BF16)
            a_h = a_pair[chunk * hh:chunk * (hh + 1)]
            o = _dot(qd[hh].astype(BF16), s_prev_b) + _dot(a_h.astype(BF16), vb)
            o = o + jnp.sum(qk[hh], axis=-1, keepdims=True) * vb.astype(F32)
            o = o[:rows_in]
            ms = jnp.mean(o * o, axis=-1, keepdims=True)
            gate = gg_ref[:, vsl]
            o_ref[:, vsl] = (o * lax.rsqrt(ms + EPS) * g * (gate * _sigmoid(gate))).astype(o_ref.dtype)
            s_new = s_new + _dot_tn(kd[hh].astype(BF16), vb)
        s_sc[j] = s_new

    @pl.when(c_idx == pl.num_programs(1) - 1)
    def _finish():
        for h in range(heads):
            sf_ref[0, h] = s_sc[h // per, dk * (h % per):dk * (h % per + 1), :]


def _gla(z, zs, w_alpha, b_alpha, g, s0, seq, chunk, qcol, kcol, vcol, gcol):
    bsz, heads, dk, dv = s0.shape
    rows_in = min(seq, chunk)
    nc = seq // rows_in
    kw = heads * dk
    vw = heads * dv
    wa = jnp.zeros((LANES, kw), F32).at[:w_alpha.shape[0]].set(w_alpha).astype(BF16)
    wst = _gla_weights(chunk)
    kern = functools.partial(_gla_kernel, heads=heads, chunk=chunk, valid=rows_in, dk=dk, scale=dk ** -0.5)
    const = lambda shape: pl.BlockSpec(shape, lambda b, c: (0,) * len(shape))
    return pl.pallas_call(
        kern,
        out_shape=(jax.ShapeDtypeStruct((bsz * seq, vw), _mixer_dtype(rows_in)),
                   jax.ShapeDtypeStruct((bsz, heads, dk, dv), F32)),
        grid=(bsz, nc),
        in_specs=[pl.BlockSpec((rows_in, kw), lambda b, c: (b * nc + c, qcol)),
                  pl.BlockSpec((rows_in, kw), lambda b, c: (b * nc + c, kcol)),
                  pl.BlockSpec((rows_in, vw), lambda b, c: (b * nc + c, vcol)),
                  pl.BlockSpec((rows_in, vw), lambda b, c: (b * nc + c, gcol)),
                  pl.BlockSpec((rows_in, LANES), lambda b, c: (b * nc + c, 0)),
                  const((LANES, kw)), const((1, kw)), const(wst.shape), const((1, LANES)),
                  pl.BlockSpec((1, heads, dk, dv), lambda b, c: (b, 0, 0, 0))],
        out_specs=(pl.BlockSpec((rows_in, vw), lambda b, c: (b * nc + c, 0)),
                   pl.BlockSpec((1, heads, dk, dv), lambda b, c: (b, 0, 0, 0))),
        scratch_shapes=[pltpu.VMEM((kw // LANES, LANES, dv), F32)],
        compiler_params=_cparams("parallel", "arbitrary"),
        name="gla",
    )(z, z, z, z, zs, wa, b_alpha.reshape(1, kw), wst, g.reshape(1, LANES), s0)


def _conv_gate(ug, uv, cwg, cwv, cbg, cbv, prev):
    def conv(u, cw, cb, which):
        u1, u2 = prev(u, which)
        return cb + cw[0:1] * u2 + cw[1:2] * u1 + cw[2:3] * u
    cg = conv(ug, cwg, cbg, 0)
    cv = conv(uv, cwv, cbv, 1)
    return cg * _sigmoid(cg) * cv


FFN_SUB = 256


def _ffn_up_prompt_kernel(h_ref, wg_ref, wv_ref, cwg_ref, cwv_ref, cbg_ref, cbv_ref,
                          act_ref, tg_ref, tv_ref, u_sc, w_sc, *, tiles_per_seq):
    i = pl.program_id(1)
    first = (i % tiles_per_seq) == 0
    hb = h_ref[...]
    tm = hb.shape[0]

    @pl.when(i == 0)
    def _cast_weights():
        w_sc[0] = wg_ref[...].astype(BF16)
        w_sc[1] = wv_ref[...].astype(BF16)

    @pl.when(first)
    def _reset():
        u_sc[:, 0:8, :] = jnp.zeros((2, 8, u_sc.shape[2]), F32)

    def conv_half(which, u, cw_ref, cb_ref, t_ref, cs):
        cw = cw_ref[:, cs]
        u_sc[which, 8:8 + tm, cs] = u
        u1 = u_sc[which, 7:7 + tm, cs]
        u2 = u_sc[which, 6:6 + tm, cs]
        conv = cb_ref[:, cs] + cw[0:1] * u2 + cw[1:2] * u1 + cw[2:3] * u
        u_sc[which, 0:8, cs] = u[tm - 8:tm]
        t_ref[0, :, cs] = u[tm - 8:tm]
        return conv

    subs = [slice(c0, c0 + FFN_SUB) for c0 in range(0, wg_ref.shape[1], FFN_SUB)]
    for cs in subs:
        cg = conv_half(0, _dot(hb, w_sc[0, :, cs]), cwg_ref, cbg_ref, tg_ref, cs)
        cv = conv_half(1, _dot(hb, w_sc[1, :, cs]), cwv_ref, cbv_ref, tv_ref, cs)
        act_ref[:, cs] = (cg * _sigmoid(cg) * cv).astype(BF16)


def _ffn_up_prompt(h2, w_up, layer, conv_w, conv_b, bsz, seq):
    m, d = h2.shape
    dff = w_up.shape[2] // 2
    tm, tn = 1024, 512
    assert seq % tm == 0 and dff % tn == 0 and tn % FFN_SUB == 0
    nj = dff // tn
    tps = seq // tm
    kern = functools.partial(_ffn_up_prompt_kernel, tiles_per_seq=tps)
    wspec = lambda off: pl.BlockSpec((None, d, tn), lambda j, i: (layer, 0, j + off))
    cspec = lambda r, off: pl.BlockSpec((r, tn), lambda j, i: (0, j + off))
    tail = pl.BlockSpec((1, 8, tn), lambda j, i: (i // tps, 0, j))
    cb = conv_b.reshape(1, -1)
    return pl.pallas_call(
        kern,
        out_shape=(jax.ShapeDtypeStruct((m, dff), BF16),
                   jax.ShapeDtypeStruct((bsz, 8, dff), F32),
                   jax.ShapeDtypeStruct((bsz, 8, dff), F32)),
        grid=(nj, m // tm),
        in_specs=[pl.BlockSpec((tm, d), lambda j, i: (i, 0)),
                  wspec(0), wspec(nj), cspec(3, 0), cspec(3, nj), cspec(1, 0), cspec(1, nj)],
        out_specs=(pl.BlockSpec((tm, tn), lambda j, i: (i, j)), tail, tail),
        scratch_shapes=[pltpu.VMEM((2, tm + 8, tn), F32), pltpu.VMEM((2, d, tn), BF16)],
        compiler_params=_cparams("parallel", "arbitrary"),
        name="ffn_up_prompt",
    )(h2, w_up, w_up, conv_w, conv_w, cb, cb)


def _ffn_up_sample_kernel(h_ref, wg_ref, wv_ref, cwg_ref, cwv_ref, cbg_ref, cbv_ref,
                          stg_ref, stv_ref, act_ref, ug_ref, uv_ref, *, seq):
    hb = h_ref[...]
    ug = _dot(hb, wg_ref[...].astype(BF16))
    uv = _dot(hb, wv_ref[...].astype(BF16))
    tm = ug.shape[0]
    row = lax.broadcasted_iota(jnp.int32, (tm, 1), 0)
    states = (stg_ref[...], stv_ref[...])

    def prev(u, which):
        st = states[which]
        u1 = pltpu.roll(u, 1, 0)
        u2 = pltpu.roll(u, 2, 0)
        for b in range(tm // seq):
            older, newer = st[2 * b:2 * b + 1], st[2 * b + 1:2 * b + 2]
            u1 = jnp.where(row == seq * b, newer, u1)
            u2 = jnp.where(row == seq * b, older, jnp.where(row == seq * b + 1, newer, u2))
        return u1, u2

    act_ref[...] = _conv_gate(ug, uv, cwg_ref[...], cwv_ref[...], cbg_ref[...], cbv_ref[...], prev).astype(BF16)
    ug_ref[...] = ug
    uv_ref[...] = uv


def _ffn_up_sample(h2, w_up, layer, conv_w, conv_b, conv_state, seq):
    m, d = h2.shape
    dff = w_up.shape[2] // 2
    tn = 512
    assert dff % tn == 0
    nj = dff // tn
    bsz = m // seq
    assert seq >= 2
    st = conv_state.reshape(2 * bsz, 2 * dff)
    kern = functools.partial(_ffn_up_sample_kernel, seq=seq)
    wspec = lambda off: pl.BlockSpec((None, d, tn), lambda j: (layer, 0, j + off))
    cspec = lambda r, off: pl.BlockSpec((r, tn), lambda j: (0, j + off))
    cb = conv_b.reshape(1, -1)
    ospec = pl.BlockSpec((m, tn), lambda j: (0, j))
    return pl.pallas_call(
        kern,
        out_shape=(jax.ShapeDtypeStruct((m, dff), BF16),
                   jax.ShapeDtypeStruct((m, dff), F32),
                   jax.ShapeDtypeStruct((m, dff), F32)),
        grid=(nj,),
        in_specs=[pl.BlockSpec((m, d), lambda j: (0, 0)),
                  wspec(0), wspec(nj), cspec(3, 0), cspec(3, nj), cspec(1, 0), cspec(1, nj),
                  cspec(2 * bsz, 0), cspec(2 * bsz, nj)],
        out_specs=(ospec, ospec, ospec),
        compiler_params=_cparams("parallel"),
        name="ffn_up_sample",
    )(h2, w_up, w_up, conv_w, conv_w, cb, cb, st, st)


def _layer(x, l, w, dims, attn_fn, mstate, gstate, conv_state, bsz, seq):
    (heads_a, dk_a, heads_b, heads_c, dk_c) = dims
    wa = heads_a * LANES
    wb = heads_b * LANES
    prompt = conv_state is None
    z, zs = _proj_in(x, w["norm_mix_g"][l], w["w_main"], w["w_small"], l)
    qn, kn, knb, vb, vr = _qknorm(z, w["q_norm_g"][l], w["k_norm_g"][l], heads_a, dk_a, seq,
                                  v_transposed=prompt)
    lam_init = 0.8 - 0.6 * math.exp(-0.3 * l)
    oa = attn_fn(l, qn, knb, vb, lam_init)

    chunk_b = SCAN_CHUNK if seq % SCAN_CHUNK == 0 else SAMPLE_CHUNK
    gates = zs[:, GATE_I:GATE_I + 2 * heads_b].reshape(bsz, seq, 2 * heads_b)
    gt = jnp.swapaxes(gates, 1, 2)
    if seq < chunk_b:
        gt = jnp.pad(gt, ((0, 0), (0, 0), (0, chunk_b - seq)))
    ob, c_f, n_f, m_f = _mlstm(z, zs, gt, w["mlstm_gate_b"][l], w["mlstm_norm_g"][l],
                               mstate[0], mstate[1], mstate[2], seq, chunk_b, (3 * wa) // wb)

    chunk_c = GLA_CHUNK if seq % GLA_CHUNK == 0 else SAMPLE_CHUNK
    kw = heads_c * dk_c
    vw = heads_c * LANES
    c0 = 3 * wa + 4 * wb
    oc, s_f = _gla(z, zs, w["gla_w_alpha"][l], w["gla_b_alpha"][l], w["gla_norm_g"][l], gstate,
                   seq, chunk_c, c0 // kw, c0 // kw + 1, (c0 + 2 * kw) // vw, (c0 + 2 * kw) // vw + 1)

    x1, h2 = _proj_out(oa, ob, oc, w["w_out"], l, x, w["norm_ffn_g"][l])
    if prompt:
        act, tg, tv = _ffn_up_prompt(h2, w["w_up"], l, w["ffn_conv_w"][l], w["ffn_conv_b"][l], bsz, seq)
        conv_rows = jnp.concatenate([tg[:, 6:8], tv[:, 6:8]], axis=-1)
    else:
        act, ug, uv = _ffn_up_sample(h2, w["w_up"], l, w["ffn_conv_w"][l], w["ffn_conv_b"][l], conv_state, seq)
        u = jnp.concatenate([ug, uv], axis=-1).reshape(bsz, seq, -1)
        conv_rows = u[:, seq - 2:]
    x2 = _proj_down(act, w["w_down"], l, x1)

    k_rows = kn.reshape(bsz, seq, heads_a, LANES)
    v_rows = vr.reshape(bsz, seq, heads_a, LANES)
    return x2, (k_rows, v_rows, c_f, n_f, m_f[:, :, 0], s_f, conv_rows)


def kernel(x_prompt, x_sample, cache_k, cache_v, page_table, state_mlstm_C, state_mlstm_n, state_mlstm_m, state_gla_S, state_ffn_conv, norm_mix_g, w_in, q_norm_g, k_norm_g, diff_lambda, diff_subln_g, rel_bias, mlstm_gate_b, mlstm_norm_g, gla_w_alpha, gla_b_alpha, gla_norm_g, w_out, norm_ffn_g, ffn_w_up, ffn_conv_w, ffn_conv_b, ffn_w_down):
    depth = w_in.shape[0]
    bp, sp, d_model = x_prompt.shape
    bs, ss, _ = x_sample.shape
    heads_a, dv_a = cache_v.shape[3], cache_v.shape[4]
    dk_a = cache_k.shape[4] // 2
    heads_b, dk_b, dv_b = state_mlstm_C.shape[2:]
    heads_c, dk_c, dv_c = state_gla_S.shape[2:]
    rank = gla_w_alpha.shape[1]
    page = cache_k.shape[2]
    assert dv_a == LANES and 2 * dk_a == LANES and dk_b == LANES and dv_b == LANES and dv_c == LANES
    assert page >= MAX_DISTANCE and ATTN_TILE >= MAX_DISTANCE and rank <= GATE_I
    wa, wb = heads_a * LANES, heads_b * LANES
    n_main = 3 * wa + 4 * wb + 2 * heads_c * dk_c + 2 * heads_c * dv_c
    gate0 = 3 * wa + 4 * wb
    c0 = gate0 + 2 * heads_b
    assert w_in.shape[2] == n_main + 2 * heads_b + rank

    w_main = jnp.concatenate([w_in[:, :, :gate0], w_in[:, :, c0:c0 + n_main - gate0]], axis=-1).astype(BF16)
    w_small = jnp.zeros((depth, d_model, LANES), F32)
    w_small = w_small.at[:, :, :rank].set(w_in[:, :, n_main + 2 * heads_b:])
    w_small = w_small.at[:, :, GATE_I:GATE_I + 2 * heads_b].set(w_in[:, :, gate0:c0]).astype(BF16)
    w = dict(norm_mix_g=norm_mix_g, w_main=w_main, w_small=w_small, q_norm_g=q_norm_g, k_norm_g=k_norm_g,
             mlstm_gate_b=mlstm_gate_b, mlstm_norm_g=mlstm_norm_g, gla_w_alpha=gla_w_alpha,
             gla_b_alpha=gla_b_alpha, gla_norm_g=gla_norm_g, w_out=w_out.astype(BF16),
             norm_ffn_g=norm_ffn_g, w_up=ffn_w_up, ffn_conv_w=ffn_conv_w,
             ffn_conv_b=ffn_conv_b, w_down=ffn_w_down)
    dims = (heads_a, dk_a, heads_b, heads_c, dk_c)

    bias_p = _prompt_bias(rel_bias, ATTN_TILE)
    bias_pg, bias_new = _decode_bias(rel_bias, page, DECODE_PAGES_PER_STEP, ss)
    kc = cache_k.reshape(depth, cache_k.shape[1], page * heads_a, LANES)
    vc = cache_v.reshape(depth, cache_v.shape[1], page * heads_a, LANES)

    def prompt_attn(l, qn, knb, vt, lam_init):
        shp = (bp, sp, wa)
        o = _prompt_attention(qn.reshape(shp), knb.reshape(shp), vt, bias_p,
                              diff_lambda[l], diff_subln_g[l], lam_init, dk_a)
        return o.reshape(bp * sp, wa)

    def sample_attn(l, qn, knb, vb, lam_init):
        shp = (bs, ss, wa)
        o = _decode_attention(l, qn.reshape(shp), knb.reshape(shp), vb.reshape(shp), kc, vc, page_table,
                              bias_pg, bias_new, diff_lambda[l], diff_subln_g[l], lam_init, dk_a)
        return o.reshape(bs * ss, wa)

    xp = x_prompt.reshape(bp * sp, d_model)
    xs = x_sample.reshape(bs * ss, d_model)
    zero_m = (jnp.zeros((bp, heads_b, dk_b, dv_b), F32), jnp.zeros((bp, heads_b, dk_b), F32),
              jnp.zeros((bp, heads_b), F32))
    zero_g = jnp.zeros((bp, heads_c, dk_c, dv_c), F32)
    rows_p, rows_s = [], []
    for l in range(depth):
        xp, rp = _layer(xp, l, w, dims, prompt_attn, zero_m, zero_g, None, bp, sp)
        rows_p.append(rp)
        xs, rs = _layer(xs, l, w, dims, sample_attn,
                        (state_mlstm_C[l], state_mlstm_n[l], state_mlstm_m[l]), state_gla_S[l],
                        state_ffn_conv[l], bs, ss)
        rows_s.append(rs)

    def field(rows, i):
        return jnp.stack([r[i] for r in rows], axis=0)

    return (xp.reshape(bp, sp, d_model), xs.reshape(bs, ss, d_model),
            *[field(rows_p, i) for i in range(7)], *[field(rows_s, i) for i in range(7)])
```

```python
import functools
import math

import numpy as np
import jax
import jax.numpy as jnp
from jax import lax
from jax.experimental import pallas as pl
from jax.experimental.pallas import tpu as pltpu

F32 = jnp.float32
BF16 = jnp.bfloat16

LANES = 128
VMEM_LIMIT = 52 * 1024 * 1024
EPS = 1e-6
NEG = -1e30
LOG2E = math.log2(math.e)
GLA_TAU = 16.0
MAX_DISTANCE = 128
ATTN_TILE = 256
SCAN_CHUNK = 256
GLA_CHUNK = 128
SAMPLE_CHUNK = 128
DECODE_PAGES_PER_STEP = 16


def _cparams(*sem):
    return pltpu.CompilerParams(dimension_semantics=sem, vmem_limit_bytes=VMEM_LIMIT)


def _split_bf16(x):
    hi = x.astype(BF16)
    lo = (x - hi.astype(F32)).astype(BF16)
    return hi, lo


def _dot(a, b):
    return jnp.dot(a, b, preferred_element_type=F32)


def _dot_nt(a, b):
    return lax.dot_general(a, b, (((1,), (1,)), ((), ())), preferred_element_type=F32)


def _dot_tn(a, b):
    return lax.dot_general(a, b, (((0,), (0,)), ((), ())), preferred_element_type=F32)


def _log_sigmoid(x):
    return jnp.minimum(x, 0.0) - jnp.log(1.0 + jnp.exp(-jnp.abs(x)))


def _sigmoid(x):
    return 1.0 / (1.0 + jnp.exp(-x))


def _div_pow2(x, n):
    assert n & (n - 1) == 0
    return lax.shift_right_logical(x, n.bit_length() - 1)


def _mod_pow2(x, n):
    assert n & (n - 1) == 0
    return x & (n - 1)


def _mixer_dtype(block_rows):
    return BF16 if block_rows % 16 == 0 else F32


def _pad_rows(x, rows):
    if x.shape[0] == rows:
        return x
    return jnp.concatenate([x, jnp.zeros((rows - x.shape[0], x.shape[1]), x.dtype)], axis=0)


def _proj_down_kernel(a_ref, w_ref, r_ref, o_ref, w_sc):
    @pl.when(pl.program_id(1) == 0)
    def _cast_weights():
        w_sc[...] = w_ref[...].astype(BF16)

    o_ref[...] = r_ref[...] + _dot(a_ref[...], w_sc[...])


def _proj_down(a, w, layer, res):
    m, k = a.shape
    n = w.shape[2]
    tm = min(m, 512)
    tn = 512
    assert m % tm == 0 and n % tn == 0
    return pl.pallas_call(
        _proj_down_kernel,
        out_shape=jax.ShapeDtypeStruct((m, n), F32),
        grid=(n // tn, m // tm),
        in_specs=[pl.BlockSpec((tm, k), lambda j, i: (i, 0)),
                  pl.BlockSpec((None, k, tn), lambda j, i: (layer, 0, j)),
                  pl.BlockSpec((tm, tn), lambda j, i: (i, j))],
        out_specs=pl.BlockSpec((tm, tn), lambda j, i: (i, j)),
        scratch_shapes=[pltpu.VMEM((k, tn), BF16)],
        compiler_params=_cparams("parallel", "arbitrary"),
        name="proj_down",
    )(a, w, res)


def _rms_rows(x, g):
    ms = jnp.mean(x * x, axis=-1, keepdims=True)
    return x * lax.rsqrt(ms + EPS) * g


def _proj_in_kernel(x_ref, g_ref, w_ref, ws_ref, z_ref, zs_ref, h_sc):
    @pl.when(pl.program_id(1) == 0)
    def _norm():
        h_sc[...] = _rms_rows(x_ref[...], g_ref[...]).astype(BF16)
        zs_ref[...] = _dot(h_sc[...], ws_ref[...])

    z_ref[...] = _dot(h_sc[...], w_ref[...])


def _proj_in(x, g, w_main, w_small, layer):
    m, d = x.shape
    n = w_main.shape[2]
    ns = w_small.shape[2]
    tm = min(m, 1024)
    tn = 512
    assert m % tm == 0 and n % tn == 0
    return pl.pallas_call(
        _proj_in_kernel,
        out_shape=(jax.ShapeDtypeStruct((m, n), F32), jax.ShapeDtypeStruct((m, ns), F32)),
        grid=(m // tm, n // tn),
        in_specs=[pl.BlockSpec((tm, d), lambda i, j: (i, 0)),
                  pl.BlockSpec((1, d), lambda i, j: (0, 0)),
                  pl.BlockSpec((None, d, tn), lambda i, j: (layer, 0, j)),
                  pl.BlockSpec((None, d, ns), lambda i, j: (layer, 0, 0))],
        out_specs=(pl.BlockSpec((tm, tn), lambda i, j: (i, j)),
                   pl.BlockSpec((tm, ns), lambda i, j: (i, 0))),
        scratch_shapes=[pltpu.VMEM((tm, d), BF16)],
        compiler_params=_cparams("parallel", "arbitrary"),
        name="proj_in",
    )(x, g.reshape(1, d), w_main, w_small)


def _proj_out_kernel(oa_ref, ob_ref, oc_ref, w_ref, r_ref, g_ref, x_ref, h_ref):
    ka, kb = oa_ref.shape[1], ob_ref.shape[1]
    acc = _dot(oa_ref[...].astype(BF16), w_ref[0:ka, :])
    acc += _dot(ob_ref[...].astype(BF16), w_ref[ka:ka + kb, :])
    acc += _dot(oc_ref[...].astype(BF16), w_ref[ka + kb:, :])
    x = r_ref[...] + acc
    x_ref[...] = x
    h_ref[...] = _rms_rows(x, g_ref[...]).astype(BF16)


def _proj_out(oa, ob, oc, w, layer, res, g):
    m = oa.shape[0]
    k, n = w.shape[1], w.shape[2]
    tm = min(m, 512)
    assert m % tm == 0 and oa.shape[1] + ob.shape[1] + oc.shape[1] == k
    rows = lambda width: pl.BlockSpec((tm, width), lambda i: (i, 0))
    return pl.pallas_call(
        _proj_out_kernel,
        out_shape=(jax.ShapeDtypeStruct((m, n), F32), jax.ShapeDtypeStruct((m, n), BF16)),
        grid=(m // tm,),
        in_specs=[rows(oa.shape[1]), rows(ob.shape[1]), rows(oc.shape[1]),
                  pl.BlockSpec((None, k, n), lambda i: (layer, 0, 0)),
                  rows(n),
                  pl.BlockSpec((1, n), lambda i: (0, 0))],
        out_specs=(rows(n), rows(n)),
        compiler_params=_cparams("parallel"),
        name="proj_out",
    )(oa, ob, oc, w, res, g.reshape(1, n))


def _qknorm_kernel(aq_ref, ak_ref, av_ref, qg_ref, kg_ref, bd_ref,
                   qn_ref, kn_ref, knb_ref, vb_ref, vr_ref, *, heads, inv_dk, scale, v_transposed):
    bd = bd_ref[...]
    qg = qg_ref[...]
    kg = kg_ref[...]

    def norm(x, g):
        hi, lo = _split_bf16(x * x)
        ss = _dot(hi, bd) + _dot(lo, bd)
        return x * lax.rsqrt(ss * inv_dk + EPS) * g

    for h in range(heads):
        sl = slice(LANES * h, LANES * (h + 1))
        qn_ref[:, sl] = (norm(aq_ref[:, sl], qg) * scale).astype(BF16)
        kn = norm(ak_ref[:, sl], kg)
        kn_ref[:, sl] = kn
        knb_ref[:, sl] = kn.astype(BF16)
        if v_transposed:
            vb_ref[0, sl, :] = av_ref[:, sl].T.astype(BF16)
    if not v_transposed:
        vb_ref[...] = av_ref[...].astype(BF16)
    vr_ref[...] = av_ref[...]


def _qknorm(z, qg, kg, heads, dk, seq, v_transposed):
    m = z.shape[0]
    w = heads * LANES
    tm = min(m, 512)
    assert m % tm == 0 and (seq % tm == 0 or not v_transposed)
    lane = np.arange(LANES)
    bd = jnp.asarray((lane[:, None] // dk) == (lane[None, :] // dk), BF16)
    reps = LANES // dk
    kern = functools.partial(_qknorm_kernel, heads=heads, inv_dk=1.0 / dk, scale=dk ** -0.5 * LOG2E,
                             v_transposed=v_transposed)
    rows = pl.BlockSpec((tm, w), lambda i: (i, 0))
    if v_transposed:
        tps = seq // tm
        v_shape = jax.ShapeDtypeStruct((m // seq, w, seq), BF16)
        v_spec = pl.BlockSpec((1, w, tm), lambda i: (i // tps, 0, i % tps))
    else:
        v_shape = jax.ShapeDtypeStruct((m, w), BF16)
        v_spec = rows
    return pl.pallas_call(
        kern,
        out_shape=(jax.ShapeDtypeStruct((m, w), BF16), jax.ShapeDtypeStruct((m, w), F32),
                   jax.ShapeDtypeStruct((m, w), BF16), v_shape, jax.ShapeDtypeStruct((m, w), F32)),
        grid=(m // tm,),
        in_specs=[pl.BlockSpec((tm, w), lambda i: (i, 0)),
                  pl.BlockSpec((tm, w), lambda i: (i, 1)),
                  pl.BlockSpec((tm, w), lambda i: (i, 2)),
                  pl.BlockSpec((1, LANES), lambda i: (0, 0)),
                  pl.BlockSpec((1, LANES), lambda i: (0, 0)),
                  pl.BlockSpec((LANES, LANES), lambda i: (0, 0))],
        out_specs=(rows, rows, rows, v_spec, rows),
        compiler_params=_cparams("parallel"),
        name="qknorm",
    )(z, z, z, jnp.tile(qg, reps).reshape(1, LANES), jnp.tile(kg, reps).reshape(1, LANES), bd)


def _bucket(n, n_buckets):
    max_exact = n_buckets // 2
    nf = jnp.maximum(n, max_exact).astype(F32)
    large = max_exact + (jnp.log(nf / max_exact) / math.log(MAX_DISTANCE / max_exact)
                         * (n_buckets - max_exact)).astype(jnp.int32)
    large = jnp.minimum(large, n_buckets - 1)
    return jnp.where(n < max_exact, jnp.maximum(n, 0), large)


def _prompt_bias_kernel(rb_ref, o_ref, *, tile, n_buckets):
    h = pl.program_id(0)
    j = lax.broadcasted_iota(jnp.int32, (tile, tile), 0)
    i = lax.broadcasted_iota(jnp.int32, (tile, tile), 1)
    far = jnp.full((tile, tile), MAX_DISTANCE, jnp.int32)
    for t, n in enumerate((i - j, tile + i - j, far)):
        b = _bucket(n, n_buckets)
        val = jnp.full((tile, tile), rb_ref[0, h], F32)
        for k in range(1, n_buckets):
            val = jnp.where(b == k, rb_ref[k, h], val)
        o_ref[0, t] = jnp.where(n >= 0, val * LOG2E, NEG)


def _prompt_bias(rel_bias, tile):
    n_buckets, heads = rel_bias.shape
    kern = functools.partial(_prompt_bias_kernel, tile=tile, n_buckets=n_buckets)
    return pl.pallas_call(
        kern,
        out_shape=jax.ShapeDtypeStruct((heads, 3, tile, tile), F32),
        grid=(heads,),
        in_specs=[pl.BlockSpec(memory_space=pltpu.SMEM)],
        out_specs=pl.BlockSpec((1, 3, tile, tile), lambda h: (h, 0, 0, 0)),
        compiler_params=_cparams("parallel"),
        name="prompt_bias",
    )(rel_bias)


def _decode_bias_kernel(rb_ref, pg_ref, new_ref, *, page, ppb, dec_seq, heads, n_buckets):
    grp = 2 * dec_seq

    def lookup(n, h):
        b = _bucket(n, n_buckets)
        val = jnp.full(n.shape, rb_ref[0, h], F32)
        for k in range(1, n_buckets):
            val = jnp.where(b == k, rb_ref[k, h], val)
        return jnp.where(n >= 0, val * LOG2E, NEG)

    qi = _mod_pow2(lax.broadcasted_iota(jnp.int32, (grp, page), 0), dec_seq)
    j = lax.broadcasted_iota(jnp.int32, (grp, page), 1)
    far = jnp.full((grp, page), MAX_DISTANCE, jnp.int32)
    for h in range(heads):
        rs = slice(grp * h, grp * (h + 1))
        far_b = lookup(far, h)
        for c in range(ppb):
            pg_ref[0, rs, page * c:page * (c + 1)] = far_b
            pg_ref[1, rs, page * c:page * (c + 1)] = far_b if c < ppb - 1 else lookup(page + qi - j, h)
        new_ref[rs, :] = lookup(jnp.where(j < dec_seq, qi - j, -1), h)


def _decode_bias(rel_bias, page, ppb, dec_seq):
    n_buckets, heads = rel_bias.shape
    rows = 2 * dec_seq * heads
    kern = functools.partial(_decode_bias_kernel, page=page, ppb=ppb, dec_seq=dec_seq, heads=heads,
                             n_buckets=n_buckets)
    return pl.pallas_call(
        kern,
        out_shape=(jax.ShapeDtypeStruct((2, rows, ppb * page), F32),
                   jax.ShapeDtypeStruct((rows, page), F32)),
        in_specs=[pl.BlockSpec(memory_space=pltpu.SMEM)],
        name="decode_bias",
    )(rel_bias)


def _diff_lambda(dl, lam_init):
    s1 = jnp.sum(dl[0:1] * dl[1:2], axis=-1, keepdims=True)
    s2 = jnp.sum(dl[2:3] * dl[3:4], axis=-1, keepdims=True)
    return jnp.exp(s1) - jnp.exp(s2) + lam_init


def _subln(o, g, lam_init):
    ms = jnp.mean(o * o, axis=-1, keepdims=True)
    return o * lax.rsqrt(ms + EPS) * g * (1.0 - lam_init)


def _flash_update(s, m_prev, l_prev):
    m_new = jnp.maximum(m_prev, jnp.max(s, axis=-1, keepdims=True))
    alpha = jnp.exp2(m_prev - m_new)
    p = jnp.exp2(s - m_new)
    l_new = alpha * l_prev + jnp.sum(p, axis=-1, keepdims=True)
    return p, alpha, m_new, l_new


def _attn_kernel(qt_ref, kt_ref, q_ref, k_ref, vt_ref, bias_ref, dl_ref, g_ref, o_ref, m_sc, l_sc, acc_sc, qm_sc,
                 *, heads, tile, dk, lam_init):
    qi = qt_ref[pl.program_id(1)]
    ki = kt_ref[pl.program_id(1)]
    n_sub = tile // LANES

    @pl.when(ki == 0)
    def _init():
        m_sc[...] = jnp.full(m_sc.shape, -jnp.inf, F32)
        l_sc[...] = jnp.zeros(l_sc.shape, F32)
        acc_sc[...] = jnp.zeros(acc_sc.shape, F32)
        first_map = _mod_pow2(lax.broadcasted_iota(jnp.int32, (LANES, heads * LANES), 1), LANES) < dk
        for c in range(2 * n_sub):
            qc = q_ref[0, LANES * (c % n_sub):LANES * (c % n_sub + 1), :]
            qm_sc[c] = jnp.where(first_map == (c < n_sub), qc, jnp.zeros_like(qc))

    def absorb(far):
        for h in range(heads):
            sl = slice(LANES * h, LANES * (h + 1))
            kh = k_ref[0, :, sl]
            vht = vt_ref[0, sl, :]
            for c in range(2 * n_sub):
                qrows = slice(LANES * (c % n_sub), LANES * (c % n_sub + 1))
                cs = slice(LANES * c, LANES * (c + 1))
                s = _dot_nt(kh, qm_sc[c, :, sl])
                m_prev = m_sc[h, :, cs]
                if far:
                    const = bias_ref[h, 0, 0:1, 0:1]
                    m_new = jnp.maximum(m_prev, jnp.max(s, axis=0, keepdims=True) + const)
                    p = jnp.exp2(s - (m_new - const))
                else:
                    s = s + bias_ref[h, 0, :, qrows]
                    m_new = jnp.maximum(m_prev, jnp.max(s, axis=0, keepdims=True))
                    p = jnp.exp2(s - m_new)
                alpha = jnp.exp2(m_prev - m_new)
                l_sc[h, :, cs] = alpha * l_sc[h, :, cs] + jnp.sum(p, axis=0, keepdims=True)
                acc_sc[h, :, cs] = alpha * acc_sc[h, :, cs] + _dot(vht, p.astype(BF16))
                m_sc[h, :, cs] = m_new

    pl.when(ki < qi - 1)(lambda: absorb(True))
    pl.when(ki >= qi - 1)(lambda: absorb(False))

    @pl.when(ki == qi)
    def _finish():
        lam = _diff_lambda(dl_ref[...], lam_init)
        g_col = g_ref[...]
        for h in range(heads):
            acc = acc_sc[h]
            l = l_sc[h]
            ot = acc[:, :tile] / l[:, :tile] - lam * (acc[:, tile:] / l[:, tile:])
            ms = jnp.mean(ot * ot, axis=0, keepdims=True)
            ot = ot * lax.rsqrt(ms + EPS) * g_col * (1.0 - lam_init)
            o_ref[0, :, LANES * h:LANES * (h + 1)] = ot.T.astype(o_ref.dtype)


def _prompt_attention(qn, knb, vt, bias, dl, g, lam_init, dk):
    bsz, seq, w = qn.shape
    heads = w // LANES
    tile = bias.shape[-1]
    nq = seq // tile
    assert seq % tile == 0 and tile % LANES == 0
    kern = functools.partial(_attn_kernel, heads=heads, tile=tile, dk=dk, lam_init=lam_init)
    pairs = [(qi, ki) for qi in range(nq) for ki in range(qi + 1)]
    q_tab = jnp.asarray([p[0] for p in pairs], jnp.int32)
    k_tab = jnp.asarray([p[1] for p in pairs], jnp.int32)

    def bias_idx(b, t, qt, kt):
        return (0, jnp.where(kt[t] == qt[t], 0, jnp.where(kt[t] == qt[t] - 1, 1, 2)), 0, 0)

    grid_spec = pltpu.PrefetchScalarGridSpec(
        num_scalar_prefetch=2,
        grid=(bsz, len(pairs)),
        in_specs=[pl.BlockSpec((1, tile, w), lambda b, t, qt, kt: (b, qt[t], 0)),
                  pl.BlockSpec((1, tile, w), lambda b, t, qt, kt: (b, kt[t], 0)),
                  pl.BlockSpec((1, w, tile), lambda b, t, qt, kt: (b, 0, kt[t])),
                  pl.BlockSpec((heads, 1, tile, tile), bias_idx),
                  pl.BlockSpec(dl.shape, lambda b, t, qt, kt: (0, 0)),
                  pl.BlockSpec((LANES, 1), lambda b, t, qt, kt: (0, 0))],
        out_specs=pl.BlockSpec((1, tile, w), lambda b, t, qt, kt: (b, qt[t], 0)),
        scratch_shapes=[pltpu.VMEM((heads, 1, 2 * tile), F32),
                        pltpu.VMEM((heads, 1, 2 * tile), F32),
                        pltpu.VMEM((heads, LANES, 2 * tile), F32),
                        pltpu.VMEM((2 * tile // LANES, LANES, w), BF16)],
    )
    return pl.pallas_call(
        kern,
        out_shape=jax.ShapeDtypeStruct((bsz, seq, w), BF16),
        grid_spec=grid_spec,
        compiler_params=_cparams("parallel", "arbitrary"),
        name="prompt_attention",
    )(q_tab, k_tab, qn, knb, vt, bias, dl, g.reshape(LANES, 1))


def _decode_attn_kernel(pt_ref, q_ref, kn_ref, vn_ref, *refs, heads, dec_seq, page, ppb, dk, lam_init):
    del pt_ref
    kc_refs, vc_refs = refs[:ppb], refs[ppb:2 * ppb]
    bias_ref, bnew_ref, dl_ref, g_ref, o_ref, wq_sc, wqb_sc, m_sc, l_sc, acc_sc = refs[2 * ppb:]
    p_idx = pl.program_id(1)
    rows = 2 * dec_seq * heads
    grp = 2 * dec_seq
    width = heads * LANES

    @pl.when(p_idx == 0)
    def _init():
        q = q_ref[0].astype(F32)
        lane = lax.broadcasted_iota(jnp.int32, (dec_seq, LANES), 1)
        pieces = []
        for h in range(heads):
            qh = q[:, LANES * h:LANES * (h + 1)]
            pieces += [jnp.where(lane < dk, qh, 0.0), jnp.where(lane >= dk, qh, 0.0)]
        wq = jnp.concatenate(pieces, axis=0)
        wq_sc[...] = wq.astype(BF16)
        c = lax.broadcasted_iota(jnp.int32, (rows, width), 1)
        r = lax.broadcasted_iota(jnp.int32, (rows, width), 0)
        own = _div_pow2(r, grp) == _div_pow2(c, LANES)
        wqb_sc[...] = jnp.where(own, jnp.concatenate([wq] * heads, axis=1), 0.0).astype(BF16)
        m_sc[...] = jnp.full(m_sc.shape, -jnp.inf, F32)
        l_sc[...] = jnp.zeros(l_sc.shape, F32)
        acc_sc[...] = jnp.zeros(acc_sc.shape, F32)

    def head_rows(ref, h):
        return ref[pl.ds(h, page, stride=heads), :].astype(BF16)

    s = jnp.concatenate(
        [jnp.concatenate([_dot_nt(wq_sc[grp * h:grp * (h + 1), :], head_rows(kc, h)) for h in range(heads)],
                         axis=0) for kc in kc_refs], axis=1) + bias_ref[0]
    p, alpha, m_new, l_new = _flash_update(s, m_sc[...], l_sc[...])
    m_sc[...] = m_new
    l_sc[...] = l_new
    pb = p.astype(BF16)
    for h in range(heads):
        rs = slice(grp * h, grp * (h + 1))
        acc = alpha[rs] * acc_sc[rs, :]
        for j, vc in enumerate(vc_refs):
            acc += _dot(pb[rs, page * j:page * (j + 1)], head_rows(vc, h))
        acc_sc[rs, :] = acc

    @pl.when(p_idx == pl.num_programs(1) - 1)
    def _finish():
        kn = _pad_rows(kn_ref[0].astype(F32), page).astype(BF16)
        vn = _pad_rows(vn_ref[0].astype(F32), page)
        s = _dot_nt(wqb_sc[...], kn) + bnew_ref[...]
        p, alpha, _, l_fin = _flash_update(s, m_sc[...], l_sc[...])
        pb = p.astype(BF16)
        lam = _diff_lambda(dl_ref[...], lam_init)
        g = g_ref[...]
        for h in range(heads):
            rs = slice(grp * h, grp * (h + 1))
            vh = vn[:, LANES * h:LANES * (h + 1)].astype(BF16)
            acc = alpha[rs] * acc_sc[rs, :] + _dot(pb[rs], vh)
            l = l_fin[rs]
            o = acc[:dec_seq] / l[:dec_seq] - lam * (acc[dec_seq:] / l[dec_seq:])
            o_ref[0, :, LANES * h:LANES * (h + 1)] = _subln(o, g, lam_init)


def _decode_attention(layer, qn, knb, vb, cache_k, cache_v, page_table, bias_pg, bias_new, dl, g, lam_init, dk):
    bsz, dec_seq, w = qn.shape
    heads = w // LANES
    page = cache_k.shape[2] // heads
    ppb = bias_pg.shape[-1] // page
    n_steps = page_table.shape[1] // ppb
    assert page_table.shape[1] % ppb == 0
    rows = 2 * dec_seq * heads
    kern = functools.partial(_decode_attn_kernel, heads=heads, dec_seq=dec_seq, page=page, ppb=ppb, dk=dk,
                             lam_init=lam_init)
    new_spec = pl.BlockSpec((1, dec_seq, w), lambda b, p, pt: (b, 0, 0))
    cache_specs = [pl.BlockSpec((None, None, page * heads, LANES),
                                lambda b, p, pt, j=j: (layer, pt[b, p * ppb + j], 0, 0)) for j in range(ppb)]
    grid_spec = pltpu.PrefetchScalarGridSpec(
        num_scalar_prefetch=1,
        grid=(bsz, n_steps),
        in_specs=[new_spec, new_spec, new_spec, *cache_specs, *cache_specs,
                  pl.BlockSpec((1, rows, ppb * page),
                               lambda b, p, pt: (jnp.where(p == n_steps - 1, 1, 0), 0, 0)),
                  pl.BlockSpec((rows, page), lambda b, p, pt: (0, 0)),
                  pl.BlockSpec(dl.shape, lambda b, p, pt: (0, 0)),
                  pl.BlockSpec((1, LANES), lambda b, p, pt: (0, 0))],
        out_specs=pl.BlockSpec((1, dec_seq, w), lambda b, p, pt: (b, 0, 0)),
        scratch_shapes=[pltpu.VMEM((rows, LANES), BF16),
                        pltpu.VMEM((rows, w), BF16),
                        pltpu.VMEM((rows, 1), F32),
                        pltpu.VMEM((rows, 1), F32),
                        pltpu.VMEM((rows, LANES), F32)],
    )
    return pl.pallas_call(
        kern,
        out_shape=jax.ShapeDtypeStruct((bsz, dec_seq, w), F32),
        grid_spec=grid_spec,
        compiler_params=_cparams("parallel", "arbitrary"),
        name="decode_attention",
    )(page_table, qn, knb, vb, *([cache_k] * ppb), *([cache_v] * ppb), bias_pg, bias_new, dl,
      g.reshape(1, LANES))


GATE_I = 16
GATE_F = 20


def _mlstm_kernel(q_ref, k_ref, v_ref, og_ref, zs_ref, gt_ref, gbl_ref, gbc_ref, g_ref,
                  c0_ref, n0_ref, m0_ref,
                  o_ref, cf_ref, nf_ref, mf_ref, c_sc, n_sc, m_sc,
                  *, heads, chunk, valid, scale):
    c_idx = pl.program_id(1)

    @pl.when(c_idx == 0)
    def _init():
        for h in range(heads):
            c_sc[h] = c0_ref[0, h].T
        n_sc[...] = n0_ref[0]
        m_sc[...] = m0_ref[0]

    rows_in = q_ref.shape[0]
    row = lax.broadcasted_iota(jnp.int32, (chunk, 1), 0)
    s_i = lax.broadcasted_iota(jnp.int32, (chunk, chunk), 0)
    t_i = lax.broadcasted_iota(jnp.int32, (chunk, chunk), 1)
    causal = s_i <= t_i
    incl = causal.astype(BF16)
    incl_t = (s_i >= t_i).astype(BF16)

    gcol = _pad_rows(zs_ref[...], chunk) + gbl_ref[...]
    lf_mat = _log_sigmoid(gcol)
    if valid < chunk:
        lf_mat = jnp.where(row < valid, lf_mat, 0.0)
    hi, lo = _split_bf16(lf_mat)
    b_mat = _dot(incl_t, hi) + _dot(incl_t, lo)

    grow = gt_ref[0] + gbc_ref[...]
    col = lax.broadcasted_iota(jnp.int32, grow.shape, 1)
    grow_id = lax.broadcasted_iota(jnp.int32, grow.shape, 0)
    lf_rows = jnp.where(grow_id >= heads, _log_sigmoid(grow), 0.0)
    if valid < chunk:
        lf_rows = jnp.where(col < valid, lf_rows, 0.0)
    hi, lo = _split_bf16(lf_rows)
    b_rows = _dot(hi, incl) + _dot(lo, incl)

    g = g_ref[...]
    first_row = lax.broadcasted_iota(jnp.int32, (16, 1), 0) == 0
    for h in range(heads):
        sl = slice(LANES * h, LANES * (h + 1))
        li_col = gcol[:, GATE_I + h:GATE_I + h + 1]
        if valid < chunk:
            li_col = jnp.where(row < valid, li_col, NEG)
        u_col = li_col - b_mat[:, GATE_F + h:GATE_F + h + 1]
        b_row = b_rows[heads + h:heads + h + 1, :]
        dt = jnp.where(causal, b_row + u_col, NEG)
        m_prev = m_sc[h:h + 1, 0:1]
        inter = b_row + m_prev
        mt = jnp.maximum(inter, jnp.max(dt, axis=0, keepdims=True))
        wi = jnp.exp(inter - mt)
        q = _pad_rows(q_ref[:, sl], chunk)
        ks = _pad_rows(k_ref[:, sl], chunk) * scale
        vtb = _pad_rows(v_ref[:, sl], chunk).T.astype(BF16)
        qb = q.astype(BF16)
        pt = _dot_nt(ks.astype(BF16), qb) * jnp.exp(dt - mt)
        ct_prev = c_sc[h]
        n_prev = n_sc[h:h + 1, :]
        num = wi * _dot_nt(ct_prev.astype(BF16), qb) + _dot(vtb, pt.astype(BF16))
        n_rows = jnp.where(first_row, jnp.broadcast_to(n_prev, (16, LANES)), 0.0)
        qn = _dot_nt(n_rows.astype(BF16), qb)[0:1]
        den = wi * qn + jnp.sum(pt, axis=0, keepdims=True)
        ht = num / jnp.maximum(jnp.abs(den), jnp.exp(-mt))
        og = _sigmoid(og_ref[:, sl])
        y = og * ht.T[:rows_in]
        ms = jnp.mean(y * y, axis=-1, keepdims=True)
        o_ref[:, sl] = (y * lax.rsqrt(ms + EPS) * g).astype(o_ref.dtype)

        m_new = mt[:, chunk - 1:chunk]
        b_last = b_row[:, chunk - 1:chunk]
        a = jnp.exp(b_last + m_prev - m_new)
        kw = ks * jnp.exp(u_col + (b_last - m_new))
        c_sc[h] = a * ct_prev + _dot(vtb, kw.astype(BF16))
        n_sc[h:h + 1, :] = a * n_prev + jnp.sum(kw, axis=0, keepdims=True)
        m_sc[h:h + 1, :] = jnp.broadcast_to(m_new, (1, LANES))

    @pl.when(c_idx == pl.num_programs(1) - 1)
    def _finish():
        for h in range(heads):
            cf_ref[0, h] = c_sc[h].T
        nf_ref[0] = n_sc[...]
        mf_ref[0] = m_sc[...]


def _mlstm(z, zs, gt, gate_b, g, c0, n0, m0, seq, chunk, col0):
    bsz, heads = c0.shape[0], c0.shape[1]
    w = heads * LANES
    rows_in = min(seq, chunk)
    nc = seq // rows_in
    valid = rows_in
    gt_w = gt.shape[-1] // nc
    kern = functools.partial(_mlstm_kernel, heads=heads, chunk=chunk, valid=valid, scale=LANES ** -0.5)
    gbl = jnp.zeros((1, LANES), F32)
    gbl = gbl.at[0, GATE_I:GATE_I + heads].set(gate_b[0]).at[0, GATE_F:GATE_F + heads].set(gate_b[1])
    gbc = gate_b.reshape(2 * heads, 1)
    m0b = jnp.broadcast_to(m0[:, :, None], (bsz, heads, LANES))

    def zspec(blk):
        return pl.BlockSpec((rows_in, w), lambda b, c: (b * nc + c, col0 + blk))

    state = lambda shape: pl.BlockSpec((1,) + shape, lambda b, c: (b,) + (0,) * len(shape))
    return pl.pallas_call(
        kern,
        out_shape=(jax.ShapeDtypeStruct((bsz * seq, w), _mixer_dtype(rows_in)),
                   jax.ShapeDtypeStruct((bsz, heads, LANES, LANES), F32),
                   jax.ShapeDtypeStruct((bsz, heads, LANES), F32),
                   jax.ShapeDtypeStruct((bsz, heads, LANES), F32)),
        grid=(bsz, nc),
        in_specs=[zspec(0), zspec(1), zspec(2), zspec(3),
                  pl.BlockSpec((rows_in, LANES), lambda b, c: (b * nc + c, 0)),
                  pl.BlockSpec((1, 2 * heads, gt_w), lambda b, c: (b, 0, c)),
                  pl.BlockSpec((1, LANES), lambda b, c: (0, 0)),
                  pl.BlockSpec((2 * heads, 1), lambda b, c: (0, 0)),
                  pl.BlockSpec((1, LANES), lambda b, c: (0, 0)),
                  state((heads, LANES, LANES)), state((heads, LANES)), state((heads, LANES))],
        out_specs=(pl.BlockSpec((rows_in, w), lambda b, c: (b * nc + c, 0)),
                   state((heads, LANES, LANES)), state((heads, LANES)), state((heads, LANES))),
        scratch_shapes=[pltpu.VMEM((heads, LANES, LANES), F32),
                        pltpu.VMEM((heads, LANES), F32),
                        pltpu.VMEM((heads, LANES), F32)],
        compiler_params=_cparams("parallel", "arbitrary"),
        name="mlstm",
    )(z, z, z, z, zs, gt, gbl, gbc, g.reshape(1, LANES), c0, n0, m0b)


def _gla_levels(chunk):
    n, out = chunk, []
    while n >= 2:
        out.append(n)
        n //= 2
    return out


def _gla_weights(chunk):
    t = np.arange(chunk)[:, None]
    s = np.arange(chunk)[None, :]
    blocks = [(s <= t).astype(np.float32), (s > t).astype(np.float32)]
    for n in _gla_levels(chunk):
        mid = (t // n) * n + n // 2 - 1
        blocks.append(((s > mid) & (s <= t)).astype(np.float32) - ((s > t) & (s <= mid)).astype(np.float32))
    return jnp.asarray(np.concatenate(blocks, axis=0), BF16)


def _gla_kernel(q_ref, k_ref, v_ref, gg_ref, zs_ref, wa_ref, ba_ref, ws_ref, g_ref, s0_ref,
                o_ref, sf_ref, s_sc, *, heads, chunk, valid, dk, scale):
    c_idx = pl.program_id(1)
    per = LANES // dk
    assert per == 2 and heads % per == 0

    @pl.when(c_idx == 0)
    def _init():
        for h in range(heads):
            s_sc[h // per, dk * (h % per):dk * (h % per + 1), :] = s0_ref[0, h]

    rows_in = q_ref.shape[0]
    row = lax.broadcasted_iota(jnp.int32, (chunk, 1), 0)
    t_i = lax.broadcasted_iota(jnp.int32, (per * chunk, chunk), 0) & (chunk - 1)
    s_i = lax.broadcasted_iota(jnp.int32, (per * chunk, chunk), 1)
    first_head = lax.broadcasted_iota(jnp.int32, (chunk, LANES), 1) < dk

    zs = _pad_rows(zs_ref[...], chunk)
    la = _log_sigmoid(_dot(zs.astype(BF16), wa_ref[...]) + ba_ref[...]) * (1.0 / GLA_TAU)
    if valid < chunk:
        la = jnp.where(row < valid, la, 0.0)
    hi, lo = _split_bf16(la)
    wst = ws_ref[...]
    e_all = _dot(wst, hi) + _dot(wst, lo)
    ones = jnp.ones((chunk, LANES), BF16)
    levels = _gla_levels(chunk)
    g = g_ref[...]
    second_half = [(row & (n - 1)) >= n // 2 for n in levels]
    same_node = [_div_pow2(t_i, n) == _div_pow2(s_i, n) for n in levels]

    def split_heads(x):
        return [jnp.where(first_head, x, 0.0), jnp.where(first_head, 0.0, x)]

    for j in range(heads // per):
        psl = slice(LANES * j, LANES * (j + 1))
        q = _pad_rows(q_ref[:, psl], chunk) * scale
        k = _pad_rows(k_ref[:, psl], chunk)
        bc = e_all[0:chunk, psl]
        rem = e_all[chunk:2 * chunk, psl]
        a_pair = jnp.zeros((per * chunk, chunk), F32)
        for li in range(len(levels)):
            e = e_all[(2 + li) * chunk:(3 + li) * chunk, psl]
            qt = jnp.where(second_half[li], q * jnp.exp(jnp.minimum(e, 0.0)), 0.0)
            kt = jnp.where(second_half[li], 0.0, k * jnp.exp(jnp.minimum(-e, 0.0)))
            qs = jnp.concatenate(split_heads(qt), axis=0).astype(BF16)
            a_pair = a_pair + jnp.where(same_node[li], _dot_nt(qs, kt.astype(BF16)), 0.0)
        s_prev = s_sc[j]
        s_prev_b = s_prev.astype(BF16)
        qd = split_heads(q * jnp.exp(bc))
        qk = split_heads(q * k)
        kd = split_heads(k * jnp.exp(rem))
        hi_p, lo_p = _split_bf16(la[:, psl])
        tot = _dot_tn(hi_p, ones) + _dot_tn(lo_p, ones)
        s_new = jnp.exp(tot) * s_prev
        for hh in range(per):
            h = per * j + hh
            vsl = slice(LANES * h, LANES * (h + 1))
            vb = _pad_rows(v_ref[:, vsl], chunk).astype(BF16)
            a_h = a_pair[chunk * hh:chunk * (hh + 1)]
            o = _dot(qd[hh].astype(BF16), s_prev_b) + _dot(a_h.astype(BF16), vb)
            o = o + jnp.sum(qk[hh], axis=-1, keepdims=True) * vb.astype(F32)
            o = o[:rows_in]
            ms = jnp.mean(o * o, axis=-1, keepdims=True)
            gate = gg_ref[:, vsl]
            o_ref[:, vsl] = (o * lax.rsqrt(ms + EPS) * g * (gate * _sigmoid(gate))).astype(o_ref.dtype)
            s_new = s_new + _dot_tn(kd[hh].astype(BF16), vb)
        s_sc[j] = s_new

    @pl.when(c_idx == pl.num_programs(1) - 1)
    def _finish():
        for h in range(heads):
            sf_ref[0, h] = s_sc[h // per, dk * (h % per):dk * (h % per + 1), :]


def _gla(z, zs, w_alpha, b_alpha, g, s0, seq, chunk, qcol, kcol, vcol, gcol):
    bsz, heads, dk, dv = s0.shape
    rows_in = min(seq, chunk)
    nc = seq // rows_in
    kw = heads * dk
    vw = heads * dv
    wa = jnp.zeros((LANES, kw), F32).at[:w_alpha.shape[0]].set(w_alpha).astype(BF16)
    wst = _gla_weights(chunk)
    kern = functools.partial(_gla_kernel, heads=heads, chunk=chunk, valid=rows_in, dk=dk, scale=dk ** -0.5)
    const = lambda shape: pl.BlockSpec(shape, lambda b, c: (0,) * len(shape))
    return pl.pallas_call(
        kern,
        out_shape=(jax.ShapeDtypeStruct((bsz * seq, vw), _mixer_dtype(rows_in)),
                   jax.ShapeDtypeStruct((bsz, heads, dk, dv), F32)),
        grid=(bsz, nc),
        in_specs=[pl.BlockSpec((rows_in, kw), lambda b, c: (b * nc + c, qcol)),
                  pl.BlockSpec((rows_in, kw), lambda b, c: (b * nc + c, kcol)),
                  pl.BlockSpec((rows_in, vw), lambda b, c: (b * nc + c, vcol)),
                  pl.BlockSpec((rows_in, vw), lambda b, c: (b * nc + c, gcol)),
                  pl.BlockSpec((rows_in, LANES), lambda b, c: (b * nc + c, 0)),
                  const((LANES, kw)), const((1, kw)), const(wst.shape), const((1, LANES)),
                  pl.BlockSpec((1, heads, dk, dv), lambda b, c: (b, 0, 0, 0))],
        out_specs=(pl.BlockSpec((rows_in, vw), lambda b, c: (b * nc + c, 0)),
                   pl.BlockSpec((1, heads, dk, dv), lambda b, c: (b, 0, 0, 0))),
        scratch_shapes=[pltpu.VMEM((kw // LANES, LANES, dv), F32)],
        compiler_params=_cparams("parallel", "arbitrary"),
        name="gla",
    )(z, z, z, z, zs, wa, b_alpha.reshape(1, kw), wst, g.reshape(1, LANES), s0)


def _conv_gate(ug, uv, cwg, cwv, cbg, cbv, prev):
    def conv(u, cw, cb, which):
        u1, u2 = prev(u, which)
        return cb + cw[0:1] * u2 + cw[1:2] * u1 + cw[2:3] * u
    cg = conv(ug, cwg, cbg, 0)
    cv = conv(uv, cwv, cbv, 1)
    return cg * _sigmoid(cg) * cv


FFN_SUB = 256


def _ffn_up_prompt_kernel(h_ref, wg_ref, wv_ref, cwg_ref, cwv_ref, cbg_ref, cbv_ref,
                          act_ref, tg_ref, tv_ref, u_sc, w_sc, *, tiles_per_seq):
    i = pl.program_id(1)
    first = (i % tiles_per_seq) == 0
    hb = h_ref[...]
    tm = hb.shape[0]

    @pl.when(i == 0)
    def _cast_weights():
        w_sc[0] = wg_ref[...].astype(BF16)
        w_sc[1] = wv_ref[...].astype(BF16)

    @pl.when(first)
    def _reset():
        u_sc[:, 0:8, :] = jnp.zeros((2, 8, u_sc.shape[2]), F32)

    def conv_half(which, u, cw_ref, cb_ref, t_ref, cs):
        cw = cw_ref[:, cs]
        u_sc[which, 8:8 + tm, cs] = u
        u1 = u_sc[which, 7:7 + tm, cs]
        u2 = u_sc[which, 6:6 + tm, cs]
        conv = cb_ref[:, cs] + cw[0:1] * u2 + cw[1:2] * u1 + cw[2:3] * u
        u_sc[which, 0:8, cs] = u[tm - 8:tm]
        t_ref[0, :, cs] = u[tm - 8:tm]
        return conv

    subs = [slice(c0, c0 + FFN_SUB) for c0 in range(0, wg_ref.shape[1], FFN_SUB)]
    for cs in subs:
        cg = conv_half(0, _dot(hb, w_sc[0, :, cs]), cwg_ref, cbg_ref, tg_ref, cs)
        cv = conv_half(1, _dot(hb, w_sc[1, :, cs]), cwv_ref, cbv_ref, tv_ref, cs)
        act_ref[:, cs] = (cg * _sigmoid(cg) * cv).astype(BF16)


def _ffn_up_prompt(h2, w_up, layer, conv_w, conv_b, bsz, seq):
    m, d = h2.shape
    dff = w_up.shape[2] // 2
    tm, tn = 1024, 512
    assert seq % tm == 0 and dff % tn == 0 and tn % FFN_SUB == 0
    nj = dff // tn
    tps = seq // tm
    kern = functools.partial(_ffn_up_prompt_kernel, tiles_per_seq=tps)
    wspec = lambda off: pl.BlockSpec((None, d, tn), lambda j, i: (layer, 0, j + off))
    cspec = lambda r, off: pl.BlockSpec((r, tn), lambda j, i: (0, j + off))
    tail = pl.BlockSpec((1, 8, tn), lambda j, i: (i // tps, 0, j))
    cb = conv_b.reshape(1, -1)
    return pl.pallas_call(
        kern,
        out_shape=(jax.ShapeDtypeStruct((m, dff), BF16),
                   jax.ShapeDtypeStruct((bsz, 8, dff), F32),
                   jax.ShapeDtypeStruct((bsz, 8, dff), F32)),
        grid=(nj, m // tm),
        in_specs=[pl.BlockSpec((tm, d), lambda j, i: (i, 0)),
                  wspec(0), wspec(nj), cspec(3, 0), cspec(3, nj), cspec(1, 0), cspec(1, nj)],
        out_specs=(pl.BlockSpec((tm, tn), lambda j, i: (i, j)), tail, tail),
        scratch_shapes=[pltpu.VMEM((2, tm + 8, tn), F32), pltpu.VMEM((2, d, tn), BF16)],
        compiler_params=_cparams("parallel", "arbitrary"),
        name="ffn_up_prompt",
    )(h2, w_up, w_up, conv_w, conv_w, cb, cb)


def _ffn_up_sample_kernel(h_ref, wg_ref, wv_ref, cwg_ref, cwv_ref, cbg_ref, cbv_ref,
                          stg_ref, stv_ref, act_ref, ug_ref, uv_ref, *, seq):
    hb = h_ref[...]
    ug = _dot(hb, wg_ref[...].astype(BF16))
    uv = _dot(hb, wv_ref[...].astype(BF16))
    tm = ug.shape[0]
    row = lax.broadcasted_iota(jnp.int32, (tm, 1), 0)
    states = (stg_ref[...], stv_ref[...])

    def prev(u, which):
        st = states[which]
        u1 = pltpu.roll(u, 1, 0)
        u2 = pltpu.roll(u, 2, 0)
        for b in range(tm // seq):
            older, newer = st[2 * b:2 * b + 1], st[2 * b + 1:2 * b + 2]
            u1 = jnp.where(row == seq * b, newer, u1)
            u2 = jnp.where(row == seq * b, older, jnp.where(row == seq * b + 1, newer, u2))
        return u1, u2

    act_ref[...] = _conv_gate(ug, uv, cwg_ref[...], cwv_ref[...], cbg_ref[...], cbv_ref[...], prev).astype(BF16)
    ug_ref[...] = ug
    uv_ref[...] = uv


def _ffn_up_sample(h2, w_up, layer, conv_w, conv_b, conv_state, seq):
    m, d = h2.shape
    dff = w_up.shape[2] // 2
    tn = 512
    assert dff % tn == 0
    nj = dff // tn
    bsz = m // seq
    assert seq >= 2
    st = conv_state.reshape(2 * bsz, 2 * dff)
    kern = functools.partial(_ffn_up_sample_kernel, seq=seq)
    wspec = lambda off: pl.BlockSpec((None, d, tn), lambda j: (layer, 0, j + off))
    cspec = lambda r, off: pl.BlockSpec((r, tn), lambda j: (0, j + off))
    cb = conv_b.reshape(1, -1)
    ospec = pl.BlockSpec((m, tn), lambda j: (0, j))
    return pl.pallas_call(
        kern,
        out_shape=(jax.ShapeDtypeStruct((m, dff), BF16),
                   jax.ShapeDtypeStruct((m, dff), F32),
                   jax.ShapeDtypeStruct((m, dff), F32)),
        grid=(nj,),
        in_specs=[pl.BlockSpec((m, d), lambda j: (0, 0)),
                  wspec(0), wspec(nj), cspec(3, 0), cspec(3, nj), cspec(1, 0), cspec(1, nj),
                  cspec(2 * bsz, 0), cspec(2 * bsz, nj)],
        out_specs=(ospec, ospec, ospec),
        compiler_params=_cparams("parallel"),
        name="ffn_up_sample",
    )(h2, w_up, w_up, conv_w, conv_w, cb, cb, st, st)


def _layer(x, l, w, dims, attn_fn, mstate, gstate, conv_state, bsz, seq):
    (heads_a, dk_a, heads_b, heads_c, dk_c) = dims
    wa = heads_a * LANES
    wb = heads_b * LANES
    prompt = conv_state is None
    z, zs = _proj_in(x, w["norm_mix_g"][l], w["w_main"], w["w_small"], l)
    qn, kn, knb, vb, vr = _qknorm(z, w["q_norm_g"][l], w["k_norm_g"][l], heads_a, dk_a, seq,
                                  v_transposed=prompt)
    lam_init = 0.8 - 0.6 * math.exp(-0.3 * l)
    oa = attn_fn(l, qn, knb, vb, lam_init)

    chunk_b = SCAN_CHUNK if seq % SCAN_CHUNK == 0 else SAMPLE_CHUNK
    gates = zs[:, GATE_I:GATE_I + 2 * heads_b].reshape(bsz, seq, 2 * heads_b)
    gt = jnp.swapaxes(gates, 1, 2)
    if seq < chunk_b:
        gt = jnp.pad(gt, ((0, 0), (0, 0), (0, chunk_b - seq)))
    ob, c_f, n_f, m_f = _mlstm(z, zs, gt, w["mlstm_gate_b"][l], w["mlstm_norm_g"][l],
                               mstate[0], mstate[1], mstate[2], seq, chunk_b, (3 * wa) // wb)

    chunk_c = GLA_CHUNK if seq % GLA_CHUNK == 0 else SAMPLE_CHUNK
    kw = heads_c * dk_c
    vw = heads_c * LANES
    c0 = 3 * wa + 4 * wb
    oc, s_f = _gla(z, zs, w["gla_w_alpha"][l], w["gla_b_alpha"][l], w["gla_norm_g"][l], gstate,
                   seq, chunk_c, c0 // kw, c0 // kw + 1, (c0 + 2 * kw) // vw, (c0 + 2 * kw) // vw + 1)

    x1, h2 = _proj_out(oa, ob, oc, w["w_out"], l, x, w["norm_ffn_g"][l])
    if prompt:
        act, tg, tv = _ffn_up_prompt(h2, w["w_up"], l, w["ffn_conv_w"][l], w["ffn_conv_b"][l], bsz, seq)
        conv_rows = jnp.concatenate([tg[:, 6:8], tv[:, 6:8]], axis=-1)
    else:
        act, ug, uv = _ffn_up_sample(h2, w["w_up"], l, w["ffn_conv_w"][l], w["ffn_conv_b"][l], conv_state, seq)
        u = jnp.concatenate([ug, uv], axis=-1).reshape(bsz, seq, -1)
        conv_rows = u[:, seq - 2:]
    x2 = _proj_down(act, w["w_down"], l, x1)

    k_rows = kn.reshape(bsz, seq, heads_a, LANES)
    v_rows = vr.reshape(bsz, seq, heads_a, LANES)
    return x2, (k_rows, v_rows, c_f, n_f, m_f[:, :, 0], s_f, conv_rows)


def kernel(x_prompt, x_sample, cache_k, cache_v, page_table, state_mlstm_C, state_mlstm_n, state_mlstm_m, state_gla_S, state_ffn_conv, norm_mix_g, w_in, q_norm_g, k_norm_g, diff_lambda, diff_subln_g, rel_bias, mlstm_gate_b, mlstm_norm_g, gla_w_alpha, gla_b_alpha, gla_norm_g, w_out, norm_ffn_g, ffn_w_up, ffn_conv_w, ffn_conv_b, ffn_w_down):
    depth = w_in.shape[0]
    bp, sp, d_model = x_prompt.shape
    bs, ss, _ = x_sample.shape
    heads_a, dv_a = cache_v.shape[3], cache_v.shape[4]
    dk_a = cache_k.shape[4] // 2
    heads_b, dk_b, dv_b = state_mlstm_C.shape[2:]
    heads_c, dk_c, dv_c = state_gla_S.shape[2:]
    rank = gla_w_alpha.shape[1]
    page = cache_k.shape[2]
    assert dv_a == LANES and 2 * dk_a == LANES and dk_b == LANES and dv_b == LANES and dv_c == LANES
    assert page >= MAX_DISTANCE and ATTN_TILE >= MAX_DISTANCE and rank <= GATE_I
    wa, wb = heads_a * LANES, heads_b * LANES
    n_main = 3 * wa + 4 * wb + 2 * heads_c * dk_c + 2 * heads_c * dv_c
    gate0 = 3 * wa + 4 * wb
    c0 = gate0 + 2 * heads_b
    assert w_in.shape[2] == n_main + 2 * heads_b + rank

    w_main = jnp.concatenate([w_in[:, :, :gate0], w_in[:, :, c0:c0 + n_main - gate0]], axis=-1).astype(BF16)
    w_small = jnp.zeros((depth, d_model, LANES), F32)
    w_small = w_small.at[:, :, :rank].set(w_in[:, :, n_main + 2 * heads_b:])
    w_small = w_small.at[:, :, GATE_I:GATE_I + 2 * heads_b].set(w_in[:, :, gate0:c0]).astype(BF16)
    w = dict(norm_mix_g=norm_mix_g, w_main=w_main, w_small=w_small, q_norm_g=q_norm_g, k_norm_g=k_norm_g,
             mlstm_gate_b=mlstm_gate_b, mlstm_norm_g=mlstm_norm_g, gla_w_alpha=gla_w_alpha,
             gla_b_alpha=gla_b_alpha, gla_norm_g=gla_norm_g, w_out=w_out.astype(BF16),
             norm_ffn_g=norm_ffn_g, w_up=ffn_w_up, ffn_conv_w=ffn_conv_w,
             ffn_conv_b=ffn_conv_b, w_down=ffn_w_down)
    dims = (heads_a, dk_a, heads_b, heads_c, dk_c)

    bias_p = _prompt_bias(rel_bias, ATTN_TILE)
    bias_pg, bias_new = _decode_bias(rel_bias, page, DECODE_PAGES_PER_STEP, ss)
    kc = cache_k.reshape(depth, cache_k.shape[1], page * heads_a, LANES)
    vc = cache_v.reshape(depth, cache_v.shape[1], page * heads_a, LANES)

    def prompt_attn(l, qn, knb, vt, lam_init):
        shp = (bp, sp, wa)
        o = _prompt_attention(qn.reshape(shp), knb.reshape(shp), vt, bias_p,
                              diff_lambda[l], diff_subln_g[l], lam_init, dk_a)
        return o.reshape(bp * sp, wa)

    def sample_attn(l, qn, knb, vb, lam_init):
        shp = (bs, ss, wa)
        o = _decode_attention(l, qn.reshape(shp), knb.reshape(shp), vb.reshape(shp), kc, vc, page_table,
                              bias_pg, bias_new, diff_lambda[l], diff_subln_g[l], lam_init, dk_a)
        return o.reshape(bs * ss, wa)

    xp = x_prompt.reshape(bp * sp, d_model)
    xs = x_sample.reshape(bs * ss, d_model)
    zero_m = (jnp.zeros((bp, heads_b, dk_b, dv_b), F32), jnp.zeros((bp, heads_b, dk_b), F32),
              jnp.zeros((bp, heads_b), F32))
    zero_g = jnp.zeros((bp, heads_c, dk_c, dv_c), F32)
    rows_p, rows_s = [], []
    for l in range(depth):
        xp, rp = _layer(xp, l, w, dims, prompt_attn, zero_m, zero_g, None, bp, sp)
        rows_p.append(rp)
        xs, rs = _layer(xs, l, w, dims, sample_attn,
                        (state_mlstm_C[l], state_mlstm_n[l], state_mlstm_m[l]), state_gla_S[l],
                        state_ffn_conv[l], bs, ss)
        rows_s.append(rs)

    def field(rows, i):
        return jnp.stack([r[i] for r in rows], axis=0)

    return (xp.reshape(bp, sp, d_model), xs.reshape(bs, ss, d_model),
            *[field(rows_p, i) for i in range(7)], *[field(rows_s, i) for i in range(7)])
```
